```python
import jax, jax.numpy as jnp
from jax import lax
import numpy as np

D_MODEL = 1024
BATCH = 4
SEQ = 8192
DEPTH = 4

N_MIXERS = 2
PLE_DIM = 256
EPS = 1e-6
NEG_BIG = -1e30

A_HEADS = 16
A_KV_HEADS = 4
A_GROUP = A_HEADS // A_KV_HEADS
A_HEAD_DIM = D_MODEL // A_HEADS
A_WIDTH = A_HEADS * A_HEAD_DIM
A_KV_WIDTH = A_KV_HEADS * A_HEAD_DIM
A_IN = A_WIDTH + 2 * A_KV_WIDTH + A_WIDTH
WINDOW = 128
BLOCK = 128

B_HEADS = 16
B_NOPE = 64
B_ROPE = 32
B_VDIM = 64
B_WIDTH = B_HEADS * B_VDIM
Q_LORA = 384
KV_LORA = 256
B_IN = Q_LORA + KV_LORA + B_ROPE + B_WIDTH
ROPE_THETA = 10000.0
Q_BLOCK = 128

N_A = (DEPTH + 1) // 2
N_B = DEPTH // 2

kernel_name = "hybrid_swa_sink_alibi_mla_encoder"


def rms_norm(x, g):
    xf = x.astype(jnp.float32)
    y = xf * lax.rsqrt(jnp.mean(xf * xf, axis=-1, keepdims=True) + EPS)
    return (y * g.astype(jnp.float32)).astype(x.dtype)


def alibi_slopes(n):
    return 2.0 ** (-8.0 * jnp.arange(1, n + 1, dtype=jnp.float32) / n)


def rotate(x, cos, sin):
    half = x.shape[-1] // 2
    x1, x2 = x[..., :half], x[..., half:]
    return jnp.concatenate([x1 * cos - x2 * sin, x1 * sin + x2 * cos], axis=-1)


def windowed_gqa(u, w_in, sink, w_out):
    B_, S, _ = u.shape
    nb = S // BLOCK
    proj = u @ w_in
    q, k, v, z = jnp.split(proj, [A_WIDTH, A_WIDTH + A_KV_WIDTH, A_WIDTH + 2 * A_KV_WIDTH], axis=-1)
    q = q.reshape(B_, nb, BLOCK, A_KV_HEADS, A_GROUP, A_HEAD_DIM).swapaxes(0, 1)
    k = k.reshape(B_, S, A_KV_HEADS, A_HEAD_DIM)
    v = v.reshape(B_, S, A_KV_HEADS, A_HEAD_DIM)
    pad = ((0, 0), (BLOCK, BLOCK), (0, 0), (0, 0))
    kp = jnp.pad(k, pad)
    vp = jnp.pad(v, pad)
    scale = A_HEAD_DIM ** -0.5
    slopes = alibi_slopes(A_HEADS).reshape(A_KV_HEADS, A_GROUP)
    sink_l = sink.astype(jnp.float32).reshape(A_KV_HEADS, A_GROUP)
    qi = jnp.arange(BLOCK)
    kj = jnp.arange(3 * BLOCK)
    rel = (kj[None, :] - BLOCK) - qi[:, None]
    dist = jnp.abs(rel).astype(jnp.float32)
    in_window = jnp.abs(rel) <= WINDOW

    def block_fn(args):
        q_blk, b = args
        start = b * BLOCK
        k_blk = lax.dynamic_slice_in_dim(kp, start, 3 * BLOCK, axis=1)
        v_blk = lax.dynamic_slice_in_dim(vp, start, 3 * BLOCK, axis=1)
        s_pos = start - BLOCK + kj
        valid = in_window & ((s_pos >= 0) & (s_pos < S))[None, :]
        logits = jnp.einsum('bqkgd,bskd->bkgqs', q_blk, k_blk).astype(jnp.float32) * scale
        logits = logits - slopes[:, :, None, None] * dist
        logits = jnp.where(valid, logits, NEG_BIG)
        sink_col = jnp.broadcast_to(sink_l[None, :, :, None, None], logits.shape[:-1] + (1,))
        probs = jax.nn.softmax(jnp.concatenate([logits, sink_col], axis=-1), axis=-1)[..., :-1]
        return jnp.einsum('bkgqs,bskd->bqkgd', probs.astype(v_blk.dtype), v_blk)

    o = lax.map(block_fn, (q, jnp.arange(nb)))
    o = o.swapaxes(0, 1).reshape(B_, S, A_WIDTH)
    return (o * jax.nn.silu(z)) @ w_out


def mla(u, w_in, q_norm, w_qb, kv_norm, w_kvb, w_out):
    B_, S, _ = u.shape
    nb = S // Q_BLOCK
    proj = u @ w_in
    cq, ckv, k_rope, z = jnp.split(proj, [Q_LORA, Q_LORA + KV_LORA, Q_LORA + KV_LORA + B_ROPE], axis=-1)
    q = (rms_norm(cq, q_norm) @ w_qb).reshape(B_, S, B_HEADS, B_NOPE + B_ROPE)
    q_nope, q_rope = q[..., :B_NOPE], q[..., B_NOPE:]
    kv = (rms_norm(ckv, kv_norm) @ w_kvb).reshape(B_, S, B_HEADS, B_NOPE + B_VDIM)
    k_nope, v = kv[..., :B_NOPE], kv[..., B_NOPE:]
    half = B_ROPE // 2
    inv_freq = ROPE_THETA ** (-jnp.arange(half, dtype=jnp.float32) / half)
    ang = jnp.arange(S, dtype=jnp.float32)[:, None] * inv_freq[None, :]
    cos, sin = jnp.cos(ang).astype(u.dtype), jnp.sin(ang).astype(u.dtype)
    q_rope = rotate(q_rope, cos[:, None, :], sin[:, None, :])
    k_rope = rotate(k_rope, cos, sin)
    scale = (B_NOPE + B_ROPE) ** -0.5
    qn = q_nope.reshape(B_, nb, Q_BLOCK, B_HEADS, B_NOPE).swapaxes(0, 1)
    qr = q_rope.reshape(B_, nb, Q_BLOCK, B_HEADS, B_ROPE).swapaxes(0, 1)

    def block_fn(args):
        qn_b, qr_b = args
        logits = (jnp.einsum('bqhd,bshd->bhqs', qn_b, k_nope)
                  + jnp.einsum('bqhr,bsr->bhqs', qr_b, k_rope)).astype(jnp.float32) * scale
        probs = jax.nn.softmax(logits, axis=-1)
        return jnp.einsum('bhqs,bshd->bqhd', probs.astype(v.dtype), v)

    o = lax.map(block_fn, (qn, qr)).swapaxes(0, 1).reshape(B_, S, B_WIDTH)
    return (o * jax.nn.silu(z)) @ w_out


def setup_inputs(seed: int = 0) -> dict:
    key = jax.random.key(seed)
    ks = jax.random.split(key, 18)
    f32 = jnp.float32
    nrm = lambda k, shape, s: jax.random.normal(k, shape, f32) * s
    gain = lambda k, shape: 1.0 + 0.05 * jax.random.normal(k, shape, f32)
    return {
        "x": jax.random.normal(ks[0], (BATCH, SEQ, D_MODEL), f32),
        "p": jax.random.normal(ks[1], (DEPTH, BATCH, SEQ, PLE_DIM), f32),
        "norm_g": gain(ks[2], (DEPTH, D_MODEL)),
        "a_w_in": nrm(ks[3], (N_A, D_MODEL, A_IN), D_MODEL ** -0.5),
        "a_sink": nrm(ks[4], (N_A, A_HEADS), 1.0),
        "a_w_out": nrm(ks[5], (N_A, A_WIDTH, D_MODEL), A_WIDTH ** -0.5),
        "b_w_in": nrm(ks[6], (N_B, D_MODEL, B_IN), D_MODEL ** -0.5),
        "b_q_norm": gain(ks[7], (N_B, Q_LORA)),
        "b_w_qb": nrm(ks[8], (N_B, Q_LORA, B_HEADS * (B_NOPE + B_ROPE)), Q_LORA ** -0.5),
        "b_kv_norm": gain(ks[9], (N_B, KV_LORA)),
        "b_w_kvb": nrm(ks[10], (N_B, KV_LORA, B_HEADS * (B_NOPE + B_VDIM)), KV_LORA ** -0.5),
        "b_w_out": nrm(ks[11], (N_B, B_WIDTH, D_MODEL), B_WIDTH ** -0.5),
        "ple_w": nrm(ks[12], (DEPTH, PLE_DIM, D_MODEL), PLE_DIM ** -0.5),
        "ple_norm_g": gain(ks[13], (DEPTH, D_MODEL)),
        "ple_w_gate": nrm(ks[14], (DEPTH, D_MODEL, D_MODEL), D_MODEL ** -0.5),
        "final_norm_g": gain(ks[15], (D_MODEL,)),
    }


def reference(x, p, norm_g, a_w_in, a_sink, a_w_out, b_w_in, b_q_norm, b_w_qb,
              b_kv_norm, b_w_kvb, b_w_out, ple_w, ple_norm_g, ple_w_gate, final_norm_g):
    h = x
    for i in range(DEPTH):
        u = rms_norm(h, norm_g[i])
        j = i // N_MIXERS
        if i % N_MIXERS == 0:
            y = windowed_gqa(u, a_w_in[j], a_sink[j], a_w_out[j])
        else:
            y = mla(u, b_w_in[j], b_q_norm[j], b_w_qb[j], b_kv_norm[j], b_w_kvb[j], b_w_out[j])
        h = h + y
        gate = jax.nn.sigmoid(rms_norm(h, ple_norm_g[i]) @ ple_w_gate[i])
        h = h + (p[i] @ ple_w[i]) * gate
    return rms_norm(h, final_norm_g)
```

```python
import functools
import math

import jax
import jax.numpy as jnp
from jax import lax
from jax.experimental import pallas as pl
from jax.experimental.pallas import tpu as pltpu

F32 = jnp.float32
BF16 = jnp.bfloat16

EPS = 1e-6
NEG_BIG = -1e30
LOG2E = 1.4426950408889634

A_HEADS = 16
A_KV_HEADS = 4
A_GROUP = A_HEADS // A_KV_HEADS
A_HEAD_DIM = 64
A_WIDTH = A_HEADS * A_HEAD_DIM
A_KV_WIDTH = A_KV_HEADS * A_HEAD_DIM
WINDOW = 128
BLOCK = 128

B_HEADS = 16
B_NOPE = 64
B_ROPE = 32
B_VDIM = 64
B_WIDTH = B_HEADS * B_VDIM
Q_LORA = 384
KV_LORA = 256
ROPE_THETA = 10000.0
B_QK_PAD = 128
B_V_ROWS = 80
B_HEAD_PAIR = 2

ROW_TILE = 512
MLA_Q_BLOCK = 1024
MLA_Q_TILE = 256
MLA_KV_TILE = 512

VMEM_LIMIT = 56 * 1024 * 1024


def _rms(x, g):
    ms = jnp.mean(x * x, axis=-1, keepdims=True)
    return x * lax.rsqrt(ms + EPS) * g


def _dot(a, b):
    return jnp.dot(a, b, preferred_element_type=F32)


def _dot_nt(a, b):
    return lax.dot_general(a, b, (((1,), (1,)), ((), ())), preferred_element_type=F32)


def _params(*sem):
    return pltpu.CompilerParams(dimension_semantics=sem, vmem_limit_bytes=VMEM_LIMIT)


def _const_spec(shape):
    nd = len(shape)
    return pl.BlockSpec(shape, lambda *_: (0,) * nd)


def _pre_a_kernel(x_ref, g_ref, w_ref, q_ref, k_ref, v_ref, z_ref):
    u = _rms(x_ref[0], g_ref[...]).astype(BF16)
    qs = (A_HEAD_DIM ** -0.5) * LOG2E
    o1, o2, o3 = A_WIDTH, A_WIDTH + A_KV_WIDTH, A_WIDTH + 2 * A_KV_WIDTH
    q_ref[0] = (_dot(u, w_ref[:, :o1]) * qs).astype(BF16)
    k_ref[0] = _dot(u, w_ref[:, o1:o2]).astype(BF16)
    v_ref[0] = _dot(u, w_ref[:, o2:o3]).astype(BF16)
    z_ref[0] = _dot(u, w_ref[:, o3:])


def _pre_a(h, g, w_in):
    b, s, d = h.shape
    tm = min(ROW_TILE, s)
    n_in = w_in.shape[1]
    row = lambda w: pl.BlockSpec((1, tm, w), lambda bi, i: (bi, i, 0))
    return pl.pallas_call(
        _pre_a_kernel,
        grid=(b, s // tm),
        in_specs=[row(d), _const_spec((1, d)), _const_spec((d, n_in))],
        out_specs=[row(A_WIDTH), row(A_KV_WIDTH), row(A_KV_WIDTH), row(A_WIDTH)],
        out_shape=[
            jax.ShapeDtypeStruct((b, s, A_WIDTH), BF16),
            jax.ShapeDtypeStruct((b, s, A_KV_WIDTH), BF16),
            jax.ShapeDtypeStruct((b, s, A_KV_WIDTH), BF16),
            jax.ShapeDtypeStruct((b, s, A_WIDTH), F32),
        ],
        compiler_params=_params("parallel", "parallel"),
    )(h, g.reshape(1, d), w_in.astype(BF16))


def _win_kernel(sink_ref, q_ref, kp_ref, kc_ref, kn_ref, vp_ref, vc_ref, vn_ref, z_ref, o_ref, *, seq):
    i = pl.program_id(1)
    kcat = jnp.concatenate([kp_ref[0], kc_ref[0], kn_ref[0]], axis=0)
    vcat = jnp.concatenate([vp_ref[0], vc_ref[0], vn_ref[0]], axis=0)
    row = lax.broadcasted_iota(jnp.int32, (BLOCK, 3 * BLOCK), 0)
    col = lax.broadcasted_iota(jnp.int32, (BLOCK, 3 * BLOCK), 1)
    rel = col - BLOCK - row
    s_pos = i * BLOCK - BLOCK + col
    valid = (jnp.abs(rel) <= WINDOW) & (s_pos >= 0) & (s_pos < seq)
    dist = jnp.abs(rel).astype(F32)
    outs = []
    for h in range(A_HEADS):
        kv = h // A_GROUP
        slope = 2.0 ** (-8.0 * (h + 1) / A_HEADS) * LOG2E
        qh = q_ref[0, :, h * A_HEAD_DIM:(h + 1) * A_HEAD_DIM]
        kh = kcat[:, kv * A_HEAD_DIM:(kv + 1) * A_HEAD_DIM]
        vh = vcat[:, kv * A_HEAD_DIM:(kv + 1) * A_HEAD_DIM]
        logits = _dot_nt(qh, kh) - slope * dist
        logits = jnp.where(valid, logits, NEG_BIG)
        sink = sink_ref[h] * LOG2E
        m = jnp.maximum(jnp.max(logits, axis=-1, keepdims=True), sink)
        p = jnp.exp2(logits - m)
        denom = jnp.sum(p, axis=-1, keepdims=True) + jnp.exp2(sink - m)
        outs.append(_dot(p.astype(BF16), vh) / denom)
    o = jnp.concatenate(outs, axis=-1)
    z = z_ref[0]
    o_ref[0] = (o * (z * jax.nn.sigmoid(z))).astype(BF16)


def _win_attn(q, k, v, z, sink):
    b, s, _ = q.shape
    nb = s // BLOCK
    cur = lambda w: pl.BlockSpec((1, BLOCK, w), lambda bi, i: (bi, i, 0))
    prev = lambda w: pl.BlockSpec((1, BLOCK, w), lambda bi, i: (bi, jnp.maximum(i - 1, 0), 0))
    nxt = lambda w: pl.BlockSpec((1, BLOCK, w), lambda bi, i: (bi, jnp.minimum(i + 1, nb - 1), 0))
    kw = A_KV_WIDTH
    return pl.pallas_call(
        functools.partial(_win_kernel, seq=s),
        grid=(b, nb),
        in_specs=[pl.BlockSpec(memory_space=pltpu.SMEM), cur(A_WIDTH),
                  prev(kw), cur(kw), nxt(kw), prev(kw), cur(kw), nxt(kw), cur(A_WIDTH)],
        out_specs=cur(A_WIDTH),
        out_shape=jax.ShapeDtypeStruct((b, s, A_WIDTH), BF16),
        compiler_params=_params("parallel", "parallel"),
    )(sink.astype(F32), q, k, k, k, v, v, v, z)


def _post_kernel(og_ref, h_ref, p_ref, wo_ref, pg_ref, wg_ref, pw_ref, fg_ref, o_ref, *, final):
    h1 = h_ref[0] + _dot(og_ref[0], wo_ref[...])
    gate = jax.nn.sigmoid(_dot(_rms(h1, pg_ref[...]).astype(BF16), wg_ref[...]))
    h2 = h1 + _dot(p_ref[0].astype(BF16), pw_ref[...]) * gate
    if final:
        h2 = _rms(h2, fg_ref[...])
    o_ref[0] = h2


def _post(og, h, p, w_out, ple_g, w_gate, ple_w, final_g, final):
    b, s, d = h.shape
    tm = min(ROW_TILE, s)
    pd = p.shape[-1]
    row = lambda w: pl.BlockSpec((1, tm, w), lambda bi, i: (bi, i, 0))
    return pl.pallas_call(
        functools.partial(_post_kernel, final=final),
        grid=(b, s // tm),
        in_specs=[row(og.shape[-1]), row(d), row(pd), _const_spec(w_out.shape), _const_spec((1, d)),
                  _const_spec((d, d)), _const_spec((pd, d)), _const_spec((1, d))],
        out_specs=row(d),
        out_shape=jax.ShapeDtypeStruct((b, s, d), F32),
        compiler_params=_params("parallel", "parallel"),
    )(og, h, p, w_out.astype(BF16), ple_g.reshape(1, d), w_gate.astype(BF16), ple_w.astype(BF16),
      final_g.reshape(1, d))


def _pre_b_kernel(x_ref, g_ref, w_ref, qn_ref, wq_ref, kn_ref, wk_ref, wvt_ref, vb_ref, r_ref,
                  q_ref, k_ref, vt_ref, z_ref):
    u = _rms(x_ref[0], g_ref[...]).astype(BF16)
    o1, o2, o3 = Q_LORA, Q_LORA + KV_LORA, Q_LORA + KV_LORA + B_QK_PAD
    rot = r_ref[...]
    cq = _rms(_dot(u, w_ref[:, :o1]), qn_ref[...]).astype(BF16)
    qs = ((B_NOPE + B_ROPE) ** -0.5) * LOG2E
    q = _dot(cq, wq_ref[...]) * jnp.tile(rot * qs, (1, B_HEADS))
    q_ref[0] = q.astype(BF16)
    ckv = _rms(_dot(u, w_ref[:, o1:o2]), kn_ref[...]).astype(BF16)
    y = _dot(u, w_ref[:, o2:o3]) * rot
    lane = lax.broadcasted_iota(jnp.int32, y.shape, 1)
    both = pltpu.roll(y, 32, 1) + pltpu.roll(y, 96, 1)
    kr = y + jnp.where(lane >= B_NOPE, both, 0.0)
    k_ref[0] = (_dot(ckv, wk_ref[...]) + jnp.tile(kr, (1, B_HEADS))).astype(BF16)
    vt_ref[0, 0] = (_dot_nt(wvt_ref[...], ckv) + vb_ref[...]).astype(BF16)
    z_ref[0] = _dot(u, w_ref[:, o3:])


def _pre_b(h, g, w_in_p, q_norm, wq_p, kv_norm, wk_p, wvt_p, vbias, rot):
    b, s, d = h.shape
    tm = min(MLA_KV_TILE, s)
    row = lambda w: pl.BlockSpec((1, tm, w), lambda bi, i: (bi, i, 0))
    vrows = B_HEADS * B_V_ROWS
    return pl.pallas_call(
        _pre_b_kernel,
        grid=(b, s // tm),
        in_specs=[row(d), _const_spec((1, d)), _const_spec(w_in_p.shape), _const_spec((1, Q_LORA)),
                  _const_spec(wq_p.shape), _const_spec((1, KV_LORA)), _const_spec(wk_p.shape),
                  _const_spec(wvt_p.shape), _const_spec((vrows, 1)),
                  pl.BlockSpec((tm, B_QK_PAD), lambda bi, i: (i, 0))],
        out_specs=[row(B_HEADS * B_QK_PAD), row(B_HEADS * B_QK_PAD),
                   pl.BlockSpec((1, 1, vrows, tm), lambda bi, i: (bi, i, 0, 0)), row(B_WIDTH)],
        out_shape=[
            jax.ShapeDtypeStruct((b, s, B_HEADS * B_QK_PAD), BF16),
            jax.ShapeDtypeStruct((b, s, B_HEADS * B_QK_PAD), BF16),
            jax.ShapeDtypeStruct((b, s // tm, vrows, tm), BF16),
            jax.ShapeDtypeStruct((b, s, B_WIDTH), F32),
        ],
        compiler_params=_params("parallel", "parallel"),
    )(h, g.reshape(1, d), w_in_p, q_norm.reshape(1, Q_LORA), wq_p, kv_norm.reshape(1, KV_LORA), wk_p,
      wvt_p, vbias, rot)


def _prep_b_weights(w_in, w_qb, w_kvb):
    half = B_ROPE // 2
    d = w_in.shape[0]
    o1, o2, o3 = Q_LORA, Q_LORA + KV_LORA, Q_LORA + KV_LORA + B_ROPE
    kr = w_in[:, o2:o3]
    kr_swapped = jnp.concatenate([kr[:, half:], kr[:, :half]], axis=1)
    kr_block = jnp.concatenate([jnp.zeros((d, B_NOPE), w_in.dtype), kr, kr_swapped], axis=1)
    w_in_p = jnp.concatenate([w_in[:, :o2], kr_block, w_in[:, o3:]], axis=1).astype(BF16)

    wq = w_qb.reshape(Q_LORA, B_HEADS, B_NOPE + B_ROPE)
    q_rope = wq[:, :, B_NOPE:]
    q_rope_swapped = jnp.concatenate([q_rope[:, :, half:], q_rope[:, :, :half]], axis=2)
    wq_p = jnp.concatenate([wq[:, :, :B_NOPE], q_rope, q_rope_swapped], axis=2)
    wq_p = wq_p.reshape(Q_LORA, B_HEADS * B_QK_PAD).astype(BF16)

    wkv = w_kvb.reshape(KV_LORA, B_HEADS, B_NOPE + B_VDIM)
    wk_p = jnp.concatenate([wkv[:, :, :B_NOPE], jnp.zeros((KV_LORA, B_HEADS, B_QK_PAD - B_NOPE), w_kvb.dtype)],
                           axis=2).reshape(KV_LORA, B_HEADS * B_QK_PAD).astype(BF16)
    wv = jnp.concatenate([wkv[:, :, B_NOPE:], jnp.zeros((KV_LORA, B_HEADS, B_V_ROWS - B_VDIM), w_kvb.dtype)],
                         axis=2).reshape(KV_LORA, B_HEADS * B_V_ROWS)
    wvt_p = wv.T.astype(BF16)
    vbias = jnp.tile(jnp.concatenate([jnp.zeros((B_VDIM,), F32), jnp.ones((B_V_ROWS - B_VDIM,), F32)]),
                     B_HEADS).reshape(B_HEADS * B_V_ROWS, 1)
    return w_in_p, wq_p, wk_p, wvt_p, vbias


def _rope_table(s):
    half = B_ROPE // 2
    inv_freq = ROPE_THETA ** (-jnp.arange(half, dtype=F32) / half)
    ang = jnp.arange(s, dtype=F32)[:, None] * inv_freq[None, :]
    cos, sin = jnp.cos(ang), jnp.sin(ang)
    return jnp.concatenate([jnp.ones((s, B_NOPE), F32), cos, cos, -sin, sin], axis=1)


def _mla_kernel(q_ref, k_ref, vt_ref, z_ref, o_ref, *, q_tile, kv_tile):
    n_kv = vt_ref.shape[1]
    q_block = q_ref.shape[1]
    for qs in range(q_block // q_tile):
        rows = slice(qs * q_tile, (qs + 1) * q_tile)
        qh = [q_ref[0, rows, hh * B_QK_PAD:(hh + 1) * B_QK_PAD] for hh in range(B_HEAD_PAIR)]

        def body(j, carry):
            start = pl.multiple_of(j * kv_tile, kv_tile)
            new = []
            for hh in range(B_HEAD_PAIR):
                m, acc = carry[2 * hh], carry[2 * hh + 1]
                kc = k_ref[0, pl.ds(start, kv_tile), hh * B_QK_PAD:(hh + 1) * B_QK_PAD]
                s = _dot_nt(kc, qh[hh])
                m_new = jnp.maximum(m, jnp.max(s, axis=0, keepdims=True))
                p = jnp.exp2(s - m_new).astype(BF16)
                vt = vt_ref[0, j, hh * B_V_ROWS:(hh + 1) * B_V_ROWS, :]
                acc = acc * jnp.exp2(m - m_new) + _dot(vt, p)
                new += [m_new, acc]
            return tuple(new)

        init = (jnp.full((1, q_tile), NEG_BIG, F32), jnp.zeros((B_V_ROWS, q_tile), F32)) * B_HEAD_PAIR
        res = lax.fori_loop(0, n_kv, body, init)
        o_t = jnp.concatenate(
            [res[2 * hh + 1][:B_VDIM] / res[2 * hh + 1][B_VDIM:B_VDIM + 1] for hh in range(B_HEAD_PAIR)], axis=0)
        z = z_ref[0, rows, :]
        o_ref[0, rows, :] = (o_t.T * (z * jax.nn.sigmoid(z))).astype(BF16)


def _mla_attn(q, k, vt, z):
    b, s, _ = q.shape
    n_kv, kv_tile = vt.shape[1], vt.shape[3]
    q_block = min(MLA_Q_BLOCK, s)
    q_tile = min(MLA_Q_TILE, q_block)
    pair_w = B_HEAD_PAIR * B_QK_PAD
    out_w = B_HEAD_PAIR * B_VDIM
    return pl.pallas_call(
        functools.partial(_mla_kernel, q_tile=q_tile, kv_tile=kv_tile),
        grid=(b, B_HEADS // B_HEAD_PAIR, s // q_block),
        in_specs=[pl.BlockSpec((1, q_block, pair_w), lambda bi, hp, qi: (bi, qi, hp)),
                  pl.BlockSpec((1, s, pair_w), lambda bi, hp, qi: (bi, 0, hp)),
                  pl.BlockSpec((1, n_kv, B_HEAD_PAIR * B_V_ROWS, kv_tile), lambda bi, hp, qi: (bi, 0, hp, 0)),
                  pl.BlockSpec((1, q_block, out_w), lambda bi, hp, qi: (bi, qi, hp))],
        out_specs=pl.BlockSpec((1, q_block, out_w), lambda bi, hp, qi: (bi, qi, hp)),
        out_shape=jax.ShapeDtypeStruct((b, s, B_WIDTH), BF16),
        compiler_params=_params("parallel", "parallel", "arbitrary"),
    )(q, k, vt, z)


def kernel(x, p, norm_g, a_w_in, a_sink, a_w_out, b_w_in, b_q_norm, b_w_qb, b_kv_norm, b_w_kvb, b_w_out,
           ple_w, ple_norm_g, ple_w_gate, final_norm_g):
    depth = p.shape[0]
    s = x.shape[1]
    rot = _rope_table(s)
    h = x
    for i in range(depth):
        j = i // 2
        if i % 2 == 0:
            q, k, v, z = _pre_a(h, norm_g[i], a_w_in[j])
            og = _win_attn(q, k, v, z, a_sink[j])
            w_out = a_w_out[j]
        else:
            w_in_p, wq_p, wk_p, wvt_p, vbias = _prep_b_weights(b_w_in[j], b_w_qb[j], b_w_kvb[j])
            q, k, vt, z = _pre_b(h, norm_g[i], w_in_p, b_q_norm[j], wq_p, b_kv_norm[j], wk_p, wvt_p, vbias, rot)
            og = _mla_attn(q, k, vt, z)
            w_out = b_w_out[j]
        h = _post(og, h, p[i], w_out, ple_norm_g[i], ple_w_gate[i], ple_w[i], final_norm_g, final=(i == depth - 1))
    return h
```

```python
import functools
import math

import jax
import jax.numpy as jnp
from jax import lax
from jax.experimental import pallas as pl
from jax.experimental.pallas import tpu as pltpu

F32 = jnp.float32
BF16 = jnp.bfloat16

EPS = 1e-6
NEG_BIG = -1e30
LOG2E = 1.4426950408889634

A_HEADS = 16
A_KV_HEADS = 4
A_GROUP = A_HEADS // A_KV_HEADS
A_HEAD_DIM = 64
A_WIDTH = A_HEADS * A_HEAD_DIM
A_KV_WIDTH = A_KV_HEADS * A_HEAD_DIM
WINDOW = 128
BLOCK = 128

B_HEADS = 16
B_NOPE = 64
B_ROPE = 32
B_VDIM = 64
B_WIDTH = B_HEADS * B_VDIM
Q_LORA = 384
KV_LORA = 256
ROPE_THETA = 10000.0
B_QK_PAD = 128
B_V_ROWS = 80
B_HEAD_PAIR = 2

ROW_TILE = 512
MLA_Q_BLOCK = 1024
MLA_Q_TILE = 256
MLA_KV_TILE = 512

VMEM_LIMIT = 56 * 1024 * 1024


def _rms(x, g):
    ms = jnp.mean(x * x, axis=-1, keepdims=True)
    return x * lax.rsqrt(ms + EPS) * g


def _dot(a, b):
    return jnp.dot(a, b, preferred_element_type=F32)


def _dot_nt(a, b):
    return lax.dot_general(a, b, (((1,), (1,)), ((), ())), preferred_element_type=F32)


def _params(*sem):
    return pltpu.CompilerParams(dimension_semantics=sem, vmem_limit_bytes=VMEM_LIMIT)


def _const_spec(shape):
    nd = len(shape)
    return pl.BlockSpec(shape, lambda *_: (0,) * nd)


def _pre_a_kernel(x_ref, g_ref, w_ref, q_ref, k_ref, v_ref, z_ref):
    u = _rms(x_ref[0], g_ref[...]).astype(BF16)
    qs = (A_HEAD_DIM ** -0.5) * LOG2E
    o1, o2, o3 = A_WIDTH, A_WIDTH + A_KV_WIDTH, A_WIDTH + 2 * A_KV_WIDTH
    q_ref[0] = (_dot(u, w_ref[:, :o1]) * qs).astype(BF16)
    k_ref[0] = _dot(u, w_ref[:, o1:o2]).astype(BF16)
    v_ref[0] = _dot(u, w_ref[:, o2:o3]).astype(BF16)
    z_ref[0] = _dot(u, w_ref[:, o3:])


def _pre_a(h, g, w_in):
    b, s, d = h.shape
    tm = min(ROW_TILE, s)
    n_in = w_in.shape[1]
    row = lambda w: pl.BlockSpec((1, tm, w), lambda bi, i: (bi, i, 0))
    return pl.pallas_call(
        _pre_a_kernel,
        grid=(b, s // tm),
        in_specs=[row(d), _const_spec((1, d)), _const_spec((d, n_in))],
        out_specs=[row(A_WIDTH), row(A_KV_WIDTH), row(A_KV_WIDTH), row(A_WIDTH)],
        out_shape=[
            jax.ShapeDtypeStruct((b, s, A_WIDTH), BF16),
            jax.ShapeDtypeStruct((b, s, A_KV_WIDTH), BF16),
            jax.ShapeDtypeStruct((b, s, A_KV_WIDTH), BF16),
            jax.ShapeDtypeStruct((b, s, A_WIDTH), F32),
        ],
        compiler_params=_params("parallel", "parallel"),
    )(h, g.reshape(1, d), w_in.astype(BF16))


def _win_kernel(sink_ref, q_ref, kp_ref, kc_ref, kn_ref, vp_ref, vc_ref, vn_ref, z_ref, o_ref, *, seq):
    i = pl.program_id(1)
    kcat = jnp.concatenate([kp_ref[0], kc_ref[0], kn_ref[0]], axis=0)
    vcat = jnp.concatenate([vp_ref[0], vc_ref[0], vn_ref[0]], axis=0)
    row = lax.broadcasted_iota(jnp.int32, (BLOCK, 3 * BLOCK), 0)
    col = lax.broadcasted_iota(jnp.int32, (BLOCK, 3 * BLOCK), 1)
    rel = col - BLOCK - row
    s_pos = i * BLOCK - BLOCK + col
    valid = (jnp.abs(rel) <= WINDOW) & (s_pos >= 0) & (s_pos < seq)
    dist = jnp.abs(rel).astype(F32)
    outs = []
    for h in range(A_HEADS):
        kv = h // A_GROUP
        slope = 2.0 ** (-8.0 * (h + 1) / A_HEADS) * LOG2E
        qh = q_ref[0, :, h * A_HEAD_DIM:(h + 1) * A_HEAD_DIM]
        kh = kcat[:, kv * A_HEAD_DIM:(kv + 1) * A_HEAD_DIM]
        vh = vcat[:, kv * A_HEAD_DIM:(kv + 1) * A_HEAD_DIM]
        logits = _dot_nt(qh, kh) - slope * dist
        logits = jnp.where(valid, logits, NEG_BIG)
        sink = sink_ref[h] * LOG2E
        m = jnp.maximum(jnp.max(logits, axis=-1, keepdims=True), sink)
        p = jnp.exp2(logits - m)
        denom = jnp.sum(p, axis=-1, keepdims=True) + jnp.exp2(sink - m)
        outs.append(_dot(p.astype(BF16), vh) / denom)
    o = jnp.concatenate(outs, axis=-1)
    z = z_ref[0]
    o_ref[0] = (o * (z * jax.nn.sigmoid(z))).astype(BF16)


def _win_attn(q, k, v, z, sink):
    b, s, _ = q.shape
    nb = s // BLOCK
    cur = lambda w: pl.BlockSpec((1, BLOCK, w), lambda bi, i: (bi, i, 0))
    prev = lambda w: pl.BlockSpec((1, BLOCK, w), lambda bi, i: (bi, jnp.maximum(i - 1, 0), 0))
    nxt = lambda w: pl.BlockSpec((1, BLOCK, w), lambda bi, i: (bi, jnp.minimum(i + 1, nb - 1), 0))
    kw = A_KV_WIDTH
    return pl.pallas_call(
        functools.partial(_win_kernel, seq=s),
        grid=(b, nb),
        in_specs=[pl.BlockSpec(memory_space=pltpu.SMEM), cur(A_WIDTH),
                  prev(kw), cur(kw), nxt(kw), prev(kw), cur(kw), nxt(kw), cur(A_WIDTH)],
        out_specs=cur(A_WIDTH),
        out_shape=jax.ShapeDtypeStruct((b, s, A_WIDTH), BF16),
        compiler_params=_params("parallel", "parallel"),
    )(sink.astype(F32), q, k, k, k, v, v, v, z)


def _post_kernel(og_ref, h_ref, p_ref, wo_ref, pg_ref, wg_ref, pw_ref, fg_ref, o_ref, *, final):
    h1 = h_ref[0] + _dot(og_ref[0], wo_ref[...])
    gate = jax.nn.sigmoid(_dot(_rms(h1, pg_ref[...]).astype(BF16), wg_ref[...]))
    h2 = h1 + _dot(p_ref[0].astype(BF16), pw_ref[...]) * gate
    if final:
        h2 = _rms(h2, fg_ref[...])
    o_ref[0] = h2


def _post(og, h, p, w_out, ple_g, w_gate, ple_w, final_g, final):
    b, s, d = h.shape
    tm = min(ROW_TILE, s)
    pd = p.shape[-1]
    row = lambda w: pl.BlockSpec((1, tm, w), lambda bi, i: (bi, i, 0))
    return pl.pallas_call(
        functools.partial(_post_kernel, final=final),
        grid=(b, s // tm),
        in_specs=[row(og.shape[-1]), row(d), row(pd), _const_spec(w_out.shape), _const_spec((1, d)),
                  _const_spec((d, d)), _const_spec((pd, d)), _const_spec((1, d))],
        out_specs=row(d),
        out_shape=jax.ShapeDtypeStruct((b, s, d), F32),
        compiler_params=_params("parallel", "parallel"),
    )(og, h, p, w_out.astype(BF16), ple_g.reshape(1, d), w_gate.astype(BF16), ple_w.astype(BF16),
      final_g.reshape(1, d))


def _pre_b_kernel(x_ref, g_ref, w_ref, qn_ref, wq_ref, kn_ref, wk_ref, wvt_ref, vb_ref, r_ref,
                  q_ref, k_ref, vt_ref, z_ref):
    u = _rms(x_ref[0], g_ref[...]).astype(BF16)
    o1, o2, o3 = Q_LORA, Q_LORA + KV_LORA, Q_LORA + KV_LORA + B_QK_PAD
    rot = r_ref[...]
    cq = _rms(_dot(u, w_ref[:, :o1]), qn_ref[...]).astype(BF16)
    qs = ((B_NOPE + B_ROPE) ** -0.5) * LOG2E
    q = _dot(cq, wq_ref[...]) * jnp.tile(rot * qs, (1, B_HEADS))
    q_ref[0] = q.astype(BF16)
    ckv = _rms(_dot(u, w_ref[:, o1:o2]), kn_ref[...]).astype(BF16)
    y = _dot(u, w_ref[:, o2:o3]) * rot
    lane = lax.broadcasted_iota(jnp.int32, y.shape, 1)
    both = pltpu.roll(y, 32, 1) + pltpu.roll(y, 96, 1)
    kr = y + jnp.where(lane >= B_NOPE, both, 0.0)
    k_ref[0] = (_dot(ckv, wk_ref[...]) + jnp.tile(kr, (1, B_HEADS))).astype(BF16)
    vt_ref[0, 0] = (_dot_nt(wvt_ref[...], ckv) + vb_ref[...]).astype(BF16)
    z_ref[0] = _dot(u, w_ref[:, o3:])


def _pre_b(h, g, w_in_p, q_norm, wq_p, kv_norm, wk_p, wvt_p, vbias, rot):
    b, s, d = h.shape
    tm = min(MLA_KV_TILE, s)
    row = lambda w: pl.BlockSpec((1, tm, w), lambda bi, i: (bi, i, 0))
    vrows = B_HEADS * B_V_ROWS
    return pl.pallas_call(
        _pre_b_kernel,
        grid=(b, s // tm),
        in_specs=[row(d), _const_spec((1, d)), _const_spec(w_in_p.shape), _const_spec((1, Q_LORA)),
                  _const_spec(wq_p.shape), _const_spec((1, KV_LORA)), _const_spec(wk_p.shape),
                  _const_spec(wvt_p.shape), _const_spec((vrows, 1)),
                  pl.BlockSpec((tm, B_QK_PAD), lambda bi, i: (i, 0))],
        out_specs=[row(B_HEADS * B_QK_PAD), row(B_HEADS * B_QK_PAD),
                   pl.BlockSpec((1, 1, vrows, tm), lambda bi, i: (bi, i, 0, 0)), row(B_WIDTH)],
        out_shape=[
            jax.ShapeDtypeStruct((b, s, B_HEADS * B_QK_PAD), BF16),
            jax.ShapeDtypeStruct((b, s, B_HEADS * B_QK_PAD), BF16),
            jax.ShapeDtypeStruct((b, s // tm, vrows, tm), BF16),
            jax.ShapeDtypeStruct((b, s, B_WIDTH), F32),
        ],
        compiler_params=_params("parallel", "parallel"),
    )(h, g.reshape(1, d), w_in_p, q_norm.reshape(1, Q_LORA), wq_p, kv_norm.reshape(1, KV_LORA), wk_p,
      wvt_p, vbias, rot)


def _prep_b_weights(w_in, w_qb, w_kvb):
    half = B_ROPE // 2
    d = w_in.shape[0]
    o1, o2, o3 = Q_LORA, Q_LORA + KV_LORA, Q_LORA + KV_LORA + B_ROPE
    kr = w_in[:, o2:o3]
    kr_swapped = jnp.concatenate([kr[:, half:], kr[:, :half]], axis=1)
    kr_block = jnp.concatenate([jnp.zeros((d, B_NOPE), w_in.dtype), kr, kr_swapped], axis=1)
    w_in_p = jnp.concatenate([w_in[:, :o2], kr_block, w_in[:, o3:]], axis=1).astype(BF16)

    wq = w_qb.reshape(Q_LORA, B_HEADS, B_NOPE + B_ROPE)
    q_rope = wq[:, :, B_NOPE:]
    q_rope_swapped = jnp.concatenate([q_rope[:, :, half:], q_rope[:, :, :half]], axis=2)
    wq_p = jnp.concatenate([wq[:, :, :B_NOPE], q_rope, q_rope_swapped], axis=2)
    wq_p = wq_p.reshape(Q_LORA, B_HEADS * B_QK_PAD).astype(BF16)

    wkv = w_kvb.reshape(KV_LORA, B_HEADS, B_NOPE + B_VDIM)
    wk_p = jnp.concatenate([wkv[:, :, :B_NOPE], jnp.zeros((KV_LORA, B_HEADS, B_QK_PAD - B_NOPE), w_kvb.dtype)],
                           axis=2).reshape(KV_LORA, B_HEADS * B_QK_PAD).astype(BF16)
    wv = jnp.concatenate([wkv[:, :, B_NOPE:], jnp.zeros((KV_LORA, B_HEADS, B_V_ROWS - B_VDIM), w_kvb.dtype)],
                         axis=2).reshape(KV_LORA, B_HEADS * B_V_ROWS)
    wvt_p = wv.T.astype(BF16)
    vbias = jnp.tile(jnp.concatenate([jnp.zeros((B_VDIM,), F32), jnp.ones((B_V_ROWS - B_VDIM,), F32)]),
                     B_HEADS).reshape(B_HEADS * B_V_ROWS, 1)
    return w_in_p, wq_p, wk_p, wvt_p, vbias


def _rope_table(s):
    half = B_ROPE // 2
    inv_freq = ROPE_THETA ** (-jnp.arange(half, dtype=F32) / half)
    ang = jnp.arange(s, dtype=F32)[:, None] * inv_freq[None, :]
    cos, sin = jnp.cos(ang), jnp.sin(ang)
    return jnp.concatenate([jnp.ones((s, B_NOPE), F32), cos, cos, -sin, sin], axis=1)


def _mla_kernel(q_ref, k_ref, vt_ref, z_ref, o_ref, s0_ref, s1_ref, p0_ref, p1_ref, *, q_tile, kv_tile):
    n_kv = vt_ref.shape[1]
    q_block = q_ref.shape[1]
    s_bufs, p_bufs = (s0_ref, s1_ref), (p0_ref, p1_ref)
    heads = range(B_HEAD_PAIR)
    for qs in range(q_block // q_tile):
        rows = slice(qs * q_tile, (qs + 1) * q_tile)
        qh = [q_ref[0, rows, hh * B_QK_PAD:(hh + 1) * B_QK_PAD] for hh in heads]

        def scores(j, slot):
            start = pl.multiple_of(j * kv_tile, kv_tile)
            tmax = []
            for hh in heads:
                kc = k_ref[0, pl.ds(start, kv_tile), hh * B_QK_PAD:(hh + 1) * B_QK_PAD]
                s = _dot_nt(kc, qh[hh])
                s_bufs[slot][hh] = s
                tmax.append(jnp.max(s, axis=0, keepdims=True))
            return tuple(tmax)

        def exps(slot, m, tmax):
            m_out, alpha = [], []
            for hh in heads:
                m_new = jnp.maximum(m[hh], tmax[hh])
                p_bufs[slot][hh] = jnp.exp2(s_bufs[slot][hh] - m_new).astype(BF16)
                alpha.append(jnp.exp2(m[hh] - m_new))
                m_out.append(m_new)
            return tuple(m_out), tuple(alpha)

        def values(j, slot, acc, alpha):
            out = []
            for hh in heads:
                vt = vt_ref[0, j, hh * B_V_ROWS:(hh + 1) * B_V_ROWS, :]
                out.append(acc[hh] * alpha[hh] + _dot(vt, p_bufs[slot][hh]))
            return tuple(out)

        m = (jnp.full((1, q_tile), NEG_BIG, F32),) * B_HEAD_PAIR
        acc = (jnp.zeros((B_V_ROWS, q_tile), F32),) * B_HEAD_PAIR
        tmax = scores(0, 0)
        m, alpha = exps(0, m, tmax)
        tmax = scores(1, 1)

        def body(i, carry):
            m, alpha, tmax, acc = carry
            t = 2 * i + 2
            acc = values(t - 2, 0, acc, alpha)
            m, alpha = exps(1, m, tmax)
            tmax = scores(t, 0)
            acc = values(t - 1, 1, acc, alpha)
            m, alpha = exps(0, m, tmax)
            tmax = scores(t + 1, 1)
            return m, alpha, tmax, acc

        m, alpha, tmax, acc = lax.fori_loop(0, (n_kv - 2) // 2, body, (m, alpha, tmax, acc))
        acc = values(n_kv - 2, 0, acc, alpha)
        m, alpha = exps(1, m, tmax)
        acc = values(n_kv - 1, 1, acc, alpha)
        o_t = jnp.concatenate([acc[hh][:B_VDIM] / acc[hh][B_VDIM:B_VDIM + 1] for hh in heads], axis=0)
        z = z_ref[0, rows, :]
        o_ref[0, rows, :] = (o_t.T * (z * jax.nn.sigmoid(z))).astype(BF16)


def _mla_attn(q, k, vt, z):
    b, s, _ = q.shape
    n_kv, kv_tile = vt.shape[1], vt.shape[3]
    assert n_kv >= 2 and n_kv % 2 == 0
    q_block = min(MLA_Q_BLOCK, s)
    q_tile = min(MLA_Q_TILE, q_block)
    pair_w = B_HEAD_PAIR * B_QK_PAD
    out_w = B_HEAD_PAIR * B_VDIM
    s_buf = pltpu.VMEM((B_HEAD_PAIR, kv_tile, q_tile), F32)
    p_buf = pltpu.VMEM((B_HEAD_PAIR, kv_tile, q_tile), BF16)
    return pl.pallas_call(
        functools.partial(_mla_kernel, q_tile=q_tile, kv_tile=kv_tile),
        grid=(b, B_HEADS // B_HEAD_PAIR, s // q_block),
        in_specs=[pl.BlockSpec((1, q_block, pair_w), lambda bi, hp, qi: (bi, qi, hp)),
                  pl.BlockSpec((1, s, pair_w), lambda bi, hp, qi: (bi, 0, hp)),
                  pl.BlockSpec((1, n_kv, B_HEAD_PAIR * B_V_ROWS, kv_tile), lambda bi, hp, qi: (bi, 0, hp, 0)),
                  pl.BlockSpec((1, q_block, out_w), lambda bi, hp, qi: (bi, qi, hp))],
        out_specs=pl.BlockSpec((1, q_block, out_w), lambda bi, hp, qi: (bi, qi, hp)),
        out_shape=jax.ShapeDtypeStruct((b, s, B_WIDTH), BF16),
        scratch_shapes=[s_buf, s_buf, p_buf, p_buf],
        compiler_params=_params("parallel", "parallel", "arbitrary"),
    )(q, k, vt, z)


def kernel(x, p, norm_g, a_w_in, a_sink, a_w_out, b_w_in, b_q_norm, b_w_qb, b_kv_norm, b_w_kvb, b_w_out,
           ple_w, ple_norm_g, ple_w_gate, final_norm_g):
    depth = p.shape[0]
    s = x.shape[1]
    rot = _rope_table(s)
    h = x
    for i in range(depth):
        j = i // 2
        if i % 2 == 0:
            q, k, v, z = _pre_a(h, norm_g[i], a_w_in[j])
            og = _win_attn(q, k, v, z, a_sink[j])
            w_out = a_w_out[j]
        else:
            w_in_p, wq_p, wk_p, wvt_p, vbias = _prep_b_weights(b_w_in[j], b_w_qb[j], b_w_kvb[j])
            q, k, vt, z = _pre_b(h, norm_g[i], w_in_p, b_q_norm[j], wq_p, b_kv_norm[j], wk_p, wvt_p, vbias, rot)
            og = _mla_attn(q, k, vt, z)
            w_out = b_w_out[j]
        h = _post(og, h, p[i], w_out, ple_norm_g[i], ple_w_gate[i], ple_w[i], final_norm_g, final=(i == depth - 1))
    return h
```

```python
import functools
import math

import jax
import jax.numpy as jnp
from jax import lax
from jax.experimental import pallas as pl
from jax.experimental.pallas import tpu as pltpu

F32 = jnp.float32
BF16 = jnp.bfloat16

EPS = 1e-6
NEG_BIG = -1e30
LOG2E = 1.4426950408889634

A_HEADS = 16
A_KV_HEADS = 4
A_GROUP = A_HEADS // A_KV_HEADS
A_HEAD_DIM = 64
A_WIDTH = A_HEADS * A_HEAD_DIM
A_KV_WIDTH = A_KV_HEADS * A_HEAD_DIM
WINDOW = 128
BLOCK = 128
A_V_ROWS = 80
WIN_Q_BLOCK = 256
MASK_DIST = 1e33

B_HEADS = 16
B_NOPE = 64
B_ROPE = 32
B_VDIM = 64
B_WIDTH = B_HEADS * B_VDIM
Q_LORA = 384
KV_LORA = 256
ROPE_THETA = 10000.0
B_QK_PAD = 128
B_V_ROWS = 80
B_HEAD_PAIR = 2

ROW_TILE = 512
MLA_Q_BLOCK = 1024
MLA_Q_TILE = 256
MLA_KV_TILE = 512

VMEM_LIMIT = 56 * 1024 * 1024


def _rms(x, g):
    ms = jnp.mean(x * x, axis=-1, keepdims=True)
    return x * lax.rsqrt(ms + EPS) * g


def _dot(a, b):
    return jnp.dot(a, b, preferred_element_type=F32)


def _dot_nt(a, b):
    return lax.dot_general(a, b, (((1,), (1,)), ((), ())), preferred_element_type=F32)


def _params(*sem):
    return pltpu.CompilerParams(dimension_semantics=sem, vmem_limit_bytes=VMEM_LIMIT)


def _const_spec(shape):
    nd = len(shape)
    return pl.BlockSpec(shape, lambda *_: (0,) * nd)


def _pre_a_kernel(x_ref, g_ref, w_ref, wvt_ref, vb_ref, q_ref, k_ref, vt_ref, z_ref):
    u = _rms(x_ref[0], g_ref[...]).astype(BF16)
    qs = (A_HEAD_DIM ** -0.5) * LOG2E
    o1, o2, o3 = A_WIDTH, A_WIDTH + A_KV_WIDTH, A_WIDTH + 2 * A_KV_WIDTH
    q_ref[0] = (_dot(u, w_ref[:, :o1]) * qs).astype(BF16)
    k = _dot(u, w_ref[:, o1:o2])
    lane = lax.broadcasted_iota(jnp.int32, (k.shape[0], 2 * A_HEAD_DIM), 1)
    low = lane < A_HEAD_DIM
    for kv in range(A_KV_HEADS):
        blk = k[:, (kv // 2) * 128:(kv // 2 + 1) * 128]
        swapped = pltpu.roll(blk, A_HEAD_DIM, 1)
        first, second = (blk, swapped) if kv % 2 == 0 else (swapped, blk)
        k_ref[0, :, kv * 256:kv * 256 + 128] = jnp.where(low, first, 0.0).astype(BF16)
        k_ref[0, :, kv * 256 + 128:(kv + 1) * 256] = jnp.where(low, 0.0, second).astype(BF16)
    vt_ref[0] = (_dot_nt(wvt_ref[...], u) + vb_ref[...]).astype(BF16)
    z_ref[0] = _dot(u, w_ref[:, o3:])


def _pre_a(h, g, w_in):
    b, s, d = h.shape
    tm = min(ROW_TILE, s)
    n_in = w_in.shape[1]
    o2, o3 = A_WIDTH + A_KV_WIDTH, A_WIDTH + 2 * A_KV_WIDTH
    pad = A_V_ROWS - A_HEAD_DIM
    wv = w_in[:, o2:o3].reshape(d, A_KV_HEADS, A_HEAD_DIM)
    wvt = jnp.concatenate([wv, jnp.zeros((d, A_KV_HEADS, pad), w_in.dtype)], axis=2)
    wvt = wvt.reshape(d, A_KV_HEADS * A_V_ROWS).T.astype(BF16)
    vbias = jnp.tile(jnp.concatenate([jnp.zeros((A_HEAD_DIM,), F32), jnp.ones((pad,), F32)]),
                     A_KV_HEADS).reshape(A_KV_HEADS * A_V_ROWS, 1)
    vrows = A_KV_HEADS * A_V_ROWS
    row = lambda w: pl.BlockSpec((1, tm, w), lambda bi, i: (bi, i, 0))
    return pl.pallas_call(
        _pre_a_kernel,
        grid=(b, s // tm),
        in_specs=[row(d), _const_spec((1, d)), _const_spec((d, n_in)), _const_spec((vrows, d)),
                  _const_spec((vrows, 1))],
        out_specs=[row(A_WIDTH), row(A_KV_HEADS * 256), pl.BlockSpec((1, vrows, tm), lambda bi, i: (bi, 0, i)),
                   row(A_WIDTH)],
        out_shape=[
            jax.ShapeDtypeStruct((b, s, A_WIDTH), BF16),
            jax.ShapeDtypeStruct((b, s, A_KV_HEADS * 256), BF16),
            jax.ShapeDtypeStruct((b, vrows, s), BF16),
            jax.ShapeDtypeStruct((b, s, A_WIDTH), F32),
        ],
        compiler_params=_params("parallel", "parallel"),
    )(h, g.reshape(1, d), w_in.astype(BF16), wvt, vbias)


def _win_kernel(sink_ref, q_ref, kp_ref, kc_ref, kn_ref, vp_ref, vc_ref, vn_ref, z_ref, o_ref,
                s0_ref, s1_ref, p0_ref, p1_ref, *, seq):
    i = pl.program_id(1)
    s_bufs, p_bufs = (s0_ref, s1_ref), (p0_ref, p1_ref)
    n_keys = WIN_Q_BLOCK + 2 * BLOCK
    r = lax.broadcasted_iota(jnp.int32, (n_keys, WIN_Q_BLOCK), 0)
    c = lax.broadcasted_iota(jnp.int32, (n_keys, WIN_Q_BLOCK), 1)
    rel = r - BLOCK - c
    k_pos = i * WIN_Q_BLOCK - BLOCK + r
    valid = (jnp.abs(rel) <= WINDOW) & (k_pos >= 0) & (k_pos < seq)
    neg_dist = jnp.where(valid, -jnp.abs(rel).astype(F32), -MASK_DIST)

    def scores(h, slot):
        kv, hp = h // A_GROUP, h // 2
        cols = slice(kv * 256 + (h % 2) * 128, kv * 256 + (h % 2 + 1) * 128)
        k_sel = jnp.concatenate([kp_ref[0, :, cols], kc_ref[0, :, cols], kn_ref[0, :, cols]], axis=0)
        slope = 2.0 ** (-8.0 * (h + 1) / A_HEADS) * LOG2E
        s = _dot_nt(k_sel, q_ref[0, :, hp * 128:(hp + 1) * 128]) + slope * neg_dist
        s_bufs[slot][...] = s
        return jnp.max(s, axis=0, keepdims=True)

    def exps(h, slot, tmax):
        m = jnp.maximum(tmax, sink_ref[h] * LOG2E)
        p_bufs[slot][...] = jnp.exp2(s_bufs[slot][...] - m).astype(BF16)
        return m

    def values(h, slot, m):
        kv = h // A_GROUP
        rows = slice(kv * A_V_ROWS, (kv + 1) * A_V_ROWS)
        vt = jnp.concatenate([vp_ref[0, rows, :], vc_ref[0, rows, :], vn_ref[0, rows, :]], axis=1)
        acc = _dot(vt, p_bufs[slot][...])
        denom = acc[A_HEAD_DIM:A_HEAD_DIM + 1] + jnp.exp2(sink_ref[h] * LOG2E - m)
        return acc[:A_HEAD_DIM] / denom

    tmax, m, pending = {}, {}, {}
    for t in range(A_HEADS + 2):
        if t >= 2:
            h = t - 2
            pending[h] = values(h, h % 2, m.pop(h))
            if h % 2 == 1:
                o_t = jnp.concatenate([pending.pop(h - 1), pending.pop(h)], axis=0)
                lanes = slice((h // 2) * 128, (h // 2 + 1) * 128)
                z = z_ref[0, :, lanes]
                o_ref[0, :, lanes] = (o_t.T * (z * jax.nn.sigmoid(z))).astype(BF16)
        if 1 <= t <= A_HEADS:
            m[t - 1] = exps(t - 1, (t - 1) % 2, tmax.pop(t - 1))
        if t < A_HEADS:
            tmax[t] = scores(t, t % 2)


def _win_attn(q, k_exp, vt, z, sink):
    b, s, _ = q.shape
    qb = WIN_Q_BLOCK
    assert s % qb == 0
    per = qb // BLOCK
    nb = s // BLOCK
    vrows = vt.shape[1]
    kw = k_exp.shape[-1]
    prev_i = lambda i: jnp.maximum(per * i - 1, 0)
    next_i = lambda i: jnp.minimum(per * i + per, nb - 1)
    cur = lambda w: pl.BlockSpec((1, qb, w), lambda bi, i: (bi, i, 0))
    s_buf = pltpu.VMEM((qb + 2 * BLOCK, qb), F32)
    p_buf = pltpu.VMEM((qb + 2 * BLOCK, qb), BF16)
    return pl.pallas_call(
        functools.partial(_win_kernel, seq=s),
        grid=(b, s // qb),
        in_specs=[pl.BlockSpec(memory_space=pltpu.SMEM), cur(A_WIDTH),
                  pl.BlockSpec((1, BLOCK, kw), lambda bi, i: (bi, prev_i(i), 0)),
                  cur(kw),
                  pl.BlockSpec((1, BLOCK, kw), lambda bi, i: (bi, next_i(i), 0)),
                  pl.BlockSpec((1, vrows, BLOCK), lambda bi, i: (bi, 0, prev_i(i))),
                  pl.BlockSpec((1, vrows, qb), lambda bi, i: (bi, 0, i)),
                  pl.BlockSpec((1, vrows, BLOCK), lambda bi, i: (bi, 0, next_i(i))),
                  cur(A_WIDTH)],
        out_specs=cur(A_WIDTH),
        out_shape=jax.ShapeDtypeStruct((b, s, A_WIDTH), BF16),
        scratch_shapes=[s_buf, s_buf, p_buf, p_buf],
        compiler_params=_params("parallel", "parallel"),
    )(sink.astype(F32), q, k_exp, k_exp, k_exp, vt, vt, vt, z)


def _post_kernel(og_ref, h_ref, p_ref, wo_ref, pg_ref, wg_ref, pw_ref, fg_ref, o_ref, *, final):
    h1 = h_ref[0] + _dot(og_ref[0], wo_ref[...])
    gate = jax.nn.sigmoid(_dot(_rms(h1, pg_ref[...]).astype(BF16), wg_ref[...]))
    h2 = h1 + _dot(p_ref[0].astype(BF16), pw_ref[...]) * gate
    if final:
        h2 = _rms(h2, fg_ref[...])
    o_ref[0] = h2


def _post(og, h, p, w_out, ple_g, w_gate, ple_w, final_g, final):
    b, s, d = h.shape
    tm = min(ROW_TILE, s)
    pd = p.shape[-1]
    row = lambda w: pl.BlockSpec((1, tm, w), lambda bi, i: (bi, i, 0))
    return pl.pallas_call(
        functools.partial(_post_kernel, final=final),
        grid=(b, s // tm),
        in_specs=[row(og.shape[-1]), row(d), row(pd), _const_spec(w_out.shape), _const_spec((1, d)),
                  _const_spec((d, d)), _const_spec((pd, d)), _const_spec((1, d))],
        out_specs=row(d),
        out_shape=jax.ShapeDtypeStruct((b, s, d), F32),
        compiler_params=_params("parallel", "parallel"),
    )(og, h, p, w_out.astype(BF16), ple_g.reshape(1, d), w_gate.astype(BF16), ple_w.astype(BF16),
      final_g.reshape(1, d))


def _pre_b_kernel(x_ref, g_ref, w_ref, qn_ref, wq_ref, kn_ref, wk_ref, wvt_ref, vb_ref, r_ref,
                  q_ref, k_ref, vt_ref, z_ref):
    u = _rms(x_ref[0], g_ref[...]).astype(BF16)
    o1, o2, o3 = Q_LORA, Q_LORA + KV_LORA, Q_LORA + KV_LORA + B_QK_PAD
    rot = r_ref[...]
    cq = _rms(_dot(u, w_ref[:, :o1]), qn_ref[...]).astype(BF16)
    qs = ((B_NOPE + B_ROPE) ** -0.5) * LOG2E
    q = _dot(cq, wq_ref[...]) * jnp.tile(rot * qs, (1, B_HEADS))
    q_ref[0] = q.astype(BF16)
    ckv = _rms(_dot(u, w_ref[:, o1:o2]), kn_ref[...]).astype(BF16)
    y = _dot(u, w_ref[:, o2:o3]) * rot
    lane = lax.broadcasted_iota(jnp.int32, y.shape, 1)
    both = pltpu.roll(y, 32, 1) + pltpu.roll(y, 96, 1)
    kr = y + jnp.where(lane >= B_NOPE, both, 0.0)
    k_ref[0] = (_dot(ckv, wk_ref[...]) + jnp.tile(kr, (1, B_HEADS))).astype(BF16)
    vt_ref[0, 0] = (_dot_nt(wvt_ref[...], ckv) + vb_ref[...]).astype(BF16)
    z_ref[0] = _dot(u, w_ref[:, o3:])


def _pre_b(h, g, w_in_p, q_norm, wq_p, kv_norm, wk_p, wvt_p, vbias, rot):
    b, s, d = h.shape
    tm = min(MLA_KV_TILE, s)
    row = lambda w: pl.BlockSpec((1, tm, w), lambda bi, i: (bi, i, 0))
    vrows = B_HEADS * B_V_ROWS
    return pl.pallas_call(
        _pre_b_kernel,
        grid=(b, s // tm),
        in_specs=[row(d), _const_spec((1, d)), _const_spec(w_in_p.shape), _const_spec((1, Q_LORA)),
                  _const_spec(wq_p.shape), _const_spec((1, KV_LORA)), _const_spec(wk_p.shape),
                  _const_spec(wvt_p.shape), _const_spec((vrows, 1)),
                  pl.BlockSpec((tm, B_QK_PAD), lambda bi, i: (i, 0))],
        out_specs=[row(B_HEADS * B_QK_PAD), row(B_HEADS * B_QK_PAD),
                   pl.BlockSpec((1, 1, vrows, tm), lambda bi, i: (bi, i, 0, 0)), row(B_WIDTH)],
        out_shape=[
            jax.ShapeDtypeStruct((b, s, B_HEADS * B_QK_PAD), BF16),
            jax.ShapeDtypeStruct((b, s, B_HEADS * B_QK_PAD), BF16),
            jax.ShapeDtypeStruct((b, s // tm, vrows, tm), BF16),
            jax.ShapeDtypeStruct((b, s, B_WIDTH), F32),
        ],
        compiler_params=_params("parallel", "parallel"),
    )(h, g.reshape(1, d), w_in_p, q_norm.reshape(1, Q_LORA), wq_p, kv_norm.reshape(1, KV_LORA), wk_p,
      wvt_p, vbias, rot)


def _prep_b_weights(w_in, w_qb, w_kvb):
    half = B_ROPE // 2
    d = w_in.shape[0]
    o1, o2, o3 = Q_LORA, Q_LORA + KV_LORA, Q_LORA + KV_LORA + B_ROPE
    kr = w_in[:, o2:o3]
    kr_swapped = jnp.concatenate([kr[:, half:], kr[:, :half]], axis=1)
    kr_block = jnp.concatenate([jnp.zeros((d, B_NOPE), w_in.dtype), kr, kr_swapped], axis=1)
    w_in_p = jnp.concatenate([w_in[:, :o2], kr_block, w_in[:, o3:]], axis=1).astype(BF16)

    wq = w_qb.reshape(Q_LORA, B_HEADS, B_NOPE + B_ROPE)
    q_rope = wq[:, :, B_NOPE:]
    q_rope_swapped = jnp.concatenate([q_rope[:, :, half:], q_rope[:, :, :half]], axis=2)
    wq_p = jnp.concatenate([wq[:, :, :B_NOPE], q_rope, q_rope_swapped], axis=2)
    wq_p = wq_p.reshape(Q_LORA, B_HEADS * B_QK_PAD).astype(BF16)

    wkv = w_kvb.reshape(KV_LORA, B_HEADS, B_NOPE + B_VDIM)
    wk_p = jnp.concatenate([wkv[:, :, :B_NOPE], jnp.zeros((KV_LORA, B_HEADS, B_QK_PAD - B_NOPE), w_kvb.dtype)],
                           axis=2).reshape(KV_LORA, B_HEADS * B_QK_PAD).astype(BF16)
    wv = jnp.concatenate([wkv[:, :, B_NOPE:], jnp.zeros((KV_LORA, B_HEADS, B_V_ROWS - B_VDIM), w_kvb.dtype)],
                         axis=2).reshape(KV_LORA, B_HEADS * B_V_ROWS)
    wvt_p = wv.T.astype(BF16)
    vbias = jnp.tile(jnp.concatenate([jnp.zeros((B_VDIM,), F32), jnp.ones((B_V_ROWS - B_VDIM,), F32)]),
                     B_HEADS).reshape(B_HEADS * B_V_ROWS, 1)
    return w_in_p, wq_p, wk_p, wvt_p, vbias


def _rope_table(s):
    half = B_ROPE // 2
    inv_freq = ROPE_THETA ** (-jnp.arange(half, dtype=F32) / half)
    ang = jnp.arange(s, dtype=F32)[:, None] * inv_freq[None, :]
    cos, sin = jnp.cos(ang), jnp.sin(ang)
    return jnp.concatenate([jnp.ones((s, B_NOPE), F32), cos, cos, -sin, sin], axis=1)


def _mla_kernel(q_ref, k_ref, vt_ref, z_ref, o_ref, s0_ref, s1_ref, p0_ref, p1_ref, *, q_tile, kv_tile):
    n_kv = vt_ref.shape[1]
    n_q = q_ref.shape[1] // q_tile
    s_bufs, p_bufs = (s0_ref, s1_ref), (p0_ref, p1_ref)
    heads = range(B_HEAD_PAIR)
    m_init = (jnp.full((1, q_tile), NEG_BIG, F32),) * B_HEAD_PAIR

    def q_rows(qt):
        return pl.ds(pl.multiple_of(qt * q_tile, q_tile), q_tile)

    def load_q(qt):
        return [q_ref[0, q_rows(qt), hh * B_QK_PAD:(hh + 1) * B_QK_PAD] for hh in heads]

    def scores(qh, j, slot):
        tmax = []
        for hh in heads:
            kc = k_ref[0, j * kv_tile:(j + 1) * kv_tile, hh * B_QK_PAD:(hh + 1) * B_QK_PAD]
            s = _dot_nt(kc, qh[hh])
            s_bufs[slot][hh] = s
            tmax.append(jnp.max(s, axis=0, keepdims=True))
        return tuple(tmax)

    def exps(slot, m, tmax):
        m_out, alpha = [], []
        for hh in heads:
            m_new = jnp.maximum(m[hh], tmax[hh])
            p_bufs[slot][hh] = jnp.exp2(s_bufs[slot][hh] - m_new).astype(BF16)
            alpha.append(jnp.exp2(m[hh] - m_new))
            m_out.append(m_new)
        return tuple(m_out), tuple(alpha)

    def values(j, slot, acc, alpha):
        out = []
        for hh in heads:
            vt = vt_ref[0, j, hh * B_V_ROWS:(hh + 1) * B_V_ROWS, :]
            out.append(acc[hh] * alpha[hh] + _dot(vt, p_bufs[slot][hh]))
        return tuple(out)

    def tile_body(qt, carry):
        m, alpha, tmax = carry
        qh = load_q(qt)
        qh_next = load_q(jnp.minimum(qt + 1, n_q - 1))
        acc = (jnp.zeros((B_V_ROWS, q_tile), F32),) * B_HEAD_PAIR
        for t in range(2, n_kv):
            acc = values(t - 2, t % 2, acc, alpha)
            m, alpha = exps((t - 1) % 2, m, tmax)
            tmax = scores(qh, t, t % 2)
        acc = values(n_kv - 2, 0, acc, alpha)
        m, alpha = exps(1, m, tmax)
        tmax_next = scores(qh_next, 0, 0)
        acc = values(n_kv - 1, 1, acc, alpha)
        m_next, alpha_next = exps(0, m_init, tmax_next)
        tmax_next = scores(qh_next, 1, 1)
        o_t = jnp.concatenate([acc[hh][:B_VDIM] / acc[hh][B_VDIM:B_VDIM + 1] for hh in heads], axis=0)
        z = z_ref[0, q_rows(qt), :]
        o_ref[0, q_rows(qt), :] = (o_t.T * (z * jax.nn.sigmoid(z))).astype(BF16)
        return m_next, alpha_next, tmax_next

    qh0 = load_q(0)
    tmax = scores(qh0, 0, 0)
    m, alpha = exps(0, m_init, tmax)
    tmax = scores(qh0, 1, 1)
    lax.fori_loop(0, n_q, tile_body, (m, alpha, tmax))


def _mla_attn(q, k, vt, z):
    b, s, _ = q.shape
    n_kv, kv_tile = vt.shape[1], vt.shape[3]
    assert n_kv >= 2 and n_kv % 2 == 0
    q_block = min(MLA_Q_BLOCK, s)
    q_tile = min(MLA_Q_TILE, q_block)
    pair_w = B_HEAD_PAIR * B_QK_PAD
    out_w = B_HEAD_PAIR * B_VDIM
    s_buf = pltpu.VMEM((B_HEAD_PAIR, kv_tile, q_tile), F32)
    p_buf = pltpu.VMEM((B_HEAD_PAIR, kv_tile, q_tile), BF16)
    return pl.pallas_call(
        functools.partial(_mla_kernel, q_tile=q_tile, kv_tile=kv_tile),
        grid=(b, B_HEADS // B_HEAD_PAIR, s // q_block),
        in_specs=[pl.BlockSpec((1, q_block, pair_w), lambda bi, hp, qi: (bi, qi, hp)),
                  pl.BlockSpec((1, s, pair_w), lambda bi, hp, qi: (bi, 0, hp)),
                  pl.BlockSpec((1, n_kv, B_HEAD_PAIR * B_V_ROWS, kv_tile), lambda bi, hp, qi: (bi, 0, hp, 0)),
                  pl.BlockSpec((1, q_block, out_w), lambda bi, hp, qi: (bi, qi, hp))],
        out_specs=pl.BlockSpec((1, q_block, out_w), lambda bi, hp, qi: (bi, qi, hp)),
        out_shape=jax.ShapeDtypeStruct((b, s, B_WIDTH), BF16),
        scratch_shapes=[s_buf, s_buf, p_buf, p_buf],
        compiler_params=_params("parallel", "parallel", "arbitrary"),
    )(q, k, vt, z)


def kernel(x, p, norm_g, a_w_in, a_sink, a_w_out, b_w_in, b_q_norm, b_w_qb, b_kv_norm, b_w_kvb, b_w_out,
           ple_w, ple_norm_g, ple_w_gate, final_norm_g):
    depth = p.shape[0]
    s = x.shape[1]
    rot = _rope_table(s)
    h = x
    for i in range(depth):
        j = i // 2
        if i % 2 == 0:
            q, k_exp, vt, z = _pre_a(h, norm_g[i], a_w_in[j])
            og = _win_attn(q, k_exp, vt, z, a_sink[j])
            w_out = a_w_out[j]
        else:
            w_in_p, wq_p, wk_p, wvt_p, vbias = _prep_b_weights(b_w_in[j], b_w_qb[j], b_w_kvb[j])
            q, k, vt, z = _pre_b(h, norm_g[i], w_in_p, b_q_norm[j], wq_p, b_kv_norm[j], wk_p, wvt_p, vbias, rot)
            og = _mla_attn(q, k, vt, z)
            w_out = b_w_out[j]
        h = _post(og, h, p[i], w_out, ple_norm_g[i], ple_w_gate[i], ple_w[i], final_norm_g, final=(i == depth - 1))
    return h
```

```python
import functools

import jax
import jax.numpy as jnp
from jax import lax
from jax.experimental import pallas as pl
from jax.experimental.pallas import tpu as pltpu

F32 = jnp.float32
BF16 = jnp.bfloat16

EPS = 1e-6
NEG_BIG = -1e30
LOG2E = 1.4426950408889634

A_HEADS = 16
A_KV_HEADS = 4
A_GROUP = A_HEADS // A_KV_HEADS
A_HEAD_DIM = 64
A_WIDTH = A_HEADS * A_HEAD_DIM
A_KV_WIDTH = A_KV_HEADS * A_HEAD_DIM
WINDOW = 128
BLOCK = 128
A_V_ROWS = 80
WIN_Q_BLOCK = 256
MASK_DIST = 1e33

B_HEADS = 16
B_NOPE = 64
B_ROPE = 32
B_VDIM = 64
B_WIDTH = B_HEADS * B_VDIM
Q_LORA = 384
KV_LORA = 256
ROPE_THETA = 10000.0
B_QK_PAD = 128
B_V_ROWS = 80
B_HEAD_PAIR = 2

ROW_TILE = 512
MLA_Q_BLOCK = 2048
MLA_Q_TILE = 256
MLA_KV_TILE = 512

VMEM_LIMIT = 56 * 1024 * 1024


def _rms(x, g):
    ms = jnp.mean(x * x, axis=-1, keepdims=True)
    return x * lax.rsqrt(ms + EPS) * g


def _dot(a, b):
    return jnp.dot(a, b, preferred_element_type=F32)


def _dot_nt(a, b):
    return lax.dot_general(a, b, (((1,), (1,)), ((), ())), preferred_element_type=F32)


def _params(*sem):
    return pltpu.CompilerParams(dimension_semantics=sem, vmem_limit_bytes=VMEM_LIMIT)


def _const_spec(shape):
    nd = len(shape)
    return pl.BlockSpec(shape, lambda *_: (0,) * nd)


def _pre_a_kernel(x_ref, g_ref, w_ref, wvt_ref, vb_ref, q_ref, k_ref, vt_ref, z_ref):
    u = _rms(x_ref[0], g_ref[...]).astype(BF16)
    qs = (A_HEAD_DIM ** -0.5) * LOG2E
    o1, o2, o3 = A_WIDTH, A_WIDTH + A_KV_WIDTH, A_WIDTH + 2 * A_KV_WIDTH
    q_ref[0] = (_dot(u, w_ref[:, :o1]) * qs).astype(BF16)
    k = _dot(u, w_ref[:, o1:o2])
    lane = lax.broadcasted_iota(jnp.int32, (k.shape[0], 2 * A_HEAD_DIM), 1)
    low = lane < A_HEAD_DIM
    for kv in range(A_KV_HEADS):
        blk = k[:, (kv // 2) * 128:(kv // 2 + 1) * 128]
        swapped = pltpu.roll(blk, A_HEAD_DIM, 1)
        first, second = (blk, swapped) if kv % 2 == 0 else (swapped, blk)
        k_ref[0, :, kv * 256:kv * 256 + 128] = jnp.where(low, first, 0.0).astype(BF16)
        k_ref[0, :, kv * 256 + 128:(kv + 1) * 256] = jnp.where(low, 0.0, second).astype(BF16)
    vt_ref[0] = (_dot_nt(wvt_ref[...], u) + vb_ref[...]).astype(BF16)
    z_ref[0] = _dot(u, w_ref[:, o3:]).astype(BF16)


def _pre_a(h, g, w_in):
    b, s, d = h.shape
    tm = min(ROW_TILE, s)
    n_in = w_in.shape[1]
    o2, o3 = A_WIDTH + A_KV_WIDTH, A_WIDTH + 2 * A_KV_WIDTH
    pad = A_V_ROWS - A_HEAD_DIM
    wv = w_in[:, o2:o3].reshape(d, A_KV_HEADS, A_HEAD_DIM)
    wvt = jnp.concatenate([wv, jnp.zeros((d, A_KV_HEADS, pad), w_in.dtype)], axis=2)
    wvt = wvt.reshape(d, A_KV_HEADS * A_V_ROWS).T.astype(BF16)
    vbias = jnp.tile(jnp.concatenate([jnp.zeros((A_HEAD_DIM,), F32), jnp.ones((pad,), F32)]),
                     A_KV_HEADS).reshape(A_KV_HEADS * A_V_ROWS, 1)
    vrows = A_KV_HEADS * A_V_ROWS
    row = lambda w: pl.BlockSpec((1, tm, w), lambda bi, i: (bi, i, 0))
    return pl.pallas_call(
        _pre_a_kernel,
        grid=(b, s // tm),
        in_specs=[row(d), _const_spec((1, d)), _const_spec((d, n_in)), _const_spec((vrows, d)),
                  _const_spec((vrows, 1))],
        out_specs=[row(A_WIDTH), row(A_KV_HEADS * 256), pl.BlockSpec((1, vrows, tm), lambda bi, i: (bi, 0, i)),
                   row(A_WIDTH)],
        out_shape=[
            jax.ShapeDtypeStruct((b, s, A_WIDTH), BF16),
            jax.ShapeDtypeStruct((b, s, A_KV_HEADS * 256), BF16),
            jax.ShapeDtypeStruct((b, vrows, s), BF16),
            jax.ShapeDtypeStruct((b, s, A_WIDTH), BF16),
        ],
        compiler_params=_params("parallel", "parallel"),
    )(h, g.reshape(1, d), w_in.astype(BF16), wvt, vbias)


def _win_kernel(sink_ref, q_ref, kp_ref, kc_ref, kn_ref, vp_ref, vc_ref, vn_ref, z_ref, o_ref,
                s0_ref, s1_ref, p0_ref, p1_ref, *, seq):
    i = pl.program_id(1)
    s_bufs, p_bufs = (s0_ref, s1_ref), (p0_ref, p1_ref)
    n_keys = WIN_Q_BLOCK + 2 * BLOCK
    r = lax.broadcasted_iota(jnp.int32, (n_keys, WIN_Q_BLOCK), 0)
    c = lax.broadcasted_iota(jnp.int32, (n_keys, WIN_Q_BLOCK), 1)
    rel = r - BLOCK - c
    k_pos = i * WIN_Q_BLOCK - BLOCK + r
    valid = (jnp.abs(rel) <= WINDOW) & (k_pos >= 0) & (k_pos < seq)
    neg_dist = jnp.where(valid, -jnp.abs(rel).astype(F32), -MASK_DIST)

    def scores(h, slot):
        kv, hp = h // A_GROUP, h // 2
        cols = slice(kv * 256 + (h % 2) * 128, kv * 256 + (h % 2 + 1) * 128)
        k_sel = jnp.concatenate([kp_ref[0, :, cols], kc_ref[0, :, cols], kn_ref[0, :, cols]], axis=0)
        slope = 2.0 ** (-8.0 * (h + 1) / A_HEADS) * LOG2E
        s = _dot_nt(k_sel, q_ref[0, :, hp * 128:(hp + 1) * 128]) + slope * neg_dist
        s_bufs[slot][...] = s
        return jnp.max(s, axis=0, keepdims=True)

    def exps(h, slot, tmax):
        m = jnp.maximum(tmax, sink_ref[h] * LOG2E)
        p_bufs[slot][...] = jnp.exp2(s_bufs[slot][...] - m).astype(BF16)
        return m

    def values(h, slot, m):
        kv = h // A_GROUP
        rows = slice(kv * A_V_ROWS, (kv + 1) * A_V_ROWS)
        vt = jnp.concatenate([vp_ref[0, rows, :], vc_ref[0, rows, :], vn_ref[0, rows, :]], axis=1)
        acc = _dot(vt, p_bufs[slot][...])
        denom = acc[A_HEAD_DIM:A_HEAD_DIM + 1] + jnp.exp2(sink_ref[h] * LOG2E - m)
        return acc[:A_HEAD_DIM] / denom

    tmax, m, pending = {}, {}, {}
    for t in range(A_HEADS + 2):
        if t >= 2:
            h = t - 2
            pending[h] = values(h, h % 2, m.pop(h))
            if h % 2 == 1:
                o_t = jnp.concatenate([pending.pop(h - 1), pending.pop(h)], axis=0)
                lanes = slice((h // 2) * 128, (h // 2 + 1) * 128)
                z = z_ref[0, :, lanes].astype(F32)
                o_ref[0, :, lanes] = (o_t.T * (z * jax.nn.sigmoid(z))).astype(BF16)
        if 1 <= t <= A_HEADS:
            m[t - 1] = exps(t - 1, (t - 1) % 2, tmax.pop(t - 1))
        if t < A_HEADS:
            tmax[t] = scores(t, t % 2)


def _win_attn(q, k_exp, vt, z, sink):
    b, s, _ = q.shape
    qb = WIN_Q_BLOCK
    assert s % qb == 0
    per = qb // BLOCK
    nb = s // BLOCK
    vrows = vt.shape[1]
    kw = k_exp.shape[-1]
    prev_i = lambda i: jnp.maximum(per * i - 1, 0)
    next_i = lambda i: jnp.minimum(per * i + per, nb - 1)
    cur = lambda w: pl.BlockSpec((1, qb, w), lambda bi, i: (bi, i, 0))
    s_buf = pltpu.VMEM((qb + 2 * BLOCK, qb), F32)
    p_buf = pltpu.VMEM((qb + 2 * BLOCK, qb), BF16)
    return pl.pallas_call(
        functools.partial(_win_kernel, seq=s),
        grid=(b, s // qb),
        in_specs=[pl.BlockSpec(memory_space=pltpu.SMEM), cur(A_WIDTH),
                  pl.BlockSpec((1, BLOCK, kw), lambda bi, i: (bi, prev_i(i), 0)),
                  cur(kw),
                  pl.BlockSpec((1, BLOCK, kw), lambda bi, i: (bi, next_i(i), 0)),
                  pl.BlockSpec((1, vrows, BLOCK), lambda bi, i: (bi, 0, prev_i(i))),
                  pl.BlockSpec((1, vrows, qb), lambda bi, i: (bi, 0, i)),
                  pl.BlockSpec((1, vrows, BLOCK), lambda bi, i: (bi, 0, next_i(i))),
                  cur(A_WIDTH)],
        out_specs=cur(A_WIDTH),
        out_shape=jax.ShapeDtypeStruct((b, s, A_WIDTH), BF16),
        scratch_shapes=[s_buf, s_buf, p_buf, p_buf],
        compiler_params=_params("parallel", "parallel"),
    )(sink.astype(F32), q, k_exp, k_exp, k_exp, vt, vt, vt, z)


def _post_kernel(og_ref, h_ref, p_ref, wo_ref, pg_ref, wg_ref, pw_ref, fg_ref, o_ref, *, final):
    h1 = h_ref[0] + _dot(og_ref[0], wo_ref[...])
    gate = jax.nn.sigmoid(_dot(_rms(h1, pg_ref[...]).astype(BF16), wg_ref[...]))
    h2 = h1 + _dot(p_ref[0].astype(BF16), pw_ref[...]) * gate
    if final:
        h2 = _rms(h2, fg_ref[...])
    o_ref[0] = h2


def _post(og, h, p, w_out, ple_g, w_gate, ple_w, final_g, final):
    b, s, d = h.shape
    tm = min(ROW_TILE, s)
    pd = p.shape[-1]
    row = lambda w: pl.BlockSpec((1, tm, w), lambda bi, i: (bi, i, 0))
    return pl.pallas_call(
        functools.partial(_post_kernel, final=final),
        grid=(b, s // tm),
        in_specs=[row(og.shape[-1]), row(d), row(pd), _const_spec(w_out.shape), _const_spec((1, d)),
                  _const_spec((d, d)), _const_spec((pd, d)), _const_spec((1, d))],
        out_specs=row(d),
        out_shape=jax.ShapeDtypeStruct((b, s, d), F32),
        compiler_params=_params("parallel", "parallel"),
    )(og, h, p, w_out.astype(BF16), ple_g.reshape(1, d), w_gate.astype(BF16), ple_w.astype(BF16),
      final_g.reshape(1, d))


def _pre_b_kernel(x_ref, g_ref, w_ref, qn_ref, wq_ref, kn_ref, wk_ref, wvt_ref, vb_ref, r_ref,
                  q_ref, k_ref, vt_ref, z_ref):
    u = _rms(x_ref[0], g_ref[...]).astype(BF16)
    o1, o2, o3 = Q_LORA, Q_LORA + KV_LORA, Q_LORA + KV_LORA + B_QK_PAD
    rot = r_ref[...]
    cq = _rms(_dot(u, w_ref[:, :o1]), qn_ref[...]).astype(BF16)
    qs = ((B_NOPE + B_ROPE) ** -0.5) * LOG2E
    q = _dot(cq, wq_ref[...]) * jnp.tile(rot * qs, (1, B_HEADS))
    q_ref[0] = q.astype(BF16)
    ckv = _rms(_dot(u, w_ref[:, o1:o2]), kn_ref[...]).astype(BF16)
    y = _dot(u, w_ref[:, o2:o3]) * rot
    lane = lax.broadcasted_iota(jnp.int32, y.shape, 1)
    both = pltpu.roll(y, 32, 1) + pltpu.roll(y, 96, 1)
    kr = y + jnp.where(lane >= B_NOPE, both, 0.0)
    k_ref[0] = (_dot(ckv, wk_ref[...]) + jnp.tile(kr, (1, B_HEADS))).astype(BF16)
    vt_ref[0, 0] = (_dot_nt(wvt_ref[...], ckv) + vb_ref[...]).astype(BF16)
    z_ref[0] = _dot(u, w_ref[:, o3:]).astype(BF16)


def _pre_b(h, g, w_in_p, q_norm, wq_p, kv_norm, wk_p, wvt_p, vbias, rot):
    b, s, d = h.shape
    tm = min(MLA_KV_TILE, s)
    row = lambda w: pl.BlockSpec((1, tm, w), lambda bi, i: (bi, i, 0))
    vrows = B_HEADS * B_V_ROWS
    return pl.pallas_call(
        _pre_b_kernel,
        grid=(b, s // tm),
        in_specs=[row(d), _const_spec((1, d)), _const_spec(w_in_p.shape), _const_spec((1, Q_LORA)),
                  _const_spec(wq_p.shape), _const_spec((1, KV_LORA)), _const_spec(wk_p.shape),
                  _const_spec(wvt_p.shape), _const_spec((vrows, 1)),
                  pl.BlockSpec((tm, B_QK_PAD), lambda bi, i: (i, 0))],
        out_specs=[row(B_HEADS * B_QK_PAD), row(B_HEADS * B_QK_PAD),
                   pl.BlockSpec((1, 1, vrows, tm), lambda bi, i: (bi, i, 0, 0)), row(B_WIDTH)],
        out_shape=[
            jax.ShapeDtypeStruct((b, s, B_HEADS * B_QK_PAD), BF16),
            jax.ShapeDtypeStruct((b, s, B_HEADS * B_QK_PAD), BF16),
            jax.ShapeDtypeStruct((b, s // tm, vrows, tm), BF16),
            jax.ShapeDtypeStruct((b, s, B_WIDTH), BF16),
        ],
        compiler_params=_params("parallel", "parallel"),
    )(h, g.reshape(1, d), w_in_p, q_norm.reshape(1, Q_LORA), wq_p, kv_norm.reshape(1, KV_LORA), wk_p,
      wvt_p, vbias, rot)


def _prep_b_weights(w_in, w_qb, w_kvb):
    half = B_ROPE // 2
    d = w_in.shape[0]
    o1, o2, o3 = Q_LORA, Q_LORA + KV_LORA, Q_LORA + KV_LORA + B_ROPE
    kr = w_in[:, o2:o3]
    kr_swapped = jnp.concatenate([kr[:, half:], kr[:, :half]], axis=1)
    kr_block = jnp.concatenate([jnp.zeros((d, B_NOPE), w_in.dtype), kr, kr_swapped], axis=1)
    w_in_p = jnp.concatenate([w_in[:, :o2], kr_block, w_in[:, o3:]], axis=1).astype(BF16)

    wq = w_qb.reshape(Q_LORA, B_HEADS, B_NOPE + B_ROPE)
    q_rope = wq[:, :, B_NOPE:]
    q_rope_swapped = jnp.concatenate([q_rope[:, :, half:], q_rope[:, :, :half]], axis=2)
    wq_p = jnp.concatenate([wq[:, :, :B_NOPE], q_rope, q_rope_swapped], axis=2)
    wq_p = wq_p.reshape(Q_LORA, B_HEADS * B_QK_PAD).astype(BF16)

    wkv = w_kvb.reshape(KV_LORA, B_HEADS, B_NOPE + B_VDIM)
    wk_p = jnp.concatenate([wkv[:, :, :B_NOPE], jnp.zeros((KV_LORA, B_HEADS, B_QK_PAD - B_NOPE), w_kvb.dtype)],
                           axis=2).reshape(KV_LORA, B_HEADS * B_QK_PAD).astype(BF16)
    wv = jnp.concatenate([wkv[:, :, B_NOPE:], jnp.zeros((KV_LORA, B_HEADS, B_V_ROWS - B_VDIM), w_kvb.dtype)],
                         axis=2).reshape(KV_LORA, B_HEADS * B_V_ROWS)
    wvt_p = wv.T.astype(BF16)
    vbias = jnp.tile(jnp.concatenate([jnp.zeros((B_VDIM,), F32), jnp.ones((B_V_ROWS - B_VDIM,), F32)]),
                     B_HEADS).reshape(B_HEADS * B_V_ROWS, 1)
    return w_in_p, wq_p, wk_p, wvt_p, vbias


def _rope_table(s):
    half = B_ROPE // 2
    inv_freq = ROPE_THETA ** (-jnp.arange(half, dtype=F32) / half)
    ang = jnp.arange(s, dtype=F32)[:, None] * inv_freq[None, :]
    cos, sin = jnp.cos(ang), jnp.sin(ang)
    return jnp.concatenate([jnp.ones((s, B_NOPE), F32), cos, cos, -sin, sin], axis=1)


def _mla_kernel(q_ref, k_ref, vt_ref, z_ref, o_ref, s0_ref, s1_ref, p0_ref, p1_ref, *, q_tile, kv_tile):
    n_kv = vt_ref.shape[1]
    n_q = q_ref.shape[1] // q_tile
    s_bufs, p_bufs = (s0_ref, s1_ref), (p0_ref, p1_ref)
    heads = range(B_HEAD_PAIR)
    m_init = (jnp.full((1, q_tile), NEG_BIG, F32),) * B_HEAD_PAIR

    def q_rows(qt):
        return pl.ds(pl.multiple_of(qt * q_tile, q_tile), q_tile)

    def load_q(qt):
        return [q_ref[0, q_rows(qt), hh * B_QK_PAD:(hh + 1) * B_QK_PAD] for hh in heads]

    def scores(qh, j, slot):
        for hh in heads:
            kc = k_ref[0, j * kv_tile:(j + 1) * kv_tile, hh * B_QK_PAD:(hh + 1) * B_QK_PAD]
            s_bufs[slot][hh] = _dot_nt(kc, qh[hh])

    def exps(slot, m):
        m_out, alpha = [], []
        for hh in heads:
            m_new = jnp.maximum(m[hh], jnp.max(s_bufs[slot][hh], axis=0, keepdims=True))
            p_bufs[slot][hh] = jnp.exp2(s_bufs[slot][hh] - m_new).astype(BF16)
            alpha.append(jnp.exp2(m[hh] - m_new))
            m_out.append(m_new)
        return tuple(m_out), tuple(alpha)

    def values(j, slot, acc, alpha):
        out = []
        for hh in heads:
            vt = vt_ref[0, j, hh * B_V_ROWS:(hh + 1) * B_V_ROWS, :]
            out.append(acc[hh] * alpha[hh] + _dot(vt, p_bufs[slot][hh]))
        return tuple(out)

    def tile_body(qt, carry):
        m, alpha = carry
        qh = load_q(qt)
        qh_next = load_q(jnp.minimum(qt + 1, n_q - 1))
        acc = (jnp.zeros((B_V_ROWS, q_tile), F32),) * B_HEAD_PAIR
        for t in range(2, n_kv):
            acc = values(t - 2, t % 2, acc, alpha)
            m, alpha = exps((t - 1) % 2, m)
            scores(qh, t, t % 2)
        acc = values(n_kv - 2, 0, acc, alpha)
        m, alpha = exps(1, m)
        scores(qh_next, 0, 0)
        acc = values(n_kv - 1, 1, acc, alpha)
        m_next, alpha_next = exps(0, m_init)
        scores(qh_next, 1, 1)
        o_t = jnp.concatenate([acc[hh][:B_VDIM] / acc[hh][B_VDIM:B_VDIM + 1] for hh in heads], axis=0)
        z = z_ref[0, q_rows(qt), :].astype(F32)
        o_ref[0, q_rows(qt), :] = (o_t.T * (z * jax.nn.sigmoid(z))).astype(BF16)
        return m_next, alpha_next

    qh0 = load_q(0)
    scores(qh0, 0, 0)
    m, alpha = exps(0, m_init)
    scores(qh0, 1, 1)
    lax.fori_loop(0, n_q, tile_body, (m, alpha))


def _mla_attn(q, k, vt, z):
    b, s, _ = q.shape
    n_kv, kv_tile = vt.shape[1], vt.shape[3]
    assert n_kv >= 2 and n_kv % 2 == 0
    q_block = min(MLA_Q_BLOCK, s)
    q_tile = min(MLA_Q_TILE, q_block)
    pair_w = B_HEAD_PAIR * B_QK_PAD
    out_w = B_HEAD_PAIR * B_VDIM
    s_buf = pltpu.VMEM((B_HEAD_PAIR, kv_tile, q_tile), F32)
    p_buf = pltpu.VMEM((B_HEAD_PAIR, kv_tile, q_tile), BF16)
    return pl.pallas_call(
        functools.partial(_mla_kernel, q_tile=q_tile, kv_tile=kv_tile),
        grid=(b, B_HEADS // B_HEAD_PAIR, s // q_block),
        in_specs=[pl.BlockSpec((1, q_block, pair_w), lambda bi, hp, qi: (bi, qi, hp)),
                  pl.BlockSpec((1, s, pair_w), lambda bi, hp, qi: (bi, 0, hp)),
                  pl.BlockSpec((1, n_kv, B_HEAD_PAIR * B_V_ROWS, kv_tile), lambda bi, hp, qi: (bi, 0, hp, 0)),
                  pl.BlockSpec((1, q_block, out_w), lambda bi, hp, qi: (bi, qi, hp))],
        out_specs=pl.BlockSpec((1, q_block, out_w), lambda bi, hp, qi: (bi, qi, hp)),
        out_shape=jax.ShapeDtypeStruct((b, s, B_WIDTH), BF16),
        scratch_shapes=[s_buf, s_buf, p_buf, p_buf],
        compiler_params=_params("parallel", "parallel", "arbitrary"),
    )(q, k, vt, z)


def kernel(x, p, norm_g, a_w_in, a_sink, a_w_out, b_w_in, b_q_norm, b_w_qb, b_kv_norm, b_w_kvb, b_w_out,
           ple_w, ple_norm_g, ple_w_gate, final_norm_g):
    depth = p.shape[0]
    s = x.shape[1]
    rot = _rope_table(s)
    h = x
    for i in range(depth):
        j = i // 2
        if i % 2 == 0:
            q, k_exp, vt, z = _pre_a(h, norm_g[i], a_w_in[j])
            og = _win_attn(q, k_exp, vt, z, a_sink[j])
            w_out = a_w_out[j]
        else:
            w_in_p, wq_p, wk_p, wvt_p, vbias = _prep_b_weights(b_w_in[j], b_w_qb[j], b_w_kvb[j])
            q, k, vt, z = _pre_b(h, norm_g[i], w_in_p, b_q_norm[j], wq_p, b_kv_norm[j], wk_p, wvt_p, vbias, rot)
            og = _mla_attn(q, k, vt, z)
            w_out = b_w_out[j]
        h = _post(og, h, p[i], w_out, ple_norm_g[i], ple_w_gate[i], ple_w[i], final_norm_g, final=(i == depth - 1))
    return h
```

```python
import functools

import jax
import jax.numpy as jnp
from jax import lax
from jax.experimental import pallas as pl
from jax.experimental.pallas import tpu as pltpu

F32 = jnp.float32
BF16 = jnp.bfloat16

EPS = 1e-6
NEG_BIG = -1e30
LOG2E = 1.4426950408889634

A_HEADS = 16
A_KV_HEADS = 4
A_GROUP = A_HEADS // A_KV_HEADS
A_HEAD_DIM = 64
A_WIDTH = A_HEADS * A_HEAD_DIM
A_KV_WIDTH = A_KV_HEADS * A_HEAD_DIM
WINDOW = 128
BLOCK = 128
A_V_ROWS = 80
WIN_Q_BLOCK = 256
MASK_DIST = 1e33

B_HEADS = 16
B_NOPE = 64
B_ROPE = 32
B_VDIM = 64
B_WIDTH = B_HEADS * B_VDIM
Q_LORA = 384
KV_LORA = 256
ROPE_THETA = 10000.0
B_QK_PAD = 128
B_V_ROWS = 80
B_HEAD_PAIR = 2

ROW_TILE = 512
MLA_Q_BLOCK = 2048
MLA_Q_TILE = 256
MLA_Q_SUB = 1
MLA_KV_TILE = 512

VMEM_LIMIT = 56 * 1024 * 1024


def _rms(x, g):
    ms = jnp.mean(x * x, axis=-1, keepdims=True)
    return x * lax.rsqrt(ms + EPS) * g


def _dot(a, b):
    return jnp.dot(a, b, preferred_element_type=F32)


def _dot_nt(a, b):
    return lax.dot_general(a, b, (((1,), (1,)), ((), ())), preferred_element_type=F32)


def _params(*sem):
    return pltpu.CompilerParams(dimension_semantics=sem, vmem_limit_bytes=VMEM_LIMIT)


def _const_spec(shape):
    nd = len(shape)
    return pl.BlockSpec(shape, lambda *_: (0,) * nd)


def _pre_a_kernel(x_ref, g_ref, w_ref, wvt_ref, vb_ref, q_ref, k_ref, vt_ref, z_ref):
    u = _rms(x_ref[0], g_ref[...]).astype(BF16)
    qs = (A_HEAD_DIM ** -0.5) * LOG2E
    o1, o2, o3 = A_WIDTH, A_WIDTH + A_KV_WIDTH, A_WIDTH + 2 * A_KV_WIDTH
    q_ref[0] = (_dot(u, w_ref[:, :o1]) * qs).astype(BF16)
    k = _dot(u, w_ref[:, o1:o2])
    lane = lax.broadcasted_iota(jnp.int32, (k.shape[0], 2 * A_HEAD_DIM), 1)
    low = lane < A_HEAD_DIM
    for kv in range(A_KV_HEADS):
        blk = k[:, (kv // 2) * 128:(kv // 2 + 1) * 128]
        swapped = pltpu.roll(blk, A_HEAD_DIM, 1)
        first, second = (blk, swapped) if kv % 2 == 0 else (swapped, blk)
        k_ref[0, :, kv * 256:kv * 256 + 128] = jnp.where(low, first, 0.0).astype(BF16)
        k_ref[0, :, kv * 256 + 128:(kv + 1) * 256] = jnp.where(low, 0.0, second).astype(BF16)
    vt_ref[0] = (_dot_nt(wvt_ref[...], u) + vb_ref[...]).astype(BF16)
    z_ref[0] = _dot(u, w_ref[:, o3:]).astype(BF16)


def _pre_a(h, g, w_in):
    b, s, d = h.shape
    tm = min(ROW_TILE, s)
    n_in = w_in.shape[1]
    o2, o3 = A_WIDTH + A_KV_WIDTH, A_WIDTH + 2 * A_KV_WIDTH
    pad = A_V_ROWS - A_HEAD_DIM
    wv = w_in[:, o2:o3].reshape(d, A_KV_HEADS, A_HEAD_DIM)
    wvt = jnp.concatenate([wv, jnp.zeros((d, A_KV_HEADS, pad), w_in.dtype)], axis=2)
    wvt = wvt.reshape(d, A_KV_HEADS * A_V_ROWS).T.astype(BF16)
    vbias = jnp.tile(jnp.concatenate([jnp.zeros((A_HEAD_DIM,), F32), jnp.ones((pad,), F32)]),
                     A_KV_HEADS).reshape(A_KV_HEADS * A_V_ROWS, 1)
    vrows = A_KV_HEADS * A_V_ROWS
    row = lambda w: pl.BlockSpec((1, tm, w), lambda bi, i: (bi, i, 0))
    return pl.pallas_call(
        _pre_a_kernel,
        grid=(b, s // tm),
        in_specs=[row(d), _const_spec((1, d)), _const_spec((d, n_in)), _const_spec((vrows, d)),
                  _const_spec((vrows, 1))],
        out_specs=[row(A_WIDTH), row(A_KV_HEADS * 256), pl.BlockSpec((1, vrows, tm), lambda bi, i: (bi, 0, i)),
                   row(A_WIDTH)],
        out_shape=[
            jax.ShapeDtypeStruct((b, s, A_WIDTH), BF16),
            jax.ShapeDtypeStruct((b, s, A_KV_HEADS * 256), BF16),
            jax.ShapeDtypeStruct((b, vrows, s), BF16),
            jax.ShapeDtypeStruct((b, s, A_WIDTH), BF16),
        ],
        compiler_params=_params("parallel", "parallel"),
    )(h, g.reshape(1, d), w_in.astype(BF16), wvt, vbias)


def _win_kernel(sink_ref, q_ref, kp_ref, kc_ref, kn_ref, vp_ref, vc_ref, vn_ref, z_ref, o_ref,
                s0_ref, s1_ref, p0_ref, p1_ref, *, seq):
    i = pl.program_id(1)
    s_bufs, p_bufs = (s0_ref, s1_ref), (p0_ref, p1_ref)
    n_keys = WIN_Q_BLOCK + 2 * BLOCK
    r = lax.broadcasted_iota(jnp.int32, (n_keys, WIN_Q_BLOCK), 0)
    c = lax.broadcasted_iota(jnp.int32, (n_keys, WIN_Q_BLOCK), 1)
    rel = r - BLOCK - c
    k_pos = i * WIN_Q_BLOCK - BLOCK + r
    valid = (jnp.abs(rel) <= WINDOW) & (k_pos >= 0) & (k_pos < seq)
    neg_dist = jnp.where(valid, -jnp.abs(rel).astype(F32), -MASK_DIST)

    def scores(h, slot):
        kv, hp = h // A_GROUP, h // 2
        cols = slice(kv * 256 + (h % 2) * 128, kv * 256 + (h % 2 + 1) * 128)
        k_sel = jnp.concatenate([kp_ref[0, :, cols], kc_ref[0, :, cols], kn_ref[0, :, cols]], axis=0)
        slope = 2.0 ** (-8.0 * (h + 1) / A_HEADS) * LOG2E
        s = _dot_nt(k_sel, q_ref[0, :, hp * 128:(hp + 1) * 128]) + slope * neg_dist
        s_bufs[slot][...] = s
        return jnp.max(s, axis=0, keepdims=True)

    def exps(h, slot, tmax):
        m = jnp.maximum(tmax, sink_ref[h] * LOG2E)
        p_bufs[slot][...] = jnp.exp2(s_bufs[slot][...] - m).astype(BF16)
        return m

    def values(h, slot, m):
        kv = h // A_GROUP
        rows = slice(kv * A_V_ROWS, (kv + 1) * A_V_ROWS)
        vt = jnp.concatenate([vp_ref[0, rows, :], vc_ref[0, rows, :], vn_ref[0, rows, :]], axis=1)
        acc = _dot(vt, p_bufs[slot][...])
        denom = acc[A_HEAD_DIM:A_HEAD_DIM + 1] + jnp.exp2(sink_ref[h] * LOG2E - m)
        return acc[:A_HEAD_DIM] / denom

    tmax, m, pending = {}, {}, {}
    for t in range(A_HEADS + 2):
        if t >= 2:
            h = t - 2
            pending[h] = values(h, h % 2, m.pop(h))
            if h % 2 == 1:
                o_t = jnp.concatenate([pending.pop(h - 1), pending.pop(h)], axis=0)
                lanes = slice((h // 2) * 128, (h // 2 + 1) * 128)
                z = z_ref[0, :, lanes].astype(F32)
                o_ref[0, :, lanes] = (o_t.T * (z * jax.nn.sigmoid(z))).astype(BF16)
        if 1 <= t <= A_HEADS:
            m[t - 1] = exps(t - 1, (t - 1) % 2, tmax.pop(t - 1))
        if t < A_HEADS:
            tmax[t] = scores(t, t % 2)


def _win_attn(q, k_exp, vt, z, sink):
    b, s, _ = q.shape
    qb = WIN_Q_BLOCK
    assert s % qb == 0
    per = qb // BLOCK
    nb = s // BLOCK
    vrows = vt.shape[1]
    kw = k_exp.shape[-1]
    prev_i = lambda i: jnp.maximum(per * i - 1, 0)
    next_i = lambda i: jnp.minimum(per * i + per, nb - 1)
    cur = lambda w: pl.BlockSpec((1, qb, w), lambda bi, i: (bi, i, 0))
    s_buf = pltpu.VMEM((qb + 2 * BLOCK, qb), F32)
    p_buf = pltpu.VMEM((qb + 2 * BLOCK, qb), BF16)
    return pl.pallas_call(
        functools.partial(_win_kernel, seq=s),
        grid=(b, s // qb),
        in_specs=[pl.BlockSpec(memory_space=pltpu.SMEM), cur(A_WIDTH),
                  pl.BlockSpec((1, BLOCK, kw), lambda bi, i: (bi, prev_i(i), 0)),
                  cur(kw),
                  pl.BlockSpec((1, BLOCK, kw), lambda bi, i: (bi, next_i(i), 0)),
                  pl.BlockSpec((1, vrows, BLOCK), lambda bi, i: (bi, 0, prev_i(i))),
                  pl.BlockSpec((1, vrows, qb), lambda bi, i: (bi, 0, i)),
                  pl.BlockSpec((1, vrows, BLOCK), lambda bi, i: (bi, 0, next_i(i))),
                  cur(A_WIDTH)],
        out_specs=cur(A_WIDTH),
        out_shape=jax.ShapeDtypeStruct((b, s, A_WIDTH), BF16),
        scratch_shapes=[s_buf, s_buf, p_buf, p_buf],
        compiler_params=_params("parallel", "parallel"),
    )(sink.astype(F32), q, k_exp, k_exp, k_exp, vt, vt, vt, z)


def _post_kernel(og_ref, h_ref, p_ref, wo_ref, pg_ref, wg_ref, pw_ref, fg_ref, o_ref, *, final):
    h1 = h_ref[0] + _dot(og_ref[0], wo_ref[...])
    gate = jax.nn.sigmoid(_dot(_rms(h1, pg_ref[...]).astype(BF16), wg_ref[...]))
    h2 = h1 + _dot(p_ref[0].astype(BF16), pw_ref[...]) * gate
    if final:
        h2 = _rms(h2, fg_ref[...])
    o_ref[0] = h2


def _post(og, h, p, w_out, ple_g, w_gate, ple_w, final_g, final):
    b, s, d = h.shape
    tm = min(ROW_TILE, s)
    pd = p.shape[-1]
    row = lambda w: pl.BlockSpec((1, tm, w), lambda bi, i: (bi, i, 0))
    return pl.pallas_call(
        functools.partial(_post_kernel, final=final),
        grid=(b, s // tm),
        in_specs=[row(og.shape[-1]), row(d), row(pd), _const_spec(w_out.shape), _const_spec((1, d)),
                  _const_spec((d, d)), _const_spec((pd, d)), _const_spec((1, d))],
        out_specs=row(d),
        out_shape=jax.ShapeDtypeStruct((b, s, d), F32),
        compiler_params=_params("parallel", "parallel"),
    )(og, h, p, w_out.astype(BF16), ple_g.reshape(1, d), w_gate.astype(BF16), ple_w.astype(BF16),
      final_g.reshape(1, d))


def _pre_b_kernel(x_ref, g_ref, w_ref, qn_ref, wq_ref, kn_ref, wk_ref, wvt_ref, vb_ref, r_ref,
                  q_ref, k_ref, vt_ref, z_ref):
    u = _rms(x_ref[0], g_ref[...]).astype(BF16)
    o1, o2, o3 = Q_LORA, Q_LORA + KV_LORA, Q_LORA + KV_LORA + B_QK_PAD
    rot = r_ref[...]
    cq = _rms(_dot(u, w_ref[:, :o1]), qn_ref[...]).astype(BF16)
    qs = ((B_NOPE + B_ROPE) ** -0.5) * LOG2E
    q = _dot(cq, wq_ref[...]) * jnp.tile(rot * qs, (1, B_HEADS))
    q_ref[0] = q.astype(BF16)
    ckv = _rms(_dot(u, w_ref[:, o1:o2]), kn_ref[...]).astype(BF16)
    y = _dot(u, w_ref[:, o2:o3]) * rot
    lane = lax.broadcasted_iota(jnp.int32, y.shape, 1)
    both = pltpu.roll(y, 32, 1) + pltpu.roll(y, 96, 1)
    kr = y + jnp.where(lane >= B_NOPE, both, 0.0)
    k = (_dot(ckv, wk_ref[...]) + jnp.tile(kr, (1, B_HEADS))).astype(BF16)
    for hh in range(B_HEADS):
        k_ref[0, hh] = k[:, hh * B_QK_PAD:(hh + 1) * B_QK_PAD]
    vt_ref[0, 0] = (_dot_nt(wvt_ref[...], ckv) + vb_ref[...]).astype(BF16)
    z_ref[0] = _dot(u, w_ref[:, o3:]).astype(BF16)


def _pre_b(h, g, w_in_p, q_norm, wq_p, kv_norm, wk_p, wvt_p, vbias, rot):
    b, s, d = h.shape
    tm = min(MLA_KV_TILE, s)
    row = lambda w: pl.BlockSpec((1, tm, w), lambda bi, i: (bi, i, 0))
    vrows = B_HEADS * B_V_ROWS
    return pl.pallas_call(
        _pre_b_kernel,
        grid=(b, s // tm),
        in_specs=[row(d), _const_spec((1, d)), _const_spec(w_in_p.shape), _const_spec((1, Q_LORA)),
                  _const_spec(wq_p.shape), _const_spec((1, KV_LORA)), _const_spec(wk_p.shape),
                  _const_spec(wvt_p.shape), _const_spec((vrows, 1)),
                  pl.BlockSpec((tm, B_QK_PAD), lambda bi, i: (i, 0))],
        out_specs=[row(B_HEADS * B_QK_PAD),
                   pl.BlockSpec((1, B_HEADS, tm, B_QK_PAD), lambda bi, i: (bi, 0, i, 0)),
                   pl.BlockSpec((1, 1, vrows, tm), lambda bi, i: (bi, i, 0, 0)), row(B_WIDTH)],
        out_shape=[
            jax.ShapeDtypeStruct((b, s, B_HEADS * B_QK_PAD), BF16),
            jax.ShapeDtypeStruct((b, B_HEADS, s, B_QK_PAD), BF16),
            jax.ShapeDtypeStruct((b, s // tm, vrows, tm), BF16),
            jax.ShapeDtypeStruct((b, s, B_WIDTH), BF16),
        ],
        compiler_params=_params("parallel", "parallel"),
    )(h, g.reshape(1, d), w_in_p, q_norm.reshape(1, Q_LORA), wq_p, kv_norm.reshape(1, KV_LORA), wk_p,
      wvt_p, vbias, rot)


def _prep_b_weights(w_in, w_qb, w_kvb):
    half = B_ROPE // 2
    d = w_in.shape[0]
    o1, o2, o3 = Q_LORA, Q_LORA + KV_LORA, Q_LORA + KV_LORA + B_ROPE
    kr = w_in[:, o2:o3]
    kr_swapped = jnp.concatenate([kr[:, half:], kr[:, :half]], axis=1)
    kr_block = jnp.concatenate([jnp.zeros((d, B_NOPE), w_in.dtype), kr, kr_swapped], axis=1)
    w_in_p = jnp.concatenate([w_in[:, :o2], kr_block, w_in[:, o3:]], axis=1).astype(BF16)

    wq = w_qb.reshape(Q_LORA, B_HEADS, B_NOPE + B_ROPE)
    q_rope = wq[:, :, B_NOPE:]
    q_rope_swapped = jnp.concatenate([q_rope[:, :, half:], q_rope[:, :, :half]], axis=2)
    wq_p = jnp.concatenate([wq[:, :, :B_NOPE], q_rope, q_rope_swapped], axis=2)
    wq_p = wq_p.reshape(Q_LORA, B_HEADS * B_QK_PAD).astype(BF16)

    wkv = w_kvb.reshape(KV_LORA, B_HEADS, B_NOPE + B_VDIM)
    wk_p = jnp.concatenate([wkv[:, :, :B_NOPE], jnp.zeros((KV_LORA, B_HEADS, B_QK_PAD - B_NOPE), w_kvb.dtype)],
                           axis=2).reshape(KV_LORA, B_HEADS * B_QK_PAD).astype(BF16)
    wv = jnp.concatenate([wkv[:, :, B_NOPE:], jnp.zeros((KV_LORA, B_HEADS, B_V_ROWS - B_VDIM), w_kvb.dtype)],
                         axis=2).reshape(KV_LORA, B_HEADS * B_V_ROWS)
    wvt_p = wv.T.astype(BF16)
    vbias = jnp.tile(jnp.concatenate([jnp.zeros((B_VDIM,), F32), jnp.ones((B_V_ROWS - B_VDIM,), F32)]),
                     B_HEADS).reshape(B_HEADS * B_V_ROWS, 1)
    return w_in_p, wq_p, wk_p, wvt_p, vbias


def _rope_table(s):
    half = B_ROPE // 2
    inv_freq = ROPE_THETA ** (-jnp.arange(half, dtype=F32) / half)
    ang = jnp.arange(s, dtype=F32)[:, None] * inv_freq[None, :]
    cos, sin = jnp.cos(ang), jnp.sin(ang)
    return jnp.concatenate([jnp.ones((s, B_NOPE), F32), cos, cos, -sin, sin], axis=1)


def _mla_kernel(q_ref, k_ref, vt_ref, z_ref, o_ref, s0_ref, s1_ref, p0_ref, p1_ref, *, q_tile, q_sub, kv_tile):
    n_kv = vt_ref.shape[1]
    n_q = q_ref.shape[1] // (q_tile * q_sub)
    s_bufs, p_bufs = (s0_ref, s1_ref), (p0_ref, p1_ref)
    chains = [(hh, sub) for sub in range(q_sub) for hh in range(B_HEAD_PAIR)]
    m_init = (jnp.full((1, q_tile), NEG_BIG, F32),) * len(chains)

    def q_rows(qt, sub):
        return pl.ds(pl.multiple_of((qt * q_sub + sub) * q_tile, q_tile), q_tile)

    def load_q(qt):
        return [q_ref[0, q_rows(qt, sub), hh * B_QK_PAD:(hh + 1) * B_QK_PAD] for hh, sub in chains]

    def scores(qc, j, slot):
        for c, (hh, _) in enumerate(chains):
            kc = k_ref[0, hh, j * kv_tile:(j + 1) * kv_tile, :]
            s_bufs[slot][c] = _dot_nt(kc, qc[c])

    def exps(slot, m):
        m_out, alpha = [], []
        for c in range(len(chains)):
            m_new = jnp.maximum(m[c], jnp.max(s_bufs[slot][c], axis=0, keepdims=True))
            p_bufs[slot][c] = jnp.exp2(s_bufs[slot][c] - m_new).astype(BF16)
            alpha.append(jnp.exp2(m[c] - m_new))
            m_out.append(m_new)
        return tuple(m_out), tuple(alpha)

    def values(j, slot, acc, alpha):
        out = []
        for c, (hh, _) in enumerate(chains):
            vt = vt_ref[0, j, hh * B_V_ROWS:(hh + 1) * B_V_ROWS, :]
            out.append(acc[c] * alpha[c] + _dot(vt, p_bufs[slot][c]))
        return tuple(out)

    def tile_body(qt, carry):
        m, alpha = carry
        qc = load_q(qt)
        qc_next = load_q(jnp.minimum(qt + 1, n_q - 1))
        acc = (jnp.zeros((B_V_ROWS, q_tile), F32),) * len(chains)
        for t in range(2, n_kv):
            acc = values(t - 2, t % 2, acc, alpha)
            m, alpha = exps((t - 1) % 2, m)
            scores(qc, t, t % 2)
        acc = values(n_kv - 2, 0, acc, alpha)
        m, alpha = exps(1, m)
        scores(qc_next, 0, 0)
        acc = values(n_kv - 1, 1, acc, alpha)
        m_next, alpha_next = exps(0, m_init)
        scores(qc_next, 1, 1)
        for sub in range(q_sub):
            o_t = jnp.concatenate([acc[c][:B_VDIM] / acc[c][B_VDIM:B_VDIM + 1]
                                   for c, (_, s_) in enumerate(chains) if s_ == sub], axis=0)
            z = z_ref[0, q_rows(qt, sub), :].astype(F32)
            o_ref[0, q_rows(qt, sub), :] = (o_t.T * (z * jax.nn.sigmoid(z))).astype(BF16)
        return m_next, alpha_next

    qc0 = load_q(0)
    scores(qc0, 0, 0)
    m, alpha = exps(0, m_init)
    scores(qc0, 1, 1)
    lax.fori_loop(0, n_q, tile_body, (m, alpha))


def _mla_attn(q, k, vt, z):
    b, s, _ = q.shape
    n_kv, kv_tile = vt.shape[1], vt.shape[3]
    assert n_kv >= 2 and n_kv % 2 == 0
    q_block = min(MLA_Q_BLOCK, s)
    q_tile = min(MLA_Q_TILE, q_block)
    pair_w = B_HEAD_PAIR * B_QK_PAD
    out_w = B_HEAD_PAIR * B_VDIM
    q_sub = MLA_Q_SUB if q_block % (MLA_Q_SUB * q_tile) == 0 else 1
    s_buf = pltpu.VMEM((B_HEAD_PAIR * q_sub, kv_tile, q_tile), F32)
    p_buf = pltpu.VMEM((B_HEAD_PAIR * q_sub, kv_tile, q_tile), BF16)
    return pl.pallas_call(
        functools.partial(_mla_kernel, q_tile=q_tile, q_sub=q_sub, kv_tile=kv_tile),
        grid=(b, B_HEADS // B_HEAD_PAIR, s // q_block),
        in_specs=[pl.BlockSpec((1, q_block, pair_w), lambda bi, hp, qi: (bi, qi, hp)),
                  pl.BlockSpec((1, B_HEAD_PAIR, s, B_QK_PAD), lambda bi, hp, qi: (bi, hp, 0, 0)),
                  pl.BlockSpec((1, n_kv, B_HEAD_PAIR * B_V_ROWS, kv_tile), lambda bi, hp, qi: (bi, 0, hp, 0)),
                  pl.BlockSpec((1, q_block, out_w), lambda bi, hp, qi: (bi, qi, hp))],
        out_specs=pl.BlockSpec((1, q_block, out_w), lambda bi, hp, qi: (bi, qi, hp)),
        out_shape=jax.ShapeDtypeStruct((b, s, B_WIDTH), BF16),
        scratch_shapes=[s_buf, s_buf, p_buf, p_buf],
        compiler_params=_params("parallel", "parallel", "arbitrary"),
    )(q, k, vt, z)


def kernel(x, p, norm_g, a_w_in, a_sink, a_w_out, b_w_in, b_q_norm, b_w_qb, b_kv_norm, b_w_kvb, b_w_out,
           ple_w, ple_norm_g, ple_w_gate, final_norm_g):
    depth = p.shape[0]
    s = x.shape[1]
    rot = _rope_table(s)
    h = x
    for i in range(depth):
        j = i // 2
        if i % 2 == 0:
            q, k_exp, vt, z = _pre_a(h, norm_g[i], a_w_in[j])
            og = _win_attn(q, k_exp, vt, z, a_sink[j])
            w_out = a_w_out[j]
        else:
            w_in_p, wq_p, wk_p, wvt_p, vbias = _prep_b_weights(b_w_in[j], b_w_qb[j], b_w_kvb[j])
            q, k, vt, z = _pre_b(h, norm_g[i], w_in_p, b_q_norm[j], wq_p, b_kv_norm[j], wk_p, wvt_p, vbias, rot)
            og = _mla_attn(q, k, vt, z)
            w_out = b_w_out[j]
        h = _post(og, h, p[i], w_out, ple_norm_g[i], ple_w_gate[i], ple_w[i], final_norm_g, final=(i == depth - 1))
    return h
```

```python
import functools

import jax
import jax.numpy as jnp
from jax import lax
from jax.experimental import pallas as pl
from jax.experimental.pallas import tpu as pltpu

F32 = jnp.float32
BF16 = jnp.bfloat16

EPS = 1e-6
NEG_BIG = -1e30
LOG2E = 1.4426950408889634

A_HEADS = 16
A_KV_HEADS = 4
A_GROUP = A_HEADS // A_KV_HEADS
A_HEAD_DIM = 64
A_WIDTH = A_HEADS * A_HEAD_DIM
A_KV_WIDTH = A_KV_HEADS * A_HEAD_DIM
WINDOW = 128
BLOCK = 128
A_V_ROWS = 80
WIN_Q_BLOCK = 256
MASK_DIST = 1e33

B_HEADS = 16
B_NOPE = 64
B_ROPE = 32
B_VDIM = 64
B_WIDTH = B_HEADS * B_VDIM
Q_LORA = 384
KV_LORA = 256
ROPE_THETA = 10000.0
B_QK_PAD = 128
B_V_ROWS = 80
B_HEAD_PAIR = 2

ROW_TILE = 512
MLA_Q_BLOCK = 2048
MLA_Q_TILE = 256
MLA_Q_SUB = 1
MLA_KV_TILE = 512

VMEM_LIMIT = 56 * 1024 * 1024


def _rms(x, g):
    ms = jnp.mean(x * x, axis=-1, keepdims=True)
    return x * lax.rsqrt(ms + EPS) * g


def _dot(a, b):
    return jnp.dot(a, b, preferred_element_type=F32)


def _dot_nt(a, b):
    return lax.dot_general(a, b, (((1,), (1,)), ((), ())), preferred_element_type=F32)


def _params(*sem):
    return pltpu.CompilerParams(dimension_semantics=sem, vmem_limit_bytes=VMEM_LIMIT)


def _const_spec(shape):
    nd = len(shape)
    return pl.BlockSpec(shape, lambda *_: (0,) * nd)


def _pre_a_kernel(x_ref, g_ref, w_ref, wvt_ref, vb_ref, q_ref, k_ref, vt_ref, z_ref):
    u = _rms(x_ref[0], g_ref[...]).astype(BF16)
    qs = (A_HEAD_DIM ** -0.5) * LOG2E
    o1, o2, o3 = A_WIDTH, A_WIDTH + A_KV_WIDTH, A_WIDTH + 2 * A_KV_WIDTH
    q_ref[0] = (_dot(u, w_ref[:, :o1]) * qs).astype(BF16)
    k = _dot(u, w_ref[:, o1:o2])
    lane = lax.broadcasted_iota(jnp.int32, (k.shape[0], 2 * A_HEAD_DIM), 1)
    low = lane < A_HEAD_DIM
    for kv in range(A_KV_HEADS):
        blk = k[:, (kv // 2) * 128:(kv // 2 + 1) * 128]
        swapped = pltpu.roll(blk, A_HEAD_DIM, 1)
        first, second = (blk, swapped) if kv % 2 == 0 else (swapped, blk)
        k_ref[0, :, kv * 256:kv * 256 + 128] = jnp.where(low, first, 0.0).astype(BF16)
        k_ref[0, :, kv * 256 + 128:(kv + 1) * 256] = jnp.where(low, 0.0, second).astype(BF16)
    vt_ref[0] = (_dot_nt(wvt_ref[...], u) + vb_ref[...]).astype(BF16)
    z_ref[0] = _dot(u, w_ref[:, o3:]).astype(BF16)


def _pre_a(h, g, w_in):
    b, s, d = h.shape
    tm = min(ROW_TILE, s)
    n_in = w_in.shape[1]
    o2, o3 = A_WIDTH + A_KV_WIDTH, A_WIDTH + 2 * A_KV_WIDTH
    pad = A_V_ROWS - A_HEAD_DIM
    wv = w_in[:, o2:o3].reshape(d, A_KV_HEADS, A_HEAD_DIM)
    wvt = jnp.concatenate([wv, jnp.zeros((d, A_KV_HEADS, pad), w_in.dtype)], axis=2)
    wvt = wvt.reshape(d, A_KV_HEADS * A_V_ROWS).T.astype(BF16)
    vbias = jnp.tile(jnp.concatenate([jnp.zeros((A_HEAD_DIM,), F32), jnp.ones((pad,), F32)]),
                     A_KV_HEADS).reshape(A_KV_HEADS * A_V_ROWS, 1)
    vrows = A_KV_HEADS * A_V_ROWS
    row = lambda w: pl.BlockSpec((1, tm, w), lambda bi, i: (bi, i, 0))
    return pl.pallas_call(
        _pre_a_kernel,
        grid=(b, s // tm),
        in_specs=[row(d), _const_spec((1, d)), _const_spec((d, n_in)), _const_spec((vrows, d)),
                  _const_spec((vrows, 1))],
        out_specs=[row(A_WIDTH), row(A_KV_HEADS * 256), pl.BlockSpec((1, vrows, tm), lambda bi, i: (bi, 0, i)),
                   row(A_WIDTH)],
        out_shape=[
            jax.ShapeDtypeStruct((b, s, A_WIDTH), BF16),
            jax.ShapeDtypeStruct((b, s, A_KV_HEADS * 256), BF16),
            jax.ShapeDtypeStruct((b, vrows, s), BF16),
            jax.ShapeDtypeStruct((b, s, A_WIDTH), BF16),
        ],
        compiler_params=_params("parallel", "parallel"),
    )(h, g.reshape(1, d), w_in.astype(BF16), wvt, vbias)


def _win_kernel(sink_ref, q_ref, kp_ref, kc_ref, kn_ref, vp_ref, vc_ref, vn_ref, z_ref, o_ref,
                s0_ref, s1_ref, p0_ref, p1_ref, *, seq):
    i = pl.program_id(1)
    s_bufs, p_bufs = (s0_ref, s1_ref), (p0_ref, p1_ref)
    n_keys = WIN_Q_BLOCK + 2 * BLOCK
    r = lax.broadcasted_iota(jnp.int32, (n_keys, WIN_Q_BLOCK), 0)
    c = lax.broadcasted_iota(jnp.int32, (n_keys, WIN_Q_BLOCK), 1)
    rel = r - BLOCK - c
    k_pos = i * WIN_Q_BLOCK - BLOCK + r
    valid = (jnp.abs(rel) <= WINDOW) & (k_pos >= 0) & (k_pos < seq)
    neg_dist = jnp.where(valid, -jnp.abs(rel).astype(F32), -MASK_DIST)

    def scores(h, slot):
        kv, hp = h // A_GROUP, h // 2
        cols = slice(kv * 256 + (h % 2) * 128, kv * 256 + (h % 2 + 1) * 128)
        k_sel = jnp.concatenate([kp_ref[0, :, cols], kc_ref[0, :, cols], kn_ref[0, :, cols]], axis=0)
        slope = 2.0 ** (-8.0 * (h + 1) / A_HEADS) * LOG2E
        s = _dot_nt(k_sel, q_ref[0, :, hp * 128:(hp + 1) * 128]) + slope * neg_dist
        s_bufs[slot][...] = s
        return jnp.max(s, axis=0, keepdims=True)

    def exps(h, slot, tmax):
        m = jnp.maximum(tmax, sink_ref[h] * LOG2E)
        p_bufs[slot][...] = jnp.exp2(s_bufs[slot][...] - m).astype(BF16)
        return m

    def values(h, slot, m):
        kv = h // A_GROUP
        rows = slice(kv * A_V_ROWS, (kv + 1) * A_V_ROWS)
        vt = jnp.concatenate([vp_ref[0, rows, :], vc_ref[0, rows, :], vn_ref[0, rows, :]], axis=1)
        acc = _dot(vt, p_bufs[slot][...])
        denom = acc[A_HEAD_DIM:A_HEAD_DIM + 1] + jnp.exp2(sink_ref[h] * LOG2E - m)
        return acc[:A_HEAD_DIM] / denom

    tmax, m, pending = {}, {}, {}
    for t in range(A_HEADS + 2):
        if t >= 2:
            h = t - 2
            pending[h] = values(h, h % 2, m.pop(h))
            if h % 2 == 1:
                o_t = jnp.concatenate([pending.pop(h - 1), pending.pop(h)], axis=0)
                lanes = slice((h // 2) * 128, (h // 2 + 1) * 128)
                z = z_ref[0, :, lanes].astype(F32)
                o_ref[0, :, lanes] = (o_t.T * (z * jax.nn.sigmoid(z))).astype(BF16)
        if 1 <= t <= A_HEADS:
            m[t - 1] = exps(t - 1, (t - 1) % 2, tmax.pop(t - 1))
        if t < A_HEADS:
            tmax[t] = scores(t, t % 2)


def _win_attn(q, k_exp, vt, z, sink):
    b, s, _ = q.shape
    qb = WIN_Q_BLOCK
    assert s % qb == 0
    per = qb // BLOCK
    nb = s // BLOCK
    vrows = vt.shape[1]
    kw = k_exp.shape[-1]
    prev_i = lambda i: jnp.maximum(per * i - 1, 0)
    next_i = lambda i: jnp.minimum(per * i + per, nb - 1)
    cur = lambda w: pl.BlockSpec((1, qb, w), lambda bi, i: (bi, i, 0))
    s_buf = pltpu.VMEM((qb + 2 * BLOCK, qb), F32)
    p_buf = pltpu.VMEM((qb + 2 * BLOCK, qb), BF16)
    return pl.pallas_call(
        functools.partial(_win_kernel, seq=s),
        grid=(b, s // qb),
        in_specs=[pl.BlockSpec(memory_space=pltpu.SMEM), cur(A_WIDTH),
                  pl.BlockSpec((1, BLOCK, kw), lambda bi, i: (bi, prev_i(i), 0)),
                  cur(kw),
                  pl.BlockSpec((1, BLOCK, kw), lambda bi, i: (bi, next_i(i), 0)),
                  pl.BlockSpec((1, vrows, BLOCK), lambda bi, i: (bi, 0, prev_i(i))),
                  pl.BlockSpec((1, vrows, qb), lambda bi, i: (bi, 0, i)),
                  pl.BlockSpec((1, vrows, BLOCK), lambda bi, i: (bi, 0, next_i(i))),
                  cur(A_WIDTH)],
        out_specs=cur(A_WIDTH),
        out_shape=jax.ShapeDtypeStruct((b, s, A_WIDTH), BF16),
        scratch_shapes=[s_buf, s_buf, p_buf, p_buf],
        compiler_params=_params("parallel", "parallel"),
    )(sink.astype(F32), q, k_exp, k_exp, k_exp, vt, vt, vt, z)


def _post_kernel(og_ref, h_ref, p_ref, wo_ref, pg_ref, wg_ref, pw_ref, fg_ref, o_ref, *, final):
    h1 = h_ref[0] + _dot(og_ref[0], wo_ref[...])
    gate = jax.nn.sigmoid(_dot(_rms(h1, pg_ref[...]).astype(BF16), wg_ref[...]))
    h2 = h1 + _dot(p_ref[0].astype(BF16), pw_ref[...]) * gate
    if final:
        h2 = _rms(h2, fg_ref[...])
    o_ref[0] = h2


def _post(og, h, p, w_out, ple_g, w_gate, ple_w, final_g, final):
    b, s, d = h.shape
    tm = min(ROW_TILE, s)
    pd = p.shape[-1]
    row = lambda w: pl.BlockSpec((1, tm, w), lambda bi, i: (bi, i, 0))
    return pl.pallas_call(
        functools.partial(_post_kernel, final=final),
        grid=(b, s // tm),
        in_specs=[row(og.shape[-1]), row(d), row(pd), _const_spec(w_out.shape), _const_spec((1, d)),
                  _const_spec((d, d)), _const_spec((pd, d)), _const_spec((1, d))],
        out_specs=row(d),
        out_shape=jax.ShapeDtypeStruct((b, s, d), F32),
        compiler_params=_params("parallel", "parallel"),
    )(og, h, p, w_out.astype(BF16), ple_g.reshape(1, d), w_gate.astype(BF16), ple_w.astype(BF16),
      final_g.reshape(1, d))


def _pre_b_kernel(x_ref, g_ref, w_ref, qn_ref, wq_ref, kn_ref, wk_ref, wvt_ref, vb_ref, r_ref,
                  q_ref, k_ref, vt_ref, z_ref):
    u = _rms(x_ref[0], g_ref[...]).astype(BF16)
    o1, o2, o3 = Q_LORA, Q_LORA + KV_LORA, Q_LORA + KV_LORA + B_QK_PAD
    rot = r_ref[...]
    cq = _rms(_dot(u, w_ref[:, :o1]), qn_ref[...]).astype(BF16)
    qs = ((B_NOPE + B_ROPE) ** -0.5) * LOG2E
    q = _dot(cq, wq_ref[...]) * jnp.tile(rot * qs, (1, B_HEADS))
    q_ref[0] = q.astype(BF16)
    ckv = _rms(_dot(u, w_ref[:, o1:o2]), kn_ref[...]).astype(BF16)
    y = _dot(u, w_ref[:, o2:o3]) * rot
    lane = lax.broadcasted_iota(jnp.int32, y.shape, 1)
    both = pltpu.roll(y, 32, 1) + pltpu.roll(y, 96, 1)
    kr = y + jnp.where(lane >= B_NOPE, both, 0.0)
    k = (_dot(ckv, wk_ref[...]) + jnp.tile(kr, (1, B_HEADS))).astype(BF16)
    for hh in range(B_HEADS):
        k_ref[0, hh] = k[:, hh * B_QK_PAD:(hh + 1) * B_QK_PAD]
    vt_ref[0, 0] = (_dot_nt(wvt_ref[...], ckv) + vb_ref[...]).astype(BF16)
    z_ref[0] = _dot(u, w_ref[:, o3:]).astype(BF16)


def _pre_b(h, g, w_in_p, q_norm, wq_p, kv_norm, wk_p, wvt_p, vbias, rot):
    b, s, d = h.shape
    tm = min(MLA_KV_TILE, s)
    row = lambda w: pl.BlockSpec((1, tm, w), lambda bi, i: (bi, i, 0))
    vrows = B_HEADS * B_V_ROWS
    return pl.pallas_call(
        _pre_b_kernel,
        grid=(b, s // tm),
        in_specs=[row(d), _const_spec((1, d)), _const_spec(w_in_p.shape), _const_spec((1, Q_LORA)),
                  _const_spec(wq_p.shape), _const_spec((1, KV_LORA)), _const_spec(wk_p.shape),
                  _const_spec(wvt_p.shape), _const_spec((vrows, 1)),
                  pl.BlockSpec((tm, B_QK_PAD), lambda bi, i: (i, 0))],
        out_specs=[row(B_HEADS * B_QK_PAD),
                   pl.BlockSpec((1, B_HEADS, tm, B_QK_PAD), lambda bi, i: (bi, 0, i, 0)),
                   pl.BlockSpec((1, 1, vrows, tm), lambda bi, i: (bi, i, 0, 0)), row(B_WIDTH)],
        out_shape=[
            jax.ShapeDtypeStruct((b, s, B_HEADS * B_QK_PAD), BF16),
            jax.ShapeDtypeStruct((b, B_HEADS, s, B_QK_PAD), BF16),
            jax.ShapeDtypeStruct((b, s // tm, vrows, tm), BF16),
            jax.ShapeDtypeStruct((b, s, B_WIDTH), BF16),
        ],
        compiler_params=_params("parallel", "parallel"),
    )(h, g.reshape(1, d), w_in_p, q_norm.reshape(1, Q_LORA), wq_p, kv_norm.reshape(1, KV_LORA), wk_p,
      wvt_p, vbias, rot)


def _prep_b_weights(w_in, w_qb, w_kvb):
    half = B_ROPE // 2
    d = w_in.shape[0]
    o1, o2, o3 = Q_LORA, Q_LORA + KV_LORA, Q_LORA + KV_LORA + B_ROPE
    kr = w_in[:, o2:o3]
    kr_swapped = jnp.concatenate([kr[:, half:], kr[:, :half]], axis=1)
    kr_block = jnp.concatenate([jnp.zeros((d, B_NOPE), w_in.dtype), kr, kr_swapped], axis=1)
    w_in_p = jnp.concatenate([w_in[:, :o2], kr_block, w_in[:, o3:]], axis=1).astype(BF16)

    wq = w_qb.reshape(Q_LORA, B_HEADS, B_NOPE + B_ROPE)
    q_rope = wq[:, :, B_NOPE:]
    q_rope_swapped = jnp.concatenate([q_rope[:, :, half:], q_rope[:, :, :half]], axis=2)
    wq_p = jnp.concatenate([wq[:, :, :B_NOPE], q_rope, q_rope_swapped], axis=2)
    wq_p = wq_p.reshape(Q_LORA, B_HEADS * B_QK_PAD).astype(BF16)

    wkv = w_kvb.reshape(KV_LORA, B_HEADS, B_NOPE + B_VDIM)
    wk_p = jnp.concatenate([wkv[:, :, :B_NOPE], jnp.zeros((KV_LORA, B_HEADS, B_QK_PAD - B_NOPE), w_kvb.dtype)],
                           axis=2).reshape(KV_LORA, B_HEADS * B_QK_PAD).astype(BF16)
    wv = jnp.concatenate([wkv[:, :, B_NOPE:], jnp.zeros((KV_LORA, B_HEADS, B_V_ROWS - B_VDIM), w_kvb.dtype)],
                         axis=2).reshape(KV_LORA, B_HEADS * B_V_ROWS)
    wvt_p = wv.T.astype(BF16)
    vbias = jnp.tile(jnp.concatenate([jnp.zeros((B_VDIM,), F32), jnp.ones((B_V_ROWS - B_VDIM,), F32)]),
                     B_HEADS).reshape(B_HEADS * B_V_ROWS, 1)
    return w_in_p, wq_p, wk_p, wvt_p, vbias


def _rope_table(s):
    half = B_ROPE // 2
    inv_freq = ROPE_THETA ** (-jnp.arange(half, dtype=F32) / half)
    ang = jnp.arange(s, dtype=F32)[:, None] * inv_freq[None, :]
    cos, sin = jnp.cos(ang), jnp.sin(ang)
    return jnp.concatenate([jnp.ones((s, B_NOPE), F32), cos, cos, -sin, sin], axis=1)


def _mla_kernel(q_ref, k_ref, vt_ref, z_ref, o_ref, s0_ref, s1_ref, p0_ref, p1_ref, *, q_tile, q_sub, kv_tile):
    n_kv = vt_ref.shape[1]
    n_q = q_ref.shape[1] // (q_tile * q_sub)
    s_bufs, p_bufs = (s0_ref, s1_ref), (p0_ref, p1_ref)
    chains = [(hh, sub) for sub in range(q_sub) for hh in range(B_HEAD_PAIR)]
    m_init = (jnp.full((1, q_tile), NEG_BIG, F32),) * len(chains)

    def q_rows(qt, sub):
        return pl.ds(pl.multiple_of((qt * q_sub + sub) * q_tile, q_tile), q_tile)

    def load_q(qt):
        return [q_ref[0, q_rows(qt, sub), hh * B_QK_PAD:(hh + 1) * B_QK_PAD] for hh, sub in chains]

    def score_chain(qc, j, slot, c):
        kc = k_ref[0, chains[c][0], j * kv_tile:(j + 1) * kv_tile, :]
        s_bufs[slot][c] = _dot_nt(kc, qc[c])

    def scores(qc, j, slot):
        for c in range(len(chains)):
            score_chain(qc, j, slot, c)

    def exps(slot, m):
        m_out, alpha = [], []
        for c in range(len(chains)):
            m_new = jnp.maximum(m[c], jnp.max(s_bufs[slot][c], axis=0, keepdims=True))
            p_bufs[slot][c] = jnp.exp2(s_bufs[slot][c] - m_new).astype(BF16)
            alpha.append(jnp.exp2(m[c] - m_new))
            m_out.append(m_new)
        return tuple(m_out), tuple(alpha)

    def value_chain(j, slot, acc, alpha, c):
        vt = vt_ref[0, j, chains[c][0] * B_V_ROWS:(chains[c][0] + 1) * B_V_ROWS, :]
        return acc[c] * alpha[c] + _dot(vt, p_bufs[slot][c])

    def values(j, slot, acc, alpha):
        return tuple(value_chain(j, slot, acc, alpha, c) for c in range(len(chains)))

    def scores_and_values(qc, j_s, s_slot, j_v, v_slot, acc, alpha):
        out = []
        for c in range(len(chains)):
            score_chain(qc, j_s, s_slot, c)
            out.append(value_chain(j_v, v_slot, acc, alpha, c))
        return tuple(out)

    def tile_body(qt, carry):
        m, alpha = carry
        qc = load_q(qt)
        qc_next = load_q(jnp.minimum(qt + 1, n_q - 1))
        acc = (jnp.zeros((B_V_ROWS, q_tile), F32),) * len(chains)
        for t in range(2, n_kv):
            acc = scores_and_values(qc, t, t % 2, t - 2, t % 2, acc, alpha)
            m, alpha = exps((t - 1) % 2, m)
        acc = scores_and_values(qc_next, 0, 0, n_kv - 2, 0, acc, alpha)
        m, alpha = exps(1, m)
        acc = scores_and_values(qc_next, 1, 1, n_kv - 1, 1, acc, alpha)
        m_next, alpha_next = exps(0, m_init)
        for sub in range(q_sub):
            o_t = jnp.concatenate([acc[c][:B_VDIM] / acc[c][B_VDIM:B_VDIM + 1]
                                   for c, (_, s_) in enumerate(chains) if s_ == sub], axis=0)
            z = z_ref[0, q_rows(qt, sub), :].astype(F32)
            o_ref[0, q_rows(qt, sub), :] = (o_t.T * (z * jax.nn.sigmoid(z))).astype(BF16)
        return m_next, alpha_next

    qc0 = load_q(0)
    scores(qc0, 0, 0)
    m, alpha = exps(0, m_init)
    scores(qc0, 1, 1)
    lax.fori_loop(0, n_q, tile_body, (m, alpha))


def _mla_attn(q, k, vt, z):
    b, s, _ = q.shape
    n_kv, kv_tile = vt.shape[1], vt.shape[3]
    assert n_kv >= 2 and n_kv % 2 == 0
    q_block = min(MLA_Q_BLOCK, s)
    q_tile = min(MLA_Q_TILE, q_block)
    pair_w = B_HEAD_PAIR * B_QK_PAD
    out_w = B_HEAD_PAIR * B_VDIM
    q_sub = MLA_Q_SUB if q_block % (MLA_Q_SUB * q_tile) == 0 else 1
    s_buf = pltpu.VMEM((B_HEAD_PAIR * q_sub, kv_tile, q_tile), F32)
    p_buf = pltpu.VMEM((B_HEAD_PAIR * q_sub, kv_tile, q_tile), BF16)
    return pl.pallas_call(
        functools.partial(_mla_kernel, q_tile=q_tile, q_sub=q_sub, kv_tile=kv_tile),
        grid=(b, B_HEADS // B_HEAD_PAIR, s // q_block),
        in_specs=[pl.BlockSpec((1, q_block, pair_w), lambda bi, hp, qi: (bi, qi, hp)),
                  pl.BlockSpec((1, B_HEAD_PAIR, s, B_QK_PAD), lambda bi, hp, qi: (bi, hp, 0, 0)),
                  pl.BlockSpec((1, n_kv, B_HEAD_PAIR * B_V_ROWS, kv_tile), lambda bi, hp, qi: (bi, 0, hp, 0)),
                  pl.BlockSpec((1, q_block, out_w), lambda bi, hp, qi: (bi, qi, hp))],
        out_specs=pl.BlockSpec((1, q_block, out_w), lambda bi, hp, qi: (bi, qi, hp)),
        out_shape=jax.ShapeDtypeStruct((b, s, B_WIDTH), BF16),
        scratch_shapes=[s_buf, s_buf, p_buf, p_buf],
        compiler_params=_params("parallel", "parallel", "arbitrary"),
    )(q, k, vt, z)


def kernel(x, p, norm_g, a_w_in, a_sink, a_w_out, b_w_in, b_q_norm, b_w_qb, b_kv_norm, b_w_kvb, b_w_out,
           ple_w, ple_norm_g, ple_w_gate, final_norm_g):
    depth = p.shape[0]
    s = x.shape[1]
    rot = _rope_table(s)
    h = x
    for i in range(depth):
        j = i // 2
        if i % 2 == 0:
            q, k_exp, vt, z = _pre_a(h, norm_g[i], a_w_in[j])
            og = _win_attn(q, k_exp, vt, z, a_sink[j])
            w_out = a_w_out[j]
        else:
            w_in_p, wq_p, wk_p, wvt_p, vbias = _prep_b_weights(b_w_in[j], b_w_qb[j], b_w_kvb[j])
            q, k, vt, z = _pre_b(h, norm_g[i], w_in_p, b_q_norm[j], wq_p, b_kv_norm[j], wk_p, wvt_p, vbias, rot)
            og = _mla_attn(q, k, vt, z)
            w_out = b_w_out[j]
        h = _post(og, h, p[i], w_out, ple_norm_g[i], ple_w_gate[i], ple_w[i], final_norm_g, final=(i == depth - 1))
    return h
```

```python
import functools

import jax
import jax.numpy as jnp
from jax import lax
from jax.experimental import pallas as pl
from jax.experimental.pallas import tpu as pltpu

F32 = jnp.float32
BF16 = jnp.bfloat16

EPS = 1e-6
NEG_BIG = -1e30
LOG2E = 1.4426950408889634

A_HEADS = 16
A_KV_HEADS = 4
A_GROUP = A_HEADS // A_KV_HEADS
A_HEAD_DIM = 64
A_WIDTH = A_HEADS * A_HEAD_DIM
A_KV_WIDTH = A_KV_HEADS * A_HEAD_DIM
WINDOW = 128
BLOCK = 128
A_V_ROWS = 80
WIN_Q_BLOCK = 256
MASK_DIST = 1e33

B_HEADS = 16
B_NOPE = 64
B_ROPE = 32
B_VDIM = 64
B_WIDTH = B_HEADS * B_VDIM
Q_LORA = 384
KV_LORA = 256
ROPE_THETA = 10000.0
B_QK_PAD = 128
B_V_ROWS = 80
B_HEAD_PAIR = 2

ROW_TILE = 512
MLA_Q_BLOCK = 2048
MLA_Q_TILE = 256
STEP_ORDER = "sve"
MLA_Q_SUB = 1
MLA_KV_TILE = 512

VMEM_LIMIT = 56 * 1024 * 1024


def _rms(x, g):
    ms = jnp.mean(x * x, axis=-1, keepdims=True)
    return x * lax.rsqrt(ms + EPS) * g


def _dot(a, b):
    return jnp.dot(a, b, preferred_element_type=F32)


def _dot_nt(a, b):
    return lax.dot_general(a, b, (((1,), (1,)), ((), ())), preferred_element_type=F32)


def _params(*sem):
    return pltpu.CompilerParams(dimension_semantics=sem, vmem_limit_bytes=VMEM_LIMIT)


def _const_spec(shape):
    nd = len(shape)
    return pl.BlockSpec(shape, lambda *_: (0,) * nd)


def _pre_a_kernel(x_ref, g_ref, w_ref, wvt_ref, vb_ref, q_ref, k_ref, vt_ref, z_ref):
    u = _rms(x_ref[0], g_ref[...]).astype(BF16)
    qs = (A_HEAD_DIM ** -0.5) * LOG2E
    o1, o2, o3 = A_WIDTH, A_WIDTH + A_KV_WIDTH, A_WIDTH + 2 * A_KV_WIDTH
    q_ref[0] = (_dot(u, w_ref[:, :o1]) * qs).astype(BF16)
    k = _dot(u, w_ref[:, o1:o2])
    lane = lax.broadcasted_iota(jnp.int32, (k.shape[0], 2 * A_HEAD_DIM), 1)
    low = lane < A_HEAD_DIM
    for kv in range(A_KV_HEADS):
        blk = k[:, (kv // 2) * 128:(kv // 2 + 1) * 128]
        swapped = pltpu.roll(blk, A_HEAD_DIM, 1)
        first, second = (blk, swapped) if kv % 2 == 0 else (swapped, blk)
        k_ref[0, :, kv * 256:kv * 256 + 128] = jnp.where(low, first, 0.0).astype(BF16)
        k_ref[0, :, kv * 256 + 128:(kv + 1) * 256] = jnp.where(low, 0.0, second).astype(BF16)
    vt_ref[0] = (_dot_nt(wvt_ref[...], u) + vb_ref[...]).astype(BF16)
    z_ref[0] = _dot(u, w_ref[:, o3:]).astype(BF16)


def _pre_a(h, g, w_in):
    b, s, d = h.shape
    tm = min(ROW_TILE, s)
    n_in = w_in.shape[1]
    o2, o3 = A_WIDTH + A_KV_WIDTH, A_WIDTH + 2 * A_KV_WIDTH
    pad = A_V_ROWS - A_HEAD_DIM
    wv = w_in[:, o2:o3].reshape(d, A_KV_HEADS, A_HEAD_DIM)
    wvt = jnp.concatenate([wv, jnp.zeros((d, A_KV_HEADS, pad), w_in.dtype)], axis=2)
    wvt = wvt.reshape(d, A_KV_HEADS * A_V_ROWS).T.astype(BF16)
    vbias = jnp.tile(jnp.concatenate([jnp.zeros((A_HEAD_DIM,), F32), jnp.ones((pad,), F32)]),
                     A_KV_HEADS).reshape(A_KV_HEADS * A_V_ROWS, 1)
    vrows = A_KV_HEADS * A_V_ROWS
    row = lambda w: pl.BlockSpec((1, tm, w), lambda bi, i: (bi, i, 0))
    return pl.pallas_call(
        _pre_a_kernel,
        grid=(b, s // tm),
        in_specs=[row(d), _const_spec((1, d)), _const_spec((d, n_in)), _const_spec((vrows, d)),
                  _const_spec((vrows, 1))],
        out_specs=[row(A_WIDTH), row(A_KV_HEADS * 256), pl.BlockSpec((1, vrows, tm), lambda bi, i: (bi, 0, i)),
                   row(A_WIDTH)],
        out_shape=[
            jax.ShapeDtypeStruct((b, s, A_WIDTH), BF16),
            jax.ShapeDtypeStruct((b, s, A_KV_HEADS * 256), BF16),
            jax.ShapeDtypeStruct((b, vrows, s), BF16),
            jax.ShapeDtypeStruct((b, s, A_WIDTH), BF16),
        ],
        compiler_params=_params("parallel", "parallel"),
    )(h, g.reshape(1, d), w_in.astype(BF16), wvt, vbias)


def _win_kernel(sink_ref, q_ref, kp_ref, kc_ref, kn_ref, vp_ref, vc_ref, vn_ref, z_ref, o_ref,
                s0_ref, s1_ref, p0_ref, p1_ref, *, seq):
    i = pl.program_id(1)
    s_bufs, p_bufs = (s0_ref, s1_ref), (p0_ref, p1_ref)
    n_keys = WIN_Q_BLOCK + 2 * BLOCK
    r = lax.broadcasted_iota(jnp.int32, (n_keys, WIN_Q_BLOCK), 0)
    c = lax.broadcasted_iota(jnp.int32, (n_keys, WIN_Q_BLOCK), 1)
    rel = r - BLOCK - c
    k_pos = i * WIN_Q_BLOCK - BLOCK + r
    valid = (jnp.abs(rel) <= WINDOW) & (k_pos >= 0) & (k_pos < seq)
    neg_dist = jnp.where(valid, -jnp.abs(rel).astype(F32), -MASK_DIST)

    def scores(h, slot):
        kv, hp = h // A_GROUP, h // 2
        cols = slice(kv * 256 + (h % 2) * 128, kv * 256 + (h % 2 + 1) * 128)
        k_sel = jnp.concatenate([kp_ref[0, :, cols], kc_ref[0, :, cols], kn_ref[0, :, cols]], axis=0)
        slope = 2.0 ** (-8.0 * (h + 1) / A_HEADS) * LOG2E
        s = _dot_nt(k_sel, q_ref[0, :, hp * 128:(hp + 1) * 128]) + slope * neg_dist
        s_bufs[slot][...] = s
        return jnp.max(s, axis=0, keepdims=True)

    def exps(h, slot, tmax):
        m = jnp.maximum(tmax, sink_ref[h] * LOG2E)
        p_bufs[slot][...] = jnp.exp2(s_bufs[slot][...] - m).astype(BF16)
        return m

    def values(h, slot, m):
        kv = h // A_GROUP
        rows = slice(kv * A_V_ROWS, (kv + 1) * A_V_ROWS)
        vt = jnp.concatenate([vp_ref[0, rows, :], vc_ref[0, rows, :], vn_ref[0, rows, :]], axis=1)
        acc = _dot(vt, p_bufs[slot][...])
        denom = acc[A_HEAD_DIM:A_HEAD_DIM + 1] + jnp.exp2(sink_ref[h] * LOG2E - m)
        return acc[:A_HEAD_DIM] / denom

    tmax, m, pending = {}, {}, {}
    for t in range(A_HEADS + 2):
        if t >= 2:
            h = t - 2
            pending[h] = values(h, h % 2, m.pop(h))
            if h % 2 == 1:
                o_t = jnp.concatenate([pending.pop(h - 1), pending.pop(h)], axis=0)
                lanes = slice((h // 2) * 128, (h // 2 + 1) * 128)
                z = z_ref[0, :, lanes].astype(F32)
                o_ref[0, :, lanes] = (o_t.T * (z * jax.nn.sigmoid(z))).astype(BF16)
        if 1 <= t <= A_HEADS:
            m[t - 1] = exps(t - 1, (t - 1) % 2, tmax.pop(t - 1))
        if t < A_HEADS:
            tmax[t] = scores(t, t % 2)


def _win_attn(q, k_exp, vt, z, sink):
    b, s, _ = q.shape
    qb = WIN_Q_BLOCK
    assert s % qb == 0
    per = qb // BLOCK
    nb = s // BLOCK
    vrows = vt.shape[1]
    kw = k_exp.shape[-1]
    prev_i = lambda i: jnp.maximum(per * i - 1, 0)
    next_i = lambda i: jnp.minimum(per * i + per, nb - 1)
    cur = lambda w: pl.BlockSpec((1, qb, w), lambda bi, i: (bi, i, 0))
    s_buf = pltpu.VMEM((qb + 2 * BLOCK, qb), F32)
    p_buf = pltpu.VMEM((qb + 2 * BLOCK, qb), BF16)
    return pl.pallas_call(
        functools.partial(_win_kernel, seq=s),
        grid=(b, s // qb),
        in_specs=[pl.BlockSpec(memory_space=pltpu.SMEM), cur(A_WIDTH),
                  pl.BlockSpec((1, BLOCK, kw), lambda bi, i: (bi, prev_i(i), 0)),
                  cur(kw),
                  pl.BlockSpec((1, BLOCK, kw), lambda bi, i: (bi, next_i(i), 0)),
                  pl.BlockSpec((1, vrows, BLOCK), lambda bi, i: (bi, 0, prev_i(i))),
                  pl.BlockSpec((1, vrows, qb), lambda bi, i: (bi, 0, i)),
                  pl.BlockSpec((1, vrows, BLOCK), lambda bi, i: (bi, 0, next_i(i))),
                  cur(A_WIDTH)],
        out_specs=cur(A_WIDTH),
        out_shape=jax.ShapeDtypeStruct((b, s, A_WIDTH), BF16),
        scratch_shapes=[s_buf, s_buf, p_buf, p_buf],
        compiler_params=_params("parallel", "parallel"),
    )(sink.astype(F32), q, k_exp, k_exp, k_exp, vt, vt, vt, z)


def _post_kernel(og_ref, h_ref, p_ref, wo_ref, pg_ref, wg_ref, pw_ref, fg_ref, o_ref, *, final):
    h1 = h_ref[0] + _dot(og_ref[0], wo_ref[...])
    gate = jax.nn.sigmoid(_dot(_rms(h1, pg_ref[...]).astype(BF16), wg_ref[...]))
    h2 = h1 + _dot(p_ref[0].astype(BF16), pw_ref[...]) * gate
    if final:
        h2 = _rms(h2, fg_ref[...])
    o_ref[0] = h2


def _post(og, h, p, w_out, ple_g, w_gate, ple_w, final_g, final):
    b, s, d = h.shape
    tm = min(ROW_TILE, s)
    pd = p.shape[-1]
    row = lambda w: pl.BlockSpec((1, tm, w), lambda bi, i: (bi, i, 0))
    return pl.pallas_call(
        functools.partial(_post_kernel, final=final),
        grid=(b, s // tm),
        in_specs=[row(og.shape[-1]), row(d), row(pd), _const_spec(w_out.shape), _const_spec((1, d)),
                  _const_spec((d, d)), _const_spec((pd, d)), _const_spec((1, d))],
        out_specs=row(d),
        out_shape=jax.ShapeDtypeStruct((b, s, d), F32),
        compiler_params=_params("parallel", "parallel"),
    )(og, h, p, w_out.astype(BF16), ple_g.reshape(1, d), w_gate.astype(BF16), ple_w.astype(BF16),
      final_g.reshape(1, d))


def _pre_b_kernel(x_ref, g_ref, w_ref, qn_ref, wqt_ref, kn_ref, wk_ref, wvt_ref, vb_ref, r_ref, rt_ref,
                  qt_ref, k_ref, vt_ref, z_ref):
    u = _rms(x_ref[0], g_ref[...]).astype(BF16)
    o1, o2, o3 = Q_LORA, Q_LORA + KV_LORA, Q_LORA + KV_LORA + B_QK_PAD
    rot = r_ref[...]
    cq = _rms(_dot(u, w_ref[:, :o1]), qn_ref[...]).astype(BF16)
    qs = ((B_NOPE + B_ROPE) ** -0.5) * LOG2E
    q_t = (_dot_nt(wqt_ref[...], cq) * jnp.tile(rt_ref[...] * qs, (B_HEADS, 1))).astype(BF16)
    for hh in range(B_HEADS):
        for c in range(q_t.shape[1] // MLA_Q_TILE):
            qt_ref[0, hh, c] = q_t[hh * B_QK_PAD:(hh + 1) * B_QK_PAD, c * MLA_Q_TILE:(c + 1) * MLA_Q_TILE]
    ckv = _rms(_dot(u, w_ref[:, o1:o2]), kn_ref[...]).astype(BF16)
    y = _dot(u, w_ref[:, o2:o3]) * rot
    lane = lax.broadcasted_iota(jnp.int32, y.shape, 1)
    both = pltpu.roll(y, 32, 1) + pltpu.roll(y, 96, 1)
    kr = y + jnp.where(lane >= B_NOPE, both, 0.0)
    k = (_dot(ckv, wk_ref[...]) + jnp.tile(kr, (1, B_HEADS))).astype(BF16)
    for hh in range(B_HEADS):
        k_ref[0, hh] = k[:, hh * B_QK_PAD:(hh + 1) * B_QK_PAD]
    vt_ref[0, 0] = (_dot_nt(wvt_ref[...], ckv) + vb_ref[...]).astype(BF16)
    z_ref[0] = _dot(u, w_ref[:, o3:]).astype(BF16)


def _pre_b(h, g, w_in_p, q_norm, wq_p, kv_norm, wk_p, wvt_p, vbias, rot):
    b, s, d = h.shape
    tm = min(MLA_KV_TILE, s)
    assert tm % MLA_Q_TILE == 0
    row = lambda w: pl.BlockSpec((1, tm, w), lambda bi, i: (bi, i, 0))
    vrows = B_HEADS * B_V_ROWS
    return pl.pallas_call(
        _pre_b_kernel,
        grid=(b, s // tm),
        in_specs=[row(d), _const_spec((1, d)), _const_spec(w_in_p.shape), _const_spec((1, Q_LORA)),
                  _const_spec((wq_p.shape[1], wq_p.shape[0])), _const_spec((1, KV_LORA)), _const_spec(wk_p.shape),
                  _const_spec(wvt_p.shape), _const_spec((vrows, 1)),
                  pl.BlockSpec((tm, B_QK_PAD), lambda bi, i: (i, 0)),
                  pl.BlockSpec((B_QK_PAD, tm), lambda bi, i: (0, i))],
        out_specs=[pl.BlockSpec((1, B_HEADS, tm // MLA_Q_TILE, B_QK_PAD, MLA_Q_TILE), lambda bi, i: (bi, 0, i, 0, 0)),
                   pl.BlockSpec((1, B_HEADS, tm, B_QK_PAD), lambda bi, i: (bi, 0, i, 0)),
                   pl.BlockSpec((1, 1, vrows, tm), lambda bi, i: (bi, i, 0, 0)), row(B_WIDTH)],
        out_shape=[
            jax.ShapeDtypeStruct((b, B_HEADS, s // MLA_Q_TILE, B_QK_PAD, MLA_Q_TILE), BF16),
            jax.ShapeDtypeStruct((b, B_HEADS, s, B_QK_PAD), BF16),
            jax.ShapeDtypeStruct((b, s // tm, vrows, tm), BF16),
            jax.ShapeDtypeStruct((b, s, B_WIDTH), BF16),
        ],
        compiler_params=_params("parallel", "parallel"),
    )(h, g.reshape(1, d), w_in_p, q_norm.reshape(1, Q_LORA), wq_p.T, kv_norm.reshape(1, KV_LORA), wk_p,
      wvt_p, vbias, rot, rot.T)


def _prep_b_weights(w_in, w_qb, w_kvb):
    half = B_ROPE // 2
    d = w_in.shape[0]
    o1, o2, o3 = Q_LORA, Q_LORA + KV_LORA, Q_LORA + KV_LORA + B_ROPE
    kr = w_in[:, o2:o3]
    kr_swapped = jnp.concatenate([kr[:, half:], kr[:, :half]], axis=1)
    kr_block = jnp.concatenate([jnp.zeros((d, B_NOPE), w_in.dtype), kr, kr_swapped], axis=1)
    w_in_p = jnp.concatenate([w_in[:, :o2], kr_block, w_in[:, o3:]], axis=1).astype(BF16)

    wq = w_qb.reshape(Q_LORA, B_HEADS, B_NOPE + B_ROPE)
    q_rope = wq[:, :, B_NOPE:]
    q_rope_swapped = jnp.concatenate([q_rope[:, :, half:], q_rope[:, :, :half]], axis=2)
    wq_p = jnp.concatenate([wq[:, :, :B_NOPE], q_rope, q_rope_swapped], axis=2)
    wq_p = wq_p.reshape(Q_LORA, B_HEADS * B_QK_PAD).astype(BF16)

    wkv = w_kvb.reshape(KV_LORA, B_HEADS, B_NOPE + B_VDIM)
    wk_p = jnp.concatenate([wkv[:, :, :B_NOPE], jnp.zeros((KV_LORA, B_HEADS, B_QK_PAD - B_NOPE), w_kvb.dtype)],
                           axis=2).reshape(KV_LORA, B_HEADS * B_QK_PAD).astype(BF16)
    wv = jnp.concatenate([wkv[:, :, B_NOPE:], jnp.zeros((KV_LORA, B_HEADS, B_V_ROWS - B_VDIM), w_kvb.dtype)],
                         axis=2).reshape(KV_LORA, B_HEADS * B_V_ROWS)
    wvt_p = wv.T.astype(BF16)
    vbias = jnp.tile(jnp.concatenate([jnp.zeros((B_VDIM,), F32), jnp.ones((B_V_ROWS - B_VDIM,), F32)]),
                     B_HEADS).reshape(B_HEADS * B_V_ROWS, 1)
    return w_in_p, wq_p, wk_p, wvt_p, vbias


def _rope_table(s):
    half = B_ROPE // 2
    inv_freq = ROPE_THETA ** (-jnp.arange(half, dtype=F32) / half)
    ang = jnp.arange(s, dtype=F32)[:, None] * inv_freq[None, :]
    cos, sin = jnp.cos(ang), jnp.sin(ang)
    return jnp.concatenate([jnp.ones((s, B_NOPE), F32), cos, cos, -sin, sin], axis=1)


def _mla_kernel(q_ref, k_ref, vt_ref, z_ref, o_ref, s0_ref, s1_ref, p0_ref, p1_ref, *, q_tile, q_sub, kv_tile):
    n_kv = vt_ref.shape[1]
    n_q = q_ref.shape[2] // q_sub
    s_bufs, p_bufs = (s0_ref, s1_ref), (p0_ref, p1_ref)
    chains = [(hh, sub) for sub in range(q_sub) for hh in range(B_HEAD_PAIR)]
    m_init = (jnp.full((1, q_tile), NEG_BIG, F32),) * len(chains)

    def q_rows(qt, sub):
        return pl.ds(pl.multiple_of((qt * q_sub + sub) * q_tile, q_tile), q_tile)

    def load_q(qt):
        return [q_ref[0, hh, qt * q_sub + sub] for hh, sub in chains]

    def score_chain(qc, j, slot, c):
        kc = k_ref[0, chains[c][0], j * kv_tile:(j + 1) * kv_tile, :]
        s_bufs[slot][c] = _dot(kc, qc[c])

    def scores(qc, j, slot):
        for c in range(len(chains)):
            score_chain(qc, j, slot, c)

    def exp_chain(slot, m, c):
        m_new = jnp.maximum(m[c], jnp.max(s_bufs[slot][c], axis=0, keepdims=True))
        p_bufs[slot][c] = jnp.exp2(s_bufs[slot][c] - m_new).astype(BF16)
        return m_new, jnp.exp2(m[c] - m_new)

    def exps(slot, m):
        res = [exp_chain(slot, m, c) for c in range(len(chains))]
        return tuple(r[0] for r in res), tuple(r[1] for r in res)

    def value_chain(j, slot, acc, alpha, c):
        vt = vt_ref[0, j, chains[c][0] * B_V_ROWS:(chains[c][0] + 1) * B_V_ROWS, :]
        return acc[c] * alpha[c] + _dot(vt, p_bufs[slot][c])

    def step(qc, j_s, j_v, acc, alpha, m):
        s_slot, e_slot, v_slot = j_s % 2, (j_s - 1) % 2, j_v % 2
        acc_out, m_out, alpha_out = [], [], []
        todo = ([(op, c) for op in STEP_ORDER.lower() for c in range(len(chains))] if STEP_ORDER.isupper()
                else [(op, c) for c in range(len(chains)) for op in STEP_ORDER])
        for op, c in todo:
            if True:
                if op == "s":
                    score_chain(qc, j_s % n_kv, s_slot, c)
                elif op == "v":
                    acc_out.append(value_chain(j_v, v_slot, acc, alpha, c))
                else:
                    m_new, a_new = exp_chain(e_slot, m, c)
                    m_out.append(m_new)
                    alpha_out.append(a_new)
        return tuple(acc_out), tuple(m_out), tuple(alpha_out)

    def tile_body(qt, carry):
        m, alpha = carry
        qc = load_q(qt)
        qc_next = load_q(jnp.minimum(qt + 1, n_q - 1))
        acc = (jnp.zeros((B_V_ROWS, q_tile), F32),) * len(chains)
        for t in range(2, n_kv):
            acc, m, alpha = step(qc, t, t - 2, acc, alpha, m)
        acc, m, alpha = step(qc_next, n_kv, n_kv - 2, acc, alpha, m)
        acc, m_next, alpha_next = step(qc_next, n_kv + 1, n_kv - 1, acc, alpha, m_init)
        for sub in range(q_sub):
            o_t = jnp.concatenate([acc[c][:B_VDIM] / acc[c][B_VDIM:B_VDIM + 1]
                                   for c, (_, s_) in enumerate(chains) if s_ == sub], axis=0)
            z = z_ref[0, q_rows(qt, sub), :].astype(F32)
            o_ref[0, q_rows(qt, sub), :] = (o_t.T * (z * jax.nn.sigmoid(z))).astype(BF16)
        return m_next, alpha_next

    qc0 = load_q(0)
    scores(qc0, 0, 0)
    m, alpha = exps(0, m_init)
    scores(qc0, 1, 1)
    lax.fori_loop(0, n_q, tile_body, (m, alpha))


def _mla_attn(q_t, k, vt, z):
    b, _, s, _ = k.shape
    n_kv, kv_tile = vt.shape[1], vt.shape[3]
    assert n_kv >= 2 and n_kv % 2 == 0
    q_block = min(MLA_Q_BLOCK, s)
    q_tile = q_t.shape[-1]
    out_w = B_HEAD_PAIR * B_VDIM
    q_sub = MLA_Q_SUB if q_block % (MLA_Q_SUB * q_tile) == 0 else 1
    s_buf = pltpu.VMEM((B_HEAD_PAIR * q_sub, kv_tile, q_tile), F32)
    p_buf = pltpu.VMEM((B_HEAD_PAIR * q_sub, kv_tile, q_tile), BF16)
    return pl.pallas_call(
        functools.partial(_mla_kernel, q_tile=q_tile, q_sub=q_sub, kv_tile=kv_tile),
        grid=(b, B_HEADS // B_HEAD_PAIR, s // q_block),
        in_specs=[pl.BlockSpec((1, B_HEAD_PAIR, q_block // q_tile, B_QK_PAD, q_tile),
                               lambda bi, hp, qi: (bi, hp, qi, 0, 0)),
                  pl.BlockSpec((1, B_HEAD_PAIR, s, B_QK_PAD), lambda bi, hp, qi: (bi, hp, 0, 0)),
                  pl.BlockSpec((1, n_kv, B_HEAD_PAIR * B_V_ROWS, kv_tile), lambda bi, hp, qi: (bi, 0, hp, 0)),
                  pl.BlockSpec((1, q_block, out_w), lambda bi, hp, qi: (bi, qi, hp))],
        out_specs=pl.BlockSpec((1, q_block, out_w), lambda bi, hp, qi: (bi, qi, hp)),
        out_shape=jax.ShapeDtypeStruct((b, s, B_WIDTH), BF16),
        scratch_shapes=[s_buf, s_buf, p_buf, p_buf],
        compiler_params=_params("parallel", "parallel", "arbitrary"),
    )(q_t, k, vt, z)


def kernel(x, p, norm_g, a_w_in, a_sink, a_w_out, b_w_in, b_q_norm, b_w_qb, b_kv_norm, b_w_kvb, b_w_out,
           ple_w, ple_norm_g, ple_w_gate, final_norm_g):
    depth = p.shape[0]
    s = x.shape[1]
    rot = _rope_table(s)
    h = x
    for i in range(depth):
        j = i // 2
        if i % 2 == 0:
            q, k_exp, vt, z = _pre_a(h, norm_g[i], a_w_in[j])
            og = _win_attn(q, k_exp, vt, z, a_sink[j])
            w_out = a_w_out[j]
        else:
            w_in_p, wq_p, wk_p, wvt_p, vbias = _prep_b_weights(b_w_in[j], b_w_qb[j], b_w_kvb[j])
            q, k, vt, z = _pre_b(h, norm_g[i], w_in_p, b_q_norm[j], wq_p, b_kv_norm[j], wk_p, wvt_p, vbias, rot)
            og = _mla_attn(q, k, vt, z)
            w_out = b_w_out[j]
        h = _post(og, h, p[i], w_out, ple_norm_g[i], ple_w_gate[i], ple_w[i], final_norm_g, final=(i == depth - 1))
    return h
```

```python
import functools

import jax
import jax.numpy as jnp
from jax import lax
from jax.experimental import pallas as pl
from jax.experimental.pallas import tpu as pltpu

F32 = jnp.float32
BF16 = jnp.bfloat16

EPS = 1e-6
NEG_BIG = -1e30
LOG2E = 1.4426950408889634

A_HEADS = 16
A_KV_HEADS = 4
A_GROUP = A_HEADS // A_KV_HEADS
A_HEAD_DIM = 64
A_WIDTH = A_HEADS * A_HEAD_DIM
A_KV_WIDTH = A_KV_HEADS * A_HEAD_DIM
WINDOW = 128
BLOCK = 128
A_V_ROWS = 80
WIN_Q_BLOCK = 256
MASK_DIST = 1e33

B_HEADS = 16
B_NOPE = 64
B_ROPE = 32
B_VDIM = 64
B_WIDTH = B_HEADS * B_VDIM
Q_LORA = 384
KV_LORA = 256
ROPE_THETA = 10000.0
B_QK_PAD = 128
B_V_ROWS = 80
B_HEAD_PAIR = 2

ROW_TILE = 512
MLA_Q_BLOCK = 2048
MLA_Q_TILE = 256
STEP_ORDER = "sve"
MLA_Q_SUB = 1
MLA_KV_TILE = 512
MLA_KV_CHUNKS = 1

VMEM_LIMIT = 56 * 1024 * 1024


def _rms(x, g):
    ms = jnp.mean(x * x, axis=-1, keepdims=True)
    return x * lax.rsqrt(ms + EPS) * g


def _dot(a, b):
    return jnp.dot(a, b, preferred_element_type=F32)


def _dot_nt(a, b):
    return lax.dot_general(a, b, (((1,), (1,)), ((), ())), preferred_element_type=F32)


def _params(*sem):
    return pltpu.CompilerParams(dimension_semantics=sem, vmem_limit_bytes=VMEM_LIMIT)


def _const_spec(shape):
    nd = len(shape)
    return pl.BlockSpec(shape, lambda *_: (0,) * nd)


def _pre_a_kernel(x_ref, g_ref, w_ref, wqt_ref, wvt_ref, vb_ref, qt_ref, k_ref, vt_ref, z_ref):
    u = _rms(x_ref[0], g_ref[...]).astype(BF16)
    qs = (A_HEAD_DIM ** -0.5) * LOG2E
    o1, o2, o3 = A_WIDTH, A_WIDTH + A_KV_WIDTH, A_WIDTH + 2 * A_KV_WIDTH
    q_t = (_dot_nt(wqt_ref[...], u) * qs).astype(BF16)
    for hp in range(A_WIDTH // 128):
        for c in range(q_t.shape[1] // WIN_Q_BLOCK):
            qt_ref[0, hp, c] = q_t[hp * 128:(hp + 1) * 128, c * WIN_Q_BLOCK:(c + 1) * WIN_Q_BLOCK]
    k = _dot(u, w_ref[:, o1:o2])
    lane = lax.broadcasted_iota(jnp.int32, (k.shape[0], 2 * A_HEAD_DIM), 1)
    low = lane < A_HEAD_DIM
    for kv in range(A_KV_HEADS):
        blk = k[:, (kv // 2) * 128:(kv // 2 + 1) * 128]
        swapped = pltpu.roll(blk, A_HEAD_DIM, 1)
        first, second = (blk, swapped) if kv % 2 == 0 else (swapped, blk)
        k_ref[0, :, kv * 256:kv * 256 + 128] = jnp.where(low, first, 0.0).astype(BF16)
        k_ref[0, :, kv * 256 + 128:(kv + 1) * 256] = jnp.where(low, 0.0, second).astype(BF16)
    vt_ref[0] = (_dot_nt(wvt_ref[...], u) + vb_ref[...]).astype(BF16)
    z_ref[0] = _dot(u, w_ref[:, o3:]).astype(BF16)


def _pre_a(h, g, w_in):
    b, s, d = h.shape
    tm = min(ROW_TILE, s)
    n_in = w_in.shape[1]
    o2, o3 = A_WIDTH + A_KV_WIDTH, A_WIDTH + 2 * A_KV_WIDTH
    pad = A_V_ROWS - A_HEAD_DIM
    wv = w_in[:, o2:o3].reshape(d, A_KV_HEADS, A_HEAD_DIM)
    wvt = jnp.concatenate([wv, jnp.zeros((d, A_KV_HEADS, pad), w_in.dtype)], axis=2)
    wvt = wvt.reshape(d, A_KV_HEADS * A_V_ROWS).T.astype(BF16)
    vbias = jnp.tile(jnp.concatenate([jnp.zeros((A_HEAD_DIM,), F32), jnp.ones((pad,), F32)]),
                     A_KV_HEADS).reshape(A_KV_HEADS * A_V_ROWS, 1)
    vrows = A_KV_HEADS * A_V_ROWS
    n_pairs = A_WIDTH // 128
    assert tm % WIN_Q_BLOCK == 0
    row = lambda w: pl.BlockSpec((1, tm, w), lambda bi, i: (bi, i, 0))
    return pl.pallas_call(
        _pre_a_kernel,
        grid=(b, s // tm),
        in_specs=[row(d), _const_spec((1, d)), _const_spec((d, n_in)), _const_spec((A_WIDTH, d)),
                  _const_spec((vrows, d)), _const_spec((vrows, 1))],
        out_specs=[pl.BlockSpec((1, n_pairs, tm // WIN_Q_BLOCK, 128, WIN_Q_BLOCK), lambda bi, i: (bi, 0, i, 0, 0)),
                   row(A_KV_HEADS * 256), pl.BlockSpec((1, vrows, tm), lambda bi, i: (bi, 0, i)),
                   row(A_WIDTH)],
        out_shape=[
            jax.ShapeDtypeStruct((b, n_pairs, s // WIN_Q_BLOCK, 128, WIN_Q_BLOCK), BF16),
            jax.ShapeDtypeStruct((b, s, A_KV_HEADS * 256), BF16),
            jax.ShapeDtypeStruct((b, vrows, s), BF16),
            jax.ShapeDtypeStruct((b, s, A_WIDTH), BF16),
        ],
        compiler_params=_params("parallel", "parallel"),
    )(h, g.reshape(1, d), w_in.astype(BF16), w_in[:, :A_WIDTH].T.astype(BF16), wvt, vbias)


def _win_kernel(sink_ref, q_ref, kp_ref, kc_ref, kn_ref, vp_ref, vc_ref, vn_ref, z_ref, o_ref,
                s0_ref, s1_ref, p0_ref, p1_ref, *, seq):
    i = pl.program_id(1)
    s_bufs, p_bufs = (s0_ref, s1_ref), (p0_ref, p1_ref)
    n_keys = WIN_Q_BLOCK + 2 * BLOCK
    r = lax.broadcasted_iota(jnp.int32, (n_keys, WIN_Q_BLOCK), 0)
    c = lax.broadcasted_iota(jnp.int32, (n_keys, WIN_Q_BLOCK), 1)
    rel = r - BLOCK - c
    k_pos = i * WIN_Q_BLOCK - BLOCK + r
    valid = (jnp.abs(rel) <= WINDOW) & (k_pos >= 0) & (k_pos < seq)
    neg_dist = jnp.where(valid, -jnp.abs(rel).astype(F32), -MASK_DIST)

    def scores(h, slot):
        kv, hp = h // A_GROUP, h // 2
        cols = slice(kv * 256 + (h % 2) * 128, kv * 256 + (h % 2 + 1) * 128)
        k_sel = jnp.concatenate([kp_ref[0, :, cols], kc_ref[0, :, cols], kn_ref[0, :, cols]], axis=0)
        slope = 2.0 ** (-8.0 * (h + 1) / A_HEADS) * LOG2E
        s = _dot(k_sel, q_ref[0, hp, 0]) + slope * neg_dist
        s_bufs[slot][...] = s
        return jnp.max(s, axis=0, keepdims=True)

    def exps(h, slot, tmax):
        m = jnp.maximum(tmax, sink_ref[h] * LOG2E)
        p_bufs[slot][...] = jnp.exp2(s_bufs[slot][...] - m).astype(BF16)
        return m

    def values(h, slot, m):
        kv = h // A_GROUP
        rows = slice(kv * A_V_ROWS, (kv + 1) * A_V_ROWS)
        vt = jnp.concatenate([vp_ref[0, rows, :], vc_ref[0, rows, :], vn_ref[0, rows, :]], axis=1)
        acc = _dot(vt, p_bufs[slot][...])
        denom = acc[A_HEAD_DIM:A_HEAD_DIM + 1] + jnp.exp2(sink_ref[h] * LOG2E - m)
        return acc[:A_HEAD_DIM] / denom

    tmax, m, pending = {}, {}, {}
    for t in range(A_HEADS + 2):
        if t < A_HEADS:
            tmax[t] = scores(t, t % 2)
        if t >= 2:
            h = t - 2
            pending[h] = values(h, h % 2, m.pop(h))
            if h % 2 == 1:
                o_t = jnp.concatenate([pending.pop(h - 1), pending.pop(h)], axis=0)
                lanes = slice((h // 2) * 128, (h // 2 + 1) * 128)
                z = z_ref[0, :, lanes].astype(F32)
                o_ref[0, :, lanes] = (o_t.T * (z * jax.nn.sigmoid(z))).astype(BF16)
        if 1 <= t <= A_HEADS:
            m[t - 1] = exps(t - 1, (t - 1) % 2, tmax.pop(t - 1))


def _win_attn(q_t, k_exp, vt, z, sink):
    b, s, _ = k_exp.shape
    qb = WIN_Q_BLOCK
    assert s % qb == 0
    per = qb // BLOCK
    nb = s // BLOCK
    vrows = vt.shape[1]
    kw = k_exp.shape[-1]
    prev_i = lambda i: jnp.maximum(per * i - 1, 0)
    next_i = lambda i: jnp.minimum(per * i + per, nb - 1)
    cur = lambda w: pl.BlockSpec((1, qb, w), lambda bi, i: (bi, i, 0))
    s_buf = pltpu.VMEM((qb + 2 * BLOCK, qb), F32)
    p_buf = pltpu.VMEM((qb + 2 * BLOCK, qb), BF16)
    return pl.pallas_call(
        functools.partial(_win_kernel, seq=s),
        grid=(b, s // qb),
        in_specs=[pl.BlockSpec(memory_space=pltpu.SMEM),
                  pl.BlockSpec((1, q_t.shape[1], 1, 128, qb), lambda bi, i: (bi, 0, i, 0, 0)),
                  pl.BlockSpec((1, BLOCK, kw), lambda bi, i: (bi, prev_i(i), 0)),
                  cur(kw),
                  pl.BlockSpec((1, BLOCK, kw), lambda bi, i: (bi, next_i(i), 0)),
                  pl.BlockSpec((1, vrows, BLOCK), lambda bi, i: (bi, 0, prev_i(i))),
                  pl.BlockSpec((1, vrows, qb), lambda bi, i: (bi, 0, i)),
                  pl.BlockSpec((1, vrows, BLOCK), lambda bi, i: (bi, 0, next_i(i))),
                  cur(A_WIDTH)],
        out_specs=cur(A_WIDTH),
        out_shape=jax.ShapeDtypeStruct((b, s, A_WIDTH), BF16),
        scratch_shapes=[s_buf, s_buf, p_buf, p_buf],
        compiler_params=_params("parallel", "parallel"),
    )(sink.astype(F32), q_t, k_exp, k_exp, k_exp, vt, vt, vt, z)


def _post_kernel(og_ref, h_ref, p_ref, wo_ref, pg_ref, wg_ref, pw_ref, fg_ref, o_ref, *, final):
    h1 = h_ref[0] + _dot(og_ref[0], wo_ref[...])
    gate = jax.nn.sigmoid(_dot(_rms(h1, pg_ref[...]).astype(BF16), wg_ref[...]))
    h2 = h1 + _dot(p_ref[0].astype(BF16), pw_ref[...]) * gate
    if final:
        h2 = _rms(h2, fg_ref[...])
    o_ref[0] = h2


def _post(og, h, p, w_out, ple_g, w_gate, ple_w, final_g, final):
    b, s, d = h.shape
    tm = min(ROW_TILE, s)
    pd = p.shape[-1]
    row = lambda w: pl.BlockSpec((1, tm, w), lambda bi, i: (bi, i, 0))
    return pl.pallas_call(
        functools.partial(_post_kernel, final=final),
        grid=(b, s // tm),
        in_specs=[row(og.shape[-1]), row(d), row(pd), _const_spec(w_out.shape), _const_spec((1, d)),
                  _const_spec((d, d)), _const_spec((pd, d)), _const_spec((1, d))],
        out_specs=row(d),
        out_shape=jax.ShapeDtypeStruct((b, s, d), F32),
        compiler_params=_params("parallel", "parallel"),
    )(og, h, p, w_out.astype(BF16), ple_g.reshape(1, d), w_gate.astype(BF16), ple_w.astype(BF16),
      final_g.reshape(1, d))


def _pre_b_kernel(x_ref, g_ref, w_ref, qn_ref, wqt_ref, kn_ref, wk_ref, wvt_ref, vb_ref, r_ref, rt_ref,
                  qt_ref, k_ref, vt_ref, z_ref):
    u = _rms(x_ref[0], g_ref[...]).astype(BF16)
    o1, o2, o3 = Q_LORA, Q_LORA + KV_LORA, Q_LORA + KV_LORA + B_QK_PAD
    rot = r_ref[...]
    cq = _rms(_dot(u, w_ref[:, :o1]), qn_ref[...]).astype(BF16)
    qs = ((B_NOPE + B_ROPE) ** -0.5) * LOG2E
    q_t = (_dot_nt(wqt_ref[...], cq) * jnp.tile(rt_ref[...] * qs, (B_HEADS, 1))).astype(BF16)
    for hh in range(B_HEADS):
        for c in range(q_t.shape[1] // MLA_Q_TILE):
            qt_ref[0, hh, c] = q_t[hh * B_QK_PAD:(hh + 1) * B_QK_PAD, c * MLA_Q_TILE:(c + 1) * MLA_Q_TILE]
    ckv = _rms(_dot(u, w_ref[:, o1:o2]), kn_ref[...]).astype(BF16)
    y = _dot(u, w_ref[:, o2:o3]) * rot
    lane = lax.broadcasted_iota(jnp.int32, y.shape, 1)
    both = pltpu.roll(y, 32, 1) + pltpu.roll(y, 96, 1)
    kr = y + jnp.where(lane >= B_NOPE, both, 0.0)
    k = (_dot(ckv, wk_ref[...]) + jnp.tile(kr, (1, B_HEADS))).astype(BF16)
    for hh in range(B_HEADS):
        k_ref[0, hh] = k[:, hh * B_QK_PAD:(hh + 1) * B_QK_PAD]
    vt_ref[0, 0] = (_dot_nt(wvt_ref[...], ckv) + vb_ref[...]).astype(BF16)
    z_ref[0] = _dot(u, w_ref[:, o3:]).astype(BF16)


def _pre_b(h, g, w_in_p, q_norm, wq_p, kv_norm, wk_p, wvt_p, vbias, rot):
    b, s, d = h.shape
    tm = min(MLA_KV_TILE, s)
    assert tm % MLA_Q_TILE == 0
    row = lambda w: pl.BlockSpec((1, tm, w), lambda bi, i: (bi, i, 0))
    vrows = B_HEADS * B_V_ROWS
    return pl.pallas_call(
        _pre_b_kernel,
        grid=(b, s // tm),
        in_specs=[row(d), _const_spec((1, d)), _const_spec(w_in_p.shape), _const_spec((1, Q_LORA)),
                  _const_spec((wq_p.shape[1], wq_p.shape[0])), _const_spec((1, KV_LORA)), _const_spec(wk_p.shape),
                  _const_spec(wvt_p.shape), _const_spec((vrows, 1)),
                  pl.BlockSpec((tm, B_QK_PAD), lambda bi, i: (i, 0)),
                  pl.BlockSpec((B_QK_PAD, tm), lambda bi, i: (0, i))],
        out_specs=[pl.BlockSpec((1, B_HEADS, tm // MLA_Q_TILE, B_QK_PAD, MLA_Q_TILE), lambda bi, i: (bi, 0, i, 0, 0)),
                   pl.BlockSpec((1, B_HEADS, tm, B_QK_PAD), lambda bi, i: (bi, 0, i, 0)),
                   pl.BlockSpec((1, 1, vrows, tm), lambda bi, i: (bi, i, 0, 0)), row(B_WIDTH)],
        out_shape=[
            jax.ShapeDtypeStruct((b, B_HEADS, s // MLA_Q_TILE, B_QK_PAD, MLA_Q_TILE), BF16),
            jax.ShapeDtypeStruct((b, B_HEADS, s, B_QK_PAD), BF16),
            jax.ShapeDtypeStruct((b, s // tm, vrows, tm), BF16),
            jax.ShapeDtypeStruct((b, s, B_WIDTH), BF16),
        ],
        compiler_params=_params("parallel", "parallel"),
    )(h, g.reshape(1, d), w_in_p, q_norm.reshape(1, Q_LORA), wq_p.T, kv_norm.reshape(1, KV_LORA), wk_p,
      wvt_p, vbias, rot, rot.T)


def _prep_b_weights(w_in, w_qb, w_kvb):
    half = B_ROPE // 2
    d = w_in.shape[0]
    o1, o2, o3 = Q_LORA, Q_LORA + KV_LORA, Q_LORA + KV_LORA + B_ROPE
    kr = w_in[:, o2:o3]
    kr_swapped = jnp.concatenate([kr[:, half:], kr[:, :half]], axis=1)
    kr_block = jnp.concatenate([jnp.zeros((d, B_NOPE), w_in.dtype), kr, kr_swapped], axis=1)
    w_in_p = jnp.concatenate([w_in[:, :o2], kr_block, w_in[:, o3:]], axis=1).astype(BF16)

    wq = w_qb.reshape(Q_LORA, B_HEADS, B_NOPE + B_ROPE)
    q_rope = wq[:, :, B_NOPE:]
    q_rope_swapped = jnp.concatenate([q_rope[:, :, half:], q_rope[:, :, :half]], axis=2)
    wq_p = jnp.concatenate([wq[:, :, :B_NOPE], q_rope, q_rope_swapped], axis=2)
    wq_p = wq_p.reshape(Q_LORA, B_HEADS * B_QK_PAD).astype(BF16)

    wkv = w_kvb.reshape(KV_LORA, B_HEADS, B_NOPE + B_VDIM)
    wk_p = jnp.concatenate([wkv[:, :, :B_NOPE], jnp.zeros((KV_LORA, B_HEADS, B_QK_PAD - B_NOPE), w_kvb.dtype)],
                           axis=2).reshape(KV_LORA, B_HEADS * B_QK_PAD).astype(BF16)
    wv = jnp.concatenate([wkv[:, :, B_NOPE:], jnp.zeros((KV_LORA, B_HEADS, B_V_ROWS - B_VDIM), w_kvb.dtype)],
                         axis=2).reshape(KV_LORA, B_HEADS * B_V_ROWS)
    wvt_p = wv.T.astype(BF16)
    vbias = jnp.tile(jnp.concatenate([jnp.zeros((B_VDIM,), F32), jnp.ones((B_V_ROWS - B_VDIM,), F32)]),
                     B_HEADS).reshape(B_HEADS * B_V_ROWS, 1)
    return w_in_p, wq_p, wk_p, wvt_p, vbias


def _rope_table(s):
    half = B_ROPE // 2
    inv_freq = ROPE_THETA ** (-jnp.arange(half, dtype=F32) / half)
    ang = jnp.arange(s, dtype=F32)[:, None] * inv_freq[None, :]
    cos, sin = jnp.cos(ang), jnp.sin(ang)
    return jnp.concatenate([jnp.ones((s, B_NOPE), F32), cos, cos, -sin, sin], axis=1)


def _mla_kernel(q_ref, k_ref, vt_ref, z_ref, o_ref, s0_ref, s1_ref, p0_ref, p1_ref, *, q_tile, q_sub, kv_tile):
    kv_chunks = kv_tile // vt_ref.shape[3]
    n_kv = vt_ref.shape[1] // kv_chunks
    n_q = q_ref.shape[2] // q_sub
    s_bufs, p_bufs = (s0_ref, s1_ref), (p0_ref, p1_ref)
    chains = [(hh, sub) for sub in range(q_sub) for hh in range(B_HEAD_PAIR)]
    m_init = (jnp.full((1, q_tile), NEG_BIG, F32),) * len(chains)

    def q_rows(qt, sub):
        return pl.ds(pl.multiple_of((qt * q_sub + sub) * q_tile, q_tile), q_tile)

    def load_q(qt):
        return [q_ref[0, hh, qt * q_sub + sub] for hh, sub in chains]

    def score_chain(qc, j, slot, c):
        kc = k_ref[0, chains[c][0], j * kv_tile:(j + 1) * kv_tile, :]
        s_bufs[slot][c] = _dot(kc, qc[c])

    def scores(qc, j, slot):
        for c in range(len(chains)):
            score_chain(qc, j, slot, c)

    def exp_chain(slot, m, c):
        m_new = jnp.maximum(m[c], jnp.max(s_bufs[slot][c], axis=0, keepdims=True))
        p_bufs[slot][c] = jnp.exp2(s_bufs[slot][c] - m_new).astype(BF16)
        return m_new, jnp.exp2(m[c] - m_new)

    def exps(slot, m):
        res = [exp_chain(slot, m, c) for c in range(len(chains))]
        return tuple(r[0] for r in res), tuple(r[1] for r in res)

    def value_chain(j, slot, acc, alpha, c):
        rows = slice(chains[c][0] * B_V_ROWS, (chains[c][0] + 1) * B_V_ROWS)
        vt = jnp.concatenate([vt_ref[0, j * kv_chunks + i, rows, :] for i in range(kv_chunks)], axis=1)
        return acc[c] * alpha[c] + _dot(vt, p_bufs[slot][c])

    def step(qc, j_s, j_v, acc, alpha, m):
        s_slot, e_slot, v_slot = j_s % 2, (j_s - 1) % 2, j_v % 2
        acc_out, m_out, alpha_out = [], [], []
        todo = ([(op, c) for op in STEP_ORDER.lower() for c in range(len(chains))] if STEP_ORDER.isupper()
                else [(op, c) for c in range(len(chains)) for op in STEP_ORDER])
        for op, c in todo:
            if True:
                if op == "s":
                    score_chain(qc, j_s % n_kv, s_slot, c)
                elif op == "v":
                    acc_out.append(value_chain(j_v, v_slot, acc, alpha, c))
                else:
                    m_new, a_new = exp_chain(e_slot, m, c)
                    m_out.append(m_new)
                    alpha_out.append(a_new)
        return tuple(acc_out), tuple(m_out), tuple(alpha_out)

    def tile_body(qt, carry):
        m, alpha = carry
        qc = load_q(qt)
        qc_next = load_q(jnp.minimum(qt + 1, n_q - 1))
        acc = (jnp.zeros((B_V_ROWS, q_tile), F32),) * len(chains)
        for t in range(2, n_kv):
            acc, m, alpha = step(qc, t, t - 2, acc, alpha, m)
        acc, m, alpha = step(qc_next, n_kv, n_kv - 2, acc, alpha, m)
        acc, m_next, alpha_next = step(qc_next, n_kv + 1, n_kv - 1, acc, alpha, m_init)
        for sub in range(q_sub):
            o_t = jnp.concatenate([acc[c][:B_VDIM] / acc[c][B_VDIM:B_VDIM + 1]
                                   for c, (_, s_) in enumerate(chains) if s_ == sub], axis=0)
            z = z_ref[0, q_rows(qt, sub), :].astype(F32)
            o_ref[0, q_rows(qt, sub), :] = (o_t.T * (z * jax.nn.sigmoid(z))).astype(BF16)
        return m_next, alpha_next

    qc0 = load_q(0)
    scores(qc0, 0, 0)
    m, alpha = exps(0, m_init)
    scores(qc0, 1, 1)
    lax.fori_loop(0, n_q, tile_body, (m, alpha))


def _mla_attn(q_t, k, vt, z):
    b, _, s, _ = k.shape
    n_chunks, kv_chunk = vt.shape[1], vt.shape[3]
    kv_tile = kv_chunk * (MLA_KV_CHUNKS if n_chunks % (2 * MLA_KV_CHUNKS) == 0 else 1)
    n_kv = s // kv_tile
    assert n_kv >= 2 and n_kv % 2 == 0
    q_block = min(MLA_Q_BLOCK, s)
    q_tile = q_t.shape[-1]
    out_w = B_HEAD_PAIR * B_VDIM
    q_sub = MLA_Q_SUB if q_block % (MLA_Q_SUB * q_tile) == 0 else 1
    s_buf = pltpu.VMEM((B_HEAD_PAIR * q_sub, kv_tile, q_tile), F32)
    p_buf = pltpu.VMEM((B_HEAD_PAIR * q_sub, kv_tile, q_tile), BF16)
    return pl.pallas_call(
        functools.partial(_mla_kernel, q_tile=q_tile, q_sub=q_sub, kv_tile=kv_tile),
        grid=(b, B_HEADS // B_HEAD_PAIR, s // q_block),
        in_specs=[pl.BlockSpec((1, B_HEAD_PAIR, q_block // q_tile, B_QK_PAD, q_tile),
                               lambda bi, hp, qi: (bi, hp, qi, 0, 0)),
                  pl.BlockSpec((1, B_HEAD_PAIR, s, B_QK_PAD), lambda bi, hp, qi: (bi, hp, 0, 0)),
                  pl.BlockSpec((1, n_chunks, B_HEAD_PAIR * B_V_ROWS, kv_chunk), lambda bi, hp, qi: (bi, 0, hp, 0)),
                  pl.BlockSpec((1, q_block, out_w), lambda bi, hp, qi: (bi, qi, hp))],
        out_specs=pl.BlockSpec((1, q_block, out_w), lambda bi, hp, qi: (bi, qi, hp)),
        out_shape=jax.ShapeDtypeStruct((b, s, B_WIDTH), BF16),
        scratch_shapes=[s_buf, s_buf, p_buf, p_buf],
        compiler_params=_params("parallel", "parallel", "arbitrary"),
    )(q_t, k, vt, z)


def kernel(x, p, norm_g, a_w_in, a_sink, a_w_out, b_w_in, b_q_norm, b_w_qb, b_kv_norm, b_w_kvb, b_w_out,
           ple_w, ple_norm_g, ple_w_gate, final_norm_g):
    depth = p.shape[0]
    s = x.shape[1]
    rot = _rope_table(s)
    h = x
    for i in range(depth):
        j = i // 2
        if i % 2 == 0:
            q, k_exp, vt, z = _pre_a(h, norm_g[i], a_w_in[j])
            og = _win_attn(q, k_exp, vt, z, a_sink[j])
            w_out = a_w_out[j]
        else:
            w_in_p, wq_p, wk_p, wvt_p, vbias = _prep_b_weights(b_w_in[j], b_w_qb[j], b_w_kvb[j])
            q, k, vt, z = _pre_b(h, norm_g[i], w_in_p, b_q_norm[j], wq_p, b_kv_norm[j], wk_p, wvt_p, vbias, rot)
            og = _mla_attn(q, k, vt, z)
            w_out = b_w_out[j]
        h = _post(og, h, p[i], w_out, ple_norm_g[i], ple_w_gate[i], ple_w[i], final_norm_g, final=(i == depth - 1))
    return h
```

```python
import functools

import jax
import jax.numpy as jnp
from jax import lax
from jax.experimental import pallas as pl
from jax.experimental.pallas import tpu as pltpu

F32 = jnp.float32
BF16 = jnp.bfloat16

EPS = 1e-6
NEG_BIG = -1e30
LOG2E = 1.4426950408889634

A_HEADS = 16
A_KV_HEADS = 4
A_GROUP = A_HEADS // A_KV_HEADS
A_HEAD_DIM = 64
A_WIDTH = A_HEADS * A_HEAD_DIM
A_KV_WIDTH = A_KV_HEADS * A_HEAD_DIM
WINDOW = 128
BLOCK = 128
A_V_ROWS = 80
WIN_Q_BLOCK = 256
MASK_DIST = 1e33

B_HEADS = 16
B_NOPE = 64
B_ROPE = 32
B_VDIM = 64
B_WIDTH = B_HEADS * B_VDIM
Q_LORA = 384
KV_LORA = 256
ROPE_THETA = 10000.0
B_QK_PAD = 128
B_V_ROWS = 80
B_HEAD_PAIR = 2

ROW_TILE = 512
MLA_Q_BLOCK = 2048
MLA_Q_TILE = 256
MLA_Q_SUB = 1
MLA_KV_TILE = 512
MLA_KV_CHUNKS = 1

VMEM_LIMIT = 56 * 1024 * 1024


def _rms(x, g):
    ms = jnp.mean(x * x, axis=-1, keepdims=True)
    return x * lax.rsqrt(ms + EPS) * g


def _dot(a, b):
    return jnp.dot(a, b, preferred_element_type=F32)


def _dot_nt(a, b):
    return lax.dot_general(a, b, (((1,), (1,)), ((), ())), preferred_element_type=F32)


def _params(*sem):
    return pltpu.CompilerParams(dimension_semantics=sem, vmem_limit_bytes=VMEM_LIMIT)


def _const_spec(shape):
    nd = len(shape)
    return pl.BlockSpec(shape, lambda *_: (0,) * nd)


def _pre_a_kernel(x_ref, g_ref, w_ref, wqt_ref, wvt_ref, vb_ref, qt_ref, k_ref, vt_ref, z_ref):
    u = _rms(x_ref[0], g_ref[...]).astype(BF16)
    qs = (A_HEAD_DIM ** -0.5) * LOG2E
    o1, o2, o3 = A_WIDTH, A_WIDTH + A_KV_WIDTH, A_WIDTH + 2 * A_KV_WIDTH
    q_t = (_dot_nt(wqt_ref[...], u) * qs).astype(BF16)
    for hp in range(A_WIDTH // 128):
        for c in range(q_t.shape[1] // WIN_Q_BLOCK):
            qt_ref[0, hp, c] = q_t[hp * 128:(hp + 1) * 128, c * WIN_Q_BLOCK:(c + 1) * WIN_Q_BLOCK]
    k = _dot(u, w_ref[:, o1:o2])
    lane = lax.broadcasted_iota(jnp.int32, (k.shape[0], 2 * A_HEAD_DIM), 1)
    low = lane < A_HEAD_DIM
    for kv in range(A_KV_HEADS):
        blk = k[:, (kv // 2) * 128:(kv // 2 + 1) * 128]
        swapped = pltpu.roll(blk, A_HEAD_DIM, 1)
        first, second = (blk, swapped) if kv % 2 == 0 else (swapped, blk)
        k_ref[0, :, kv * 256:kv * 256 + 128] = jnp.where(low, first, 0.0).astype(BF16)
        k_ref[0, :, kv * 256 + 128:(kv + 1) * 256] = jnp.where(low, 0.0, second).astype(BF16)
    vt_ref[0] = (_dot_nt(wvt_ref[...], u) + vb_ref[...]).astype(BF16)
    z_ref[0] = _dot(u, w_ref[:, o3:]).astype(BF16)


def _pre_a(h, g, w_in):
    b, s, d = h.shape
    tm = min(ROW_TILE, s)
    n_in = w_in.shape[1]
    o2, o3 = A_WIDTH + A_KV_WIDTH, A_WIDTH + 2 * A_KV_WIDTH
    pad = A_V_ROWS - A_HEAD_DIM
    wv = w_in[:, o2:o3].reshape(d, A_KV_HEADS, A_HEAD_DIM)
    wvt = jnp.concatenate([wv, jnp.zeros((d, A_KV_HEADS, pad), w_in.dtype)], axis=2)
    wvt = wvt.reshape(d, A_KV_HEADS * A_V_ROWS).T.astype(BF16)
    vbias = jnp.tile(jnp.concatenate([jnp.zeros((A_HEAD_DIM,), F32), jnp.ones((pad,), F32)]),
                     A_KV_HEADS).reshape(A_KV_HEADS * A_V_ROWS, 1)
    vrows = A_KV_HEADS * A_V_ROWS
    n_pairs = A_WIDTH // 128
    assert tm % WIN_Q_BLOCK == 0
    row = lambda w: pl.BlockSpec((1, tm, w), lambda bi, i: (bi, i, 0))
    return pl.pallas_call(
        _pre_a_kernel,
        grid=(b, s // tm),
        in_specs=[row(d), _const_spec((1, d)), _const_spec((d, n_in)), _const_spec((A_WIDTH, d)),
                  _const_spec((vrows, d)), _const_spec((vrows, 1))],
        out_specs=[pl.BlockSpec((1, n_pairs, tm // WIN_Q_BLOCK, 128, WIN_Q_BLOCK), lambda bi, i: (bi, 0, i, 0, 0)),
                   row(A_KV_HEADS * 256), pl.BlockSpec((1, vrows, tm), lambda bi, i: (bi, 0, i)),
                   row(A_WIDTH)],
        out_shape=[
            jax.ShapeDtypeStruct((b, n_pairs, s // WIN_Q_BLOCK, 128, WIN_Q_BLOCK), BF16),
            jax.ShapeDtypeStruct((b, s, A_KV_HEADS * 256), BF16),
            jax.ShapeDtypeStruct((b, vrows, s), BF16),
            jax.ShapeDtypeStruct((b, s, A_WIDTH), BF16),
        ],
        compiler_params=_params("parallel", "parallel"),
    )(h, g.reshape(1, d), w_in.astype(BF16), w_in[:, :A_WIDTH].T.astype(BF16), wvt, vbias)


def _win_kernel(sink_ref, q_ref, kp_ref, kc_ref, kn_ref, vp_ref, vc_ref, vn_ref, z_ref, o_ref,
                s0_ref, s1_ref, p0_ref, p1_ref, *, seq):
    i = pl.program_id(1)
    s_bufs, p_bufs = (s0_ref, s1_ref), (p0_ref, p1_ref)
    n_keys = WIN_Q_BLOCK + 2 * BLOCK
    r = lax.broadcasted_iota(jnp.int32, (n_keys, WIN_Q_BLOCK), 0)
    c = lax.broadcasted_iota(jnp.int32, (n_keys, WIN_Q_BLOCK), 1)
    rel = r - BLOCK - c
    k_pos = i * WIN_Q_BLOCK - BLOCK + r
    valid = (jnp.abs(rel) <= WINDOW) & (k_pos >= 0) & (k_pos < seq)
    neg_dist = jnp.where(valid, -jnp.abs(rel).astype(F32), -MASK_DIST)

    def scores(h, slot):
        kv, hp = h // A_GROUP, h // 2
        cols = slice(kv * 256 + (h % 2) * 128, kv * 256 + (h % 2 + 1) * 128)
        k_sel = jnp.concatenate([kp_ref[0, :, cols], kc_ref[0, :, cols], kn_ref[0, :, cols]], axis=0)
        slope = 2.0 ** (-8.0 * (h + 1) / A_HEADS) * LOG2E
        s = _dot(k_sel, q_ref[0, hp, 0]) + slope * neg_dist
        s_bufs[slot][...] = s
        return jnp.max(s, axis=0, keepdims=True)

    def exps(h, slot, tmax):
        m = jnp.maximum(tmax, sink_ref[h] * LOG2E)
        p_bufs[slot][...] = jnp.exp2(s_bufs[slot][...] - m).astype(BF16)
        return m

    def values(h, slot, m):
        kv = h // A_GROUP
        rows = slice(kv * A_V_ROWS, (kv + 1) * A_V_ROWS)
        vt = jnp.concatenate([vp_ref[0, rows, :], vc_ref[0, rows, :], vn_ref[0, rows, :]], axis=1)
        acc = _dot(vt, p_bufs[slot][...])
        denom = acc[A_HEAD_DIM:A_HEAD_DIM + 1] + jnp.exp2(sink_ref[h] * LOG2E - m)
        return acc[:A_HEAD_DIM] / denom

    tmax, m, pending = {}, {}, {}
    for t in range(A_HEADS + 2):
        if t < A_HEADS:
            tmax[t] = scores(t, t % 2)
        if t >= 2:
            h = t - 2
            pending[h] = values(h, h % 2, m.pop(h))
            if h % 2 == 1:
                o_t = jnp.concatenate([pending.pop(h - 1), pending.pop(h)], axis=0)
                lanes = slice((h // 2) * 128, (h // 2 + 1) * 128)
                z = z_ref[0, :, lanes].astype(F32)
                o_ref[0, :, lanes] = (o_t.T * (z * jax.nn.sigmoid(z))).astype(BF16)
        if 1 <= t <= A_HEADS:
            m[t - 1] = exps(t - 1, (t - 1) % 2, tmax.pop(t - 1))


def _win_attn(q_t, k_exp, vt, z, sink):
    b, s, _ = k_exp.shape
    qb = WIN_Q_BLOCK
    assert s % qb == 0
    per = qb // BLOCK
    nb = s // BLOCK
    vrows = vt.shape[1]
    kw = k_exp.shape[-1]
    prev_i = lambda i: jnp.maximum(per * i - 1, 0)
    next_i = lambda i: jnp.minimum(per * i + per, nb - 1)
    cur = lambda w: pl.BlockSpec((1, qb, w), lambda bi, i: (bi, i, 0))
    s_buf = pltpu.VMEM((qb + 2 * BLOCK, qb), F32)
    p_buf = pltpu.VMEM((qb + 2 * BLOCK, qb), BF16)
    return pl.pallas_call(
        functools.partial(_win_kernel, seq=s),
        grid=(b, s // qb),
        in_specs=[pl.BlockSpec(memory_space=pltpu.SMEM),
                  pl.BlockSpec((1, q_t.shape[1], 1, 128, qb), lambda bi, i: (bi, 0, i, 0, 0)),
                  pl.BlockSpec((1, BLOCK, kw), lambda bi, i: (bi, prev_i(i), 0)),
                  cur(kw),
                  pl.BlockSpec((1, BLOCK, kw), lambda bi, i: (bi, next_i(i), 0)),
                  pl.BlockSpec((1, vrows, BLOCK), lambda bi, i: (bi, 0, prev_i(i))),
                  pl.BlockSpec((1, vrows, qb), lambda bi, i: (bi, 0, i)),
                  pl.BlockSpec((1, vrows, BLOCK), lambda bi, i: (bi, 0, next_i(i))),
                  cur(A_WIDTH)],
        out_specs=cur(A_WIDTH),
        out_shape=jax.ShapeDtypeStruct((b, s, A_WIDTH), BF16),
        scratch_shapes=[s_buf, s_buf, p_buf, p_buf],
        compiler_params=_params("parallel", "parallel"),
    )(sink.astype(F32), q_t, k_exp, k_exp, k_exp, vt, vt, vt, z)


def _post_kernel(og_ref, h_ref, p_ref, wo_ref, pg_ref, wg_ref, pw_ref, fg_ref, o_ref, *, final):
    h1 = h_ref[0] + _dot(og_ref[0], wo_ref[...])
    gate = jax.nn.sigmoid(_dot(_rms(h1, pg_ref[...]).astype(BF16), wg_ref[...]))
    h2 = h1 + _dot(p_ref[0, 0].astype(BF16), pw_ref[...]) * gate
    if final:
        h2 = _rms(h2, fg_ref[...])
    o_ref[0] = h2


def _post(og, h, p, layer, w_out, ple_g, w_gate, ple_w, final_g, final):
    b, s, d = h.shape
    tm = min(ROW_TILE, s)
    pd = p.shape[-1]
    row = lambda w: pl.BlockSpec((1, tm, w), lambda bi, i: (bi, i, 0))
    return pl.pallas_call(
        functools.partial(_post_kernel, final=final),
        grid=(b, s // tm),
        in_specs=[row(og.shape[-1]), row(d), pl.BlockSpec((1, 1, tm, pd), lambda bi, i: (layer, bi, i, 0)),
                  _const_spec(w_out.shape), _const_spec((1, d)),
                  _const_spec((d, d)), _const_spec((pd, d)), _const_spec((1, d))],
        out_specs=row(d),
        out_shape=jax.ShapeDtypeStruct((b, s, d), F32),
        compiler_params=_params("parallel", "parallel"),
    )(og, h, p, w_out.astype(BF16), ple_g.reshape(1, d), w_gate.astype(BF16), ple_w.astype(BF16),
      final_g.reshape(1, d))


def _pre_b_kernel(x_ref, g_ref, w_ref, qn_ref, wqt_ref, kn_ref, wk_ref, wvt_ref, vb_ref, r_ref, rt_ref,
                  qt_ref, k_ref, vt_ref, z_ref):
    u = _rms(x_ref[0], g_ref[...]).astype(BF16)
    o1, o2, o3 = Q_LORA, Q_LORA + KV_LORA, Q_LORA + KV_LORA + B_QK_PAD
    rot = r_ref[...]
    cq = _rms(_dot(u, w_ref[:, :o1]), qn_ref[...]).astype(BF16)
    qs = ((B_NOPE + B_ROPE) ** -0.5) * LOG2E
    q_t = (_dot_nt(wqt_ref[...], cq) * jnp.tile(rt_ref[...] * qs, (B_HEADS, 1))).astype(BF16)
    for hh in range(B_HEADS):
        for c in range(q_t.shape[1] // MLA_Q_TILE):
            qt_ref[0, hh, c] = q_t[hh * B_QK_PAD:(hh + 1) * B_QK_PAD, c * MLA_Q_TILE:(c + 1) * MLA_Q_TILE]
    ckv = _rms(_dot(u, w_ref[:, o1:o2]), kn_ref[...]).astype(BF16)
    y = _dot(u, w_ref[:, o2:o3]) * rot
    lane = lax.broadcasted_iota(jnp.int32, y.shape, 1)
    both = pltpu.roll(y, 32, 1) + pltpu.roll(y, 96, 1)
    kr = y + jnp.where(lane >= B_NOPE, both, 0.0)
    k = (_dot(ckv, wk_ref[...]) + jnp.tile(kr, (1, B_HEADS))).astype(BF16)
    for hh in range(B_HEADS):
        k_ref[0, hh] = k[:, hh * B_QK_PAD:(hh + 1) * B_QK_PAD]
    vt_ref[0, 0] = (_dot_nt(wvt_ref[...], ckv) + vb_ref[...]).astype(BF16)
    z_ref[0] = _dot(u, w_ref[:, o3:]).astype(BF16)


def _pre_b(h, g, w_in_p, q_norm, wq_p, kv_norm, wk_p, wvt_p, vbias, rot):
    b, s, d = h.shape
    tm = min(MLA_KV_TILE, s)
    assert tm % MLA_Q_TILE == 0
    row = lambda w: pl.BlockSpec((1, tm, w), lambda bi, i: (bi, i, 0))
    vrows = B_HEADS * B_V_ROWS
    return pl.pallas_call(
        _pre_b_kernel,
        grid=(b, s // tm),
        in_specs=[row(d), _const_spec((1, d)), _const_spec(w_in_p.shape), _const_spec((1, Q_LORA)),
                  _const_spec((wq_p.shape[1], wq_p.shape[0])), _const_spec((1, KV_LORA)), _const_spec(wk_p.shape),
                  _const_spec(wvt_p.shape), _const_spec((vrows, 1)),
                  pl.BlockSpec((tm, B_QK_PAD), lambda bi, i: (i, 0)),
                  pl.BlockSpec((B_QK_PAD, tm), lambda bi, i: (0, i))],
        out_specs=[pl.BlockSpec((1, B_HEADS, tm // MLA_Q_TILE, B_QK_PAD, MLA_Q_TILE), lambda bi, i: (bi, 0, i, 0, 0)),
                   pl.BlockSpec((1, B_HEADS, tm, B_QK_PAD), lambda bi, i: (bi, 0, i, 0)),
                   pl.BlockSpec((1, 1, vrows, tm), lambda bi, i: (bi, i, 0, 0)), row(B_WIDTH)],
        out_shape=[
            jax.ShapeDtypeStruct((b, B_HEADS, s // MLA_Q_TILE, B_QK_PAD, MLA_Q_TILE), BF16),
            jax.ShapeDtypeStruct((b, B_HEADS, s, B_QK_PAD), BF16),
            jax.ShapeDtypeStruct((b, s // tm, vrows, tm), BF16),
            jax.ShapeDtypeStruct((b, s, B_WIDTH), BF16),
        ],
        compiler_params=_params("parallel", "parallel"),
    )(h, g.reshape(1, d), w_in_p, q_norm.reshape(1, Q_LORA), wq_p.T, kv_norm.reshape(1, KV_LORA), wk_p,
      wvt_p, vbias, rot, rot.T)


def _prep_b_weights(w_in, w_qb, w_kvb):
    half = B_ROPE // 2
    d = w_in.shape[0]
    o1, o2, o3 = Q_LORA, Q_LORA + KV_LORA, Q_LORA + KV_LORA + B_ROPE
    kr = w_in[:, o2:o3]
    kr_swapped = jnp.concatenate([kr[:, half:], kr[:, :half]], axis=1)
    kr_block = jnp.concatenate([jnp.zeros((d, B_NOPE), w_in.dtype), kr, kr_swapped], axis=1)
    w_in_p = jnp.concatenate([w_in[:, :o2], kr_block, w_in[:, o3:]], axis=1).astype(BF16)

    wq = w_qb.reshape(Q_LORA, B_HEADS, B_NOPE + B_ROPE)
    q_rope = wq[:, :, B_NOPE:]
    q_rope_swapped = jnp.concatenate([q_rope[:, :, half:], q_rope[:, :, :half]], axis=2)
    wq_p = jnp.concatenate([wq[:, :, :B_NOPE], q_rope, q_rope_swapped], axis=2)
    wq_p = wq_p.reshape(Q_LORA, B_HEADS * B_QK_PAD).astype(BF16)

    wkv = w_kvb.reshape(KV_LORA, B_HEADS, B_NOPE + B_VDIM)
    wk_p = jnp.concatenate([wkv[:, :, :B_NOPE], jnp.zeros((KV_LORA, B_HEADS, B_QK_PAD - B_NOPE), w_kvb.dtype)],
                           axis=2).reshape(KV_LORA, B_HEADS * B_QK_PAD).astype(BF16)
    wv = jnp.concatenate([wkv[:, :, B_NOPE:], jnp.zeros((KV_LORA, B_HEADS, B_V_ROWS - B_VDIM), w_kvb.dtype)],
                         axis=2).reshape(KV_LORA, B_HEADS * B_V_ROWS)
    wvt_p = wv.T.astype(BF16)
    vbias = jnp.tile(jnp.concatenate([jnp.zeros((B_VDIM,), F32), jnp.ones((B_V_ROWS - B_VDIM,), F32)]),
                     B_HEADS).reshape(B_HEADS * B_V_ROWS, 1)
    return w_in_p, wq_p, wk_p, wvt_p, vbias


def _rope_table(s):
    half = B_ROPE // 2
    inv_freq = ROPE_THETA ** (-jnp.arange(half, dtype=F32) / half)
    ang = jnp.arange(s, dtype=F32)[:, None] * inv_freq[None, :]
    cos, sin = jnp.cos(ang), jnp.sin(ang)
    return jnp.concatenate([jnp.ones((s, B_NOPE), F32), cos, cos, -sin, sin], axis=1)


def _mla_kernel(q_ref, k_ref, vt_ref, z_ref, o_ref, s0_ref, s1_ref, p0_ref, p1_ref, *, q_tile, q_sub, kv_tile):
    kv_chunks = kv_tile // vt_ref.shape[3]
    n_kv = vt_ref.shape[1] // kv_chunks
    n_q = q_ref.shape[2] // q_sub
    s_bufs, p_bufs = (s0_ref, s1_ref), (p0_ref, p1_ref)
    chains = [(hh, sub) for sub in range(q_sub) for hh in range(B_HEAD_PAIR)]
    m_init = (jnp.full((1, q_tile), NEG_BIG, F32),) * len(chains)

    def q_rows(qt, sub):
        return pl.ds(pl.multiple_of((qt * q_sub + sub) * q_tile, q_tile), q_tile)

    def load_q(qt):
        return [q_ref[0, hh, qt * q_sub + sub] for hh, sub in chains]

    def score_chain(qc, j, slot, c):
        kc = k_ref[0, chains[c][0], j * kv_tile:(j + 1) * kv_tile, :]
        s_bufs[slot][c] = _dot(kc, qc[c])

    def scores(qc, j, slot):
        for c in range(len(chains)):
            score_chain(qc, j, slot, c)

    def exp_chain(slot, m, c):
        m_new = jnp.maximum(m[c], jnp.max(s_bufs[slot][c], axis=0, keepdims=True))
        p_bufs[slot][c] = jnp.exp2(s_bufs[slot][c] - m_new).astype(BF16)
        return m_new, jnp.exp2(m[c] - m_new)

    def exps(slot, m):
        res = [exp_chain(slot, m, c) for c in range(len(chains))]
        return tuple(r[0] for r in res), tuple(r[1] for r in res)

    def value_chain(j, slot, acc, alpha, c):
        rows = slice(chains[c][0] * B_V_ROWS, (chains[c][0] + 1) * B_V_ROWS)
        vt = jnp.concatenate([vt_ref[0, j * kv_chunks + i, rows, :] for i in range(kv_chunks)], axis=1)
        return acc[c] * alpha[c] + _dot(vt, p_bufs[slot][c])

    def step(qc, j_s, j_v, acc, alpha, m):
        s_slot, e_slot, v_slot = j_s % 2, (j_s - 1) % 2, j_v % 2
        acc_out, m_out, alpha_out = [], [], []
        for c in range(len(chains)):
            score_chain(qc, j_s % n_kv, s_slot, c)
            acc_out.append(value_chain(j_v, v_slot, acc, alpha, c))
            m_new, a_new = exp_chain(e_slot, m, c)
            m_out.append(m_new)
            alpha_out.append(a_new)
        return tuple(acc_out), tuple(m_out), tuple(alpha_out)

    def tile_body(qt, carry):
        m, alpha = carry
        qc = load_q(qt)
        qc_next = load_q(jnp.minimum(qt + 1, n_q - 1))
        acc = (jnp.zeros((B_V_ROWS, q_tile), F32),) * len(chains)
        for t in range(2, n_kv):
            acc, m, alpha = step(qc, t, t - 2, acc, alpha, m)
        acc, m, alpha = step(qc_next, n_kv, n_kv - 2, acc, alpha, m)
        acc, m_next, alpha_next = step(qc_next, n_kv + 1, n_kv - 1, acc, alpha, m_init)
        for sub in range(q_sub):
            o_t = jnp.concatenate([acc[c][:B_VDIM] / acc[c][B_VDIM:B_VDIM + 1]
                                   for c, (_, s_) in enumerate(chains) if s_ == sub], axis=0)
            z = z_ref[0, q_rows(qt, sub), :].astype(F32)
            o_ref[0, q_rows(qt, sub), :] = (o_t.T * (z * jax.nn.sigmoid(z))).astype(BF16)
        return m_next, alpha_next

    qc0 = load_q(0)
    scores(qc0, 0, 0)
    m, alpha = exps(0, m_init)
    scores(qc0, 1, 1)
    lax.fori_loop(0, n_q, tile_body, (m, alpha))


def _mla_attn(q_t, k, vt, z):
    b, _, s, _ = k.shape
    n_chunks, kv_chunk = vt.shape[1], vt.shape[3]
    kv_tile = kv_chunk * (MLA_KV_CHUNKS if n_chunks % (2 * MLA_KV_CHUNKS) == 0 else 1)
    n_kv = s // kv_tile
    assert n_kv >= 2 and n_kv % 2 == 0
    q_block = min(MLA_Q_BLOCK, s)
    q_tile = q_t.shape[-1]
    out_w = B_HEAD_PAIR * B_VDIM
    q_sub = MLA_Q_SUB if q_block % (MLA_Q_SUB * q_tile) == 0 else 1
    s_buf = pltpu.VMEM((B_HEAD_PAIR * q_sub, kv_tile, q_tile), F32)
    p_buf = pltpu.VMEM((B_HEAD_PAIR * q_sub, kv_tile, q_tile), BF16)
    return pl.pallas_call(
        functools.partial(_mla_kernel, q_tile=q_tile, q_sub=q_sub, kv_tile=kv_tile),
        grid=(b, B_HEADS // B_HEAD_PAIR, s // q_block),
        in_specs=[pl.BlockSpec((1, B_HEAD_PAIR, q_block // q_tile, B_QK_PAD, q_tile),
                               lambda bi, hp, qi: (bi, hp, qi, 0, 0)),
                  pl.BlockSpec((1, B_HEAD_PAIR, s, B_QK_PAD), lambda bi, hp, qi: (bi, hp, 0, 0)),
                  pl.BlockSpec((1, n_chunks, B_HEAD_PAIR * B_V_ROWS, kv_chunk), lambda bi, hp, qi: (bi, 0, hp, 0)),
                  pl.BlockSpec((1, q_block, out_w), lambda bi, hp, qi: (bi, qi, hp))],
        out_specs=pl.BlockSpec((1, q_block, out_w), lambda bi, hp, qi: (bi, qi, hp)),
        out_shape=jax.ShapeDtypeStruct((b, s, B_WIDTH), BF16),
        scratch_shapes=[s_buf, s_buf, p_buf, p_buf],
        compiler_params=_params("parallel", "parallel", "arbitrary"),
    )(q_t, k, vt, z)


def kernel(x, p, norm_g, a_w_in, a_sink, a_w_out, b_w_in, b_q_norm, b_w_qb, b_kv_norm, b_w_kvb, b_w_out,
           ple_w, ple_norm_g, ple_w_gate, final_norm_g):
    depth = p.shape[0]
    s = x.shape[1]
    rot = _rope_table(s)
    h = x
    for i in range(depth):
        j = i // 2
        if i % 2 == 0:
            q, k_exp, vt, z = _pre_a(h, norm_g[i], a_w_in[j])
            og = _win_attn(q, k_exp, vt, z, a_sink[j])
            w_out = a_w_out[j]
        else:
            w_in_p, wq_p, wk_p, wvt_p, vbias = _prep_b_weights(b_w_in[j], b_w_qb[j], b_w_kvb[j])
            q, k, vt, z = _pre_b(h, norm_g[i], w_in_p, b_q_norm[j], wq_p, b_kv_norm[j], wk_p, wvt_p, vbias, rot)
            og = _mla_attn(q, k, vt, z)
            w_out = b_w_out[j]
        h = _post(og, h, p, i, w_out, ple_norm_g[i], ple_w_gate[i], ple_w[i], final_norm_g, final=(i == depth - 1))
    return h
```

```python
import functools

import jax
import jax.numpy as jnp
from jax import lax
from jax.experimental import pallas as pl
from jax.experimental.pallas import tpu as pltpu

F32 = jnp.float32
BF16 = jnp.bfloat16

EPS = 1e-6
NEG_BIG = -1e30
LOG2E = 1.4426950408889634

A_HEADS = 16
A_KV_HEADS = 4
A_GROUP = A_HEADS // A_KV_HEADS
A_HEAD_DIM = 64
A_WIDTH = A_HEADS * A_HEAD_DIM
A_KV_WIDTH = A_KV_HEADS * A_HEAD_DIM
WINDOW = 128
BLOCK = 128
A_V_ROWS = 80
WIN_Q_BLOCK = 256
MASK_DIST = 1e33

B_HEADS = 16
B_NOPE = 64
B_ROPE = 32
B_VDIM = 64
B_WIDTH = B_HEADS * B_VDIM
Q_LORA = 384
KV_LORA = 256
ROPE_THETA = 10000.0
B_QK_PAD = 128
B_V_ROWS = 128
B_HEAD_PAIR = 2

ROW_TILE = 512
MLA_Q_BLOCK = 2048
MLA_Q_TILE = 256
MLA_Q_SUB = 1
MLA_KV_TILE = 512
MLA_KV_CHUNKS = 1

VMEM_LIMIT = 56 * 1024 * 1024


def _rms(x, g):
    ms = jnp.mean(x * x, axis=-1, keepdims=True)
    return x * lax.rsqrt(ms + EPS) * g


def _dot(a, b):
    return jnp.dot(a, b, preferred_element_type=F32)


def _dot_nt(a, b):
    return lax.dot_general(a, b, (((1,), (1,)), ((), ())), preferred_element_type=F32)


def _params(*sem):
    return pltpu.CompilerParams(dimension_semantics=sem, vmem_limit_bytes=VMEM_LIMIT)


def _const_spec(shape):
    nd = len(shape)
    return pl.BlockSpec(shape, lambda *_: (0,) * nd)


def _pre_a_kernel(x_ref, g_ref, w_ref, wqt_ref, wvt_ref, vb_ref, qt_ref, k_ref, vt_ref, z_ref):
    u = _rms(x_ref[0], g_ref[...]).astype(BF16)
    qs = (A_HEAD_DIM ** -0.5) * LOG2E
    o1, o2, o3 = A_WIDTH, A_WIDTH + A_KV_WIDTH, A_WIDTH + 2 * A_KV_WIDTH
    q_t = (_dot_nt(wqt_ref[...], u) * qs).astype(BF16)
    for hp in range(A_WIDTH // 128):
        for c in range(q_t.shape[1] // WIN_Q_BLOCK):
            qt_ref[0, hp, c] = q_t[hp * 128:(hp + 1) * 128, c * WIN_Q_BLOCK:(c + 1) * WIN_Q_BLOCK]
    k = _dot(u, w_ref[:, o1:o2])
    lane = lax.broadcasted_iota(jnp.int32, (k.shape[0], 2 * A_HEAD_DIM), 1)
    low = lane < A_HEAD_DIM
    for kv in range(A_KV_HEADS):
        blk = k[:, (kv // 2) * 128:(kv // 2 + 1) * 128]
        swapped = pltpu.roll(blk, A_HEAD_DIM, 1)
        first, second = (blk, swapped) if kv % 2 == 0 else (swapped, blk)
        k_ref[0, :, kv * 256:kv * 256 + 128] = jnp.where(low, first, 0.0).astype(BF16)
        k_ref[0, :, kv * 256 + 128:(kv + 1) * 256] = jnp.where(low, 0.0, second).astype(BF16)
    vt_ref[0] = (_dot_nt(wvt_ref[...], u) + vb_ref[...]).astype(BF16)
    z_ref[0] = _dot(u, w_ref[:, o3:]).astype(BF16)


def _pre_a(h, g, w_in):
    b, s, d = h.shape
    tm = min(ROW_TILE, s)
    n_in = w_in.shape[1]
    o2, o3 = A_WIDTH + A_KV_WIDTH, A_WIDTH + 2 * A_KV_WIDTH
    pad = A_V_ROWS - A_HEAD_DIM
    wv = w_in[:, o2:o3].reshape(d, A_KV_HEADS, A_HEAD_DIM)
    wvt = jnp.concatenate([wv, jnp.zeros((d, A_KV_HEADS, pad), w_in.dtype)], axis=2)
    wvt = wvt.reshape(d, A_KV_HEADS * A_V_ROWS).T.astype(BF16)
    vbias = jnp.tile(jnp.concatenate([jnp.zeros((A_HEAD_DIM,), F32), jnp.ones((pad,), F32)]),
                     A_KV_HEADS).reshape(A_KV_HEADS * A_V_ROWS, 1)
    vrows = A_KV_HEADS * A_V_ROWS
    n_pairs = A_WIDTH // 128
    assert tm % WIN_Q_BLOCK == 0
    row = lambda w: pl.BlockSpec((1, tm, w), lambda bi, i: (bi, i, 0))
    return pl.pallas_call(
        _pre_a_kernel,
        grid=(b, s // tm),
        in_specs=[row(d), _const_spec((1, d)), _const_spec((d, n_in)), _const_spec((A_WIDTH, d)),
                  _const_spec((vrows, d)), _const_spec((vrows, 1))],
        out_specs=[pl.BlockSpec((1, n_pairs, tm // WIN_Q_BLOCK, 128, WIN_Q_BLOCK), lambda bi, i: (bi, 0, i, 0, 0)),
                   row(A_KV_HEADS * 256), pl.BlockSpec((1, vrows, tm), lambda bi, i: (bi, 0, i)),
                   row(A_WIDTH)],
        out_shape=[
            jax.ShapeDtypeStruct((b, n_pairs, s // WIN_Q_BLOCK, 128, WIN_Q_BLOCK), BF16),
            jax.ShapeDtypeStruct((b, s, A_KV_HEADS * 256), BF16),
            jax.ShapeDtypeStruct((b, vrows, s), BF16),
            jax.ShapeDtypeStruct((b, s, A_WIDTH), BF16),
        ],
        compiler_params=_params("parallel", "parallel"),
    )(h, g.reshape(1, d), w_in.astype(BF16), w_in[:, :A_WIDTH].T.astype(BF16), wvt, vbias)


def _win_kernel(sink_ref, q_ref, kp_ref, kc_ref, kn_ref, vp_ref, vc_ref, vn_ref, z_ref, o_ref,
                s0_ref, s1_ref, p0_ref, p1_ref, *, seq):
    i = pl.program_id(1)
    s_bufs, p_bufs = (s0_ref, s1_ref), (p0_ref, p1_ref)
    n_keys = WIN_Q_BLOCK + 2 * BLOCK
    r = lax.broadcasted_iota(jnp.int32, (n_keys, WIN_Q_BLOCK), 0)
    c = lax.broadcasted_iota(jnp.int32, (n_keys, WIN_Q_BLOCK), 1)
    rel = r - BLOCK - c
    k_pos = i * WIN_Q_BLOCK - BLOCK + r
    valid = (jnp.abs(rel) <= WINDOW) & (k_pos >= 0) & (k_pos < seq)
    neg_dist = jnp.where(valid, -jnp.abs(rel).astype(F32), -MASK_DIST)

    def scores(h, slot):
        kv, hp = h // A_GROUP, h // 2
        cols = slice(kv * 256 + (h % 2) * 128, kv * 256 + (h % 2 + 1) * 128)
        k_sel = jnp.concatenate([kp_ref[0, :, cols], kc_ref[0, :, cols], kn_ref[0, :, cols]], axis=0)
        slope = 2.0 ** (-8.0 * (h + 1) / A_HEADS) * LOG2E
        s = _dot(k_sel, q_ref[0, hp, 0]) + slope * neg_dist
        s_bufs[slot][...] = s
        return jnp.max(s, axis=0, keepdims=True)

    def exps(h, slot, tmax):
        m = jnp.maximum(tmax, sink_ref[h] * LOG2E)
        p_bufs[slot][...] = jnp.exp2(s_bufs[slot][...] - m).astype(BF16)
        return m

    def values(h, slot, m):
        kv = h // A_GROUP
        rows = slice(kv * A_V_ROWS, (kv + 1) * A_V_ROWS)
        vt = jnp.concatenate([vp_ref[0, rows, :], vc_ref[0, rows, :], vn_ref[0, rows, :]], axis=1)
        acc = _dot(vt, p_bufs[slot][...])
        denom = acc[A_HEAD_DIM:A_HEAD_DIM + 1] + jnp.exp2(sink_ref[h] * LOG2E - m)
        return acc[:A_HEAD_DIM] / denom

    tmax, m, pending = {}, {}, {}
    for t in range(A_HEADS + 2):
        if t < A_HEADS:
            tmax[t] = scores(t, t % 2)
        if t >= 2:
            h = t - 2
            pending[h] = values(h, h % 2, m.pop(h))
            if h % 2 == 1:
                o_t = jnp.concatenate([pending.pop(h - 1), pending.pop(h)], axis=0)
                lanes = slice((h // 2) * 128, (h // 2 + 1) * 128)
                z = z_ref[0, :, lanes].astype(F32)
                o_ref[0, :, lanes] = (o_t.T * (z * jax.nn.sigmoid(z))).astype(BF16)
        if 1 <= t <= A_HEADS:
            m[t - 1] = exps(t - 1, (t - 1) % 2, tmax.pop(t - 1))


def _win_attn(q_t, k_exp, vt, z, sink):
    b, s, _ = k_exp.shape
    qb = WIN_Q_BLOCK
    assert s % qb == 0
    per = qb // BLOCK
    nb = s // BLOCK
    vrows = vt.shape[1]
    kw = k_exp.shape[-1]
    prev_i = lambda i: jnp.maximum(per * i - 1, 0)
    next_i = lambda i: jnp.minimum(per * i + per, nb - 1)
    cur = lambda w: pl.BlockSpec((1, qb, w), lambda bi, i: (bi, i, 0))
    s_buf = pltpu.VMEM((qb + 2 * BLOCK, qb), F32)
    p_buf = pltpu.VMEM((qb + 2 * BLOCK, qb), BF16)
    return pl.pallas_call(
        functools.partial(_win_kernel, seq=s),
        grid=(b, s // qb),
        in_specs=[pl.BlockSpec(memory_space=pltpu.SMEM),
                  pl.BlockSpec((1, q_t.shape[1], 1, 128, qb), lambda bi, i: (bi, 0, i, 0, 0)),
                  pl.BlockSpec((1, BLOCK, kw), lambda bi, i: (bi, prev_i(i), 0)),
                  cur(kw),
                  pl.BlockSpec((1, BLOCK, kw), lambda bi, i: (bi, next_i(i), 0)),
                  pl.BlockSpec((1, vrows, BLOCK), lambda bi, i: (bi, 0, prev_i(i))),
                  pl.BlockSpec((1, vrows, qb), lambda bi, i: (bi, 0, i)),
                  pl.BlockSpec((1, vrows, BLOCK), lambda bi, i: (bi, 0, next_i(i))),
                  cur(A_WIDTH)],
        out_specs=cur(A_WIDTH),
        out_shape=jax.ShapeDtypeStruct((b, s, A_WIDTH), BF16),
        scratch_shapes=[s_buf, s_buf, p_buf, p_buf],
        compiler_params=_params("parallel", "parallel"),
    )(sink.astype(F32), q_t, k_exp, k_exp, k_exp, vt, vt, vt, z)


def _post_kernel(og_ref, h_ref, p_ref, wo_ref, pg_ref, wg_ref, pw_ref, fg_ref, o_ref, *, final):
    h1 = h_ref[0] + _dot(og_ref[0], wo_ref[...])
    gate = jax.nn.sigmoid(_dot(_rms(h1, pg_ref[...]).astype(BF16), wg_ref[...]))
    h2 = h1 + _dot(p_ref[0, 0].astype(BF16), pw_ref[...]) * gate
    if final:
        h2 = _rms(h2, fg_ref[...])
    o_ref[0] = h2


def _post(og, h, p, layer, w_out, ple_g, w_gate, ple_w, final_g, final):
    b, s, d = h.shape
    tm = min(ROW_TILE, s)
    pd = p.shape[-1]
    row = lambda w: pl.BlockSpec((1, tm, w), lambda bi, i: (bi, i, 0))
    return pl.pallas_call(
        functools.partial(_post_kernel, final=final),
        grid=(b, s // tm),
        in_specs=[row(og.shape[-1]), row(d), pl.BlockSpec((1, 1, tm, pd), lambda bi, i: (layer, bi, i, 0)),
                  _const_spec(w_out.shape), _const_spec((1, d)),
                  _const_spec((d, d)), _const_spec((pd, d)), _const_spec((1, d))],
        out_specs=row(d),
        out_shape=jax.ShapeDtypeStruct((b, s, d), F32),
        compiler_params=_params("parallel", "parallel"),
    )(og, h, p, w_out.astype(BF16), ple_g.reshape(1, d), w_gate.astype(BF16), ple_w.astype(BF16),
      final_g.reshape(1, d))


def _pre_b_kernel(x_ref, g_ref, w_ref, qn_ref, wqt_ref, kn_ref, wk_ref, wvt_ref, vb_ref, r_ref, rt_ref,
                  qt_ref, k_ref, vt_ref, z_ref):
    u = _rms(x_ref[0], g_ref[...]).astype(BF16)
    o1, o2, o3 = Q_LORA, Q_LORA + KV_LORA, Q_LORA + KV_LORA + B_QK_PAD
    rot = r_ref[...]
    cq = _rms(_dot(u, w_ref[:, :o1]), qn_ref[...]).astype(BF16)
    qs = ((B_NOPE + B_ROPE) ** -0.5) * LOG2E
    q_t = (_dot_nt(wqt_ref[...], cq) * jnp.tile(rt_ref[...] * qs, (B_HEADS, 1))).astype(BF16)
    for hh in range(B_HEADS):
        for c in range(q_t.shape[1] // MLA_Q_TILE):
            qt_ref[0, hh, c] = q_t[hh * B_QK_PAD:(hh + 1) * B_QK_PAD, c * MLA_Q_TILE:(c + 1) * MLA_Q_TILE]
    ckv = _rms(_dot(u, w_ref[:, o1:o2]), kn_ref[...]).astype(BF16)
    y = _dot(u, w_ref[:, o2:o3]) * rot
    lane = lax.broadcasted_iota(jnp.int32, y.shape, 1)
    both = pltpu.roll(y, 32, 1) + pltpu.roll(y, 96, 1)
    kr = y + jnp.where(lane >= B_NOPE, both, 0.0)
    k = (_dot(ckv, wk_ref[...]) + jnp.tile(kr, (1, B_HEADS))).astype(BF16)
    for hh in range(B_HEADS):
        k_ref[0, hh] = k[:, hh * B_QK_PAD:(hh + 1) * B_QK_PAD]
    vt_ref[0, 0] = (_dot_nt(wvt_ref[...], ckv) + vb_ref[...]).astype(BF16)
    z_ref[0] = _dot(u, w_ref[:, o3:]).astype(BF16)


def _pre_b(h, g, w_in_p, q_norm, wq_p, kv_norm, wk_p, wvt_p, vbias, rot):
    b, s, d = h.shape
    tm = min(MLA_KV_TILE, s)
    assert tm % MLA_Q_TILE == 0
    row = lambda w: pl.BlockSpec((1, tm, w), lambda bi, i: (bi, i, 0))
    vrows = B_HEADS * B_V_ROWS
    return pl.pallas_call(
        _pre_b_kernel,
        grid=(b, s // tm),
        in_specs=[row(d), _const_spec((1, d)), _const_spec(w_in_p.shape), _const_spec((1, Q_LORA)),
                  _const_spec((wq_p.shape[1], wq_p.shape[0])), _const_spec((1, KV_LORA)), _const_spec(wk_p.shape),
                  _const_spec(wvt_p.shape), _const_spec((vrows, 1)),
                  pl.BlockSpec((tm, B_QK_PAD), lambda bi, i: (i, 0)),
                  pl.BlockSpec((B_QK_PAD, tm), lambda bi, i: (0, i))],
        out_specs=[pl.BlockSpec((1, B_HEADS, tm // MLA_Q_TILE, B_QK_PAD, MLA_Q_TILE), lambda bi, i: (bi, 0, i, 0, 0)),
                   pl.BlockSpec((1, B_HEADS, tm, B_QK_PAD), lambda bi, i: (bi, 0, i, 0)),
                   pl.BlockSpec((1, 1, vrows, tm), lambda bi, i: (bi, i, 0, 0)), row(B_WIDTH)],
        out_shape=[
            jax.ShapeDtypeStruct((b, B_HEADS, s // MLA_Q_TILE, B_QK_PAD, MLA_Q_TILE), BF16),
            jax.ShapeDtypeStruct((b, B_HEADS, s, B_QK_PAD), BF16),
            jax.ShapeDtypeStruct((b, s // tm, vrows, tm), BF16),
            jax.ShapeDtypeStruct((b, s, B_WIDTH), BF16),
        ],
        compiler_params=_params("parallel", "parallel"),
    )(h, g.reshape(1, d), w_in_p, q_norm.reshape(1, Q_LORA), wq_p.T, kv_norm.reshape(1, KV_LORA), wk_p,
      wvt_p, vbias, rot, rot.T)


def _prep_b_weights(w_in, w_qb, w_kvb):
    half = B_ROPE // 2
    d = w_in.shape[0]
    o1, o2, o3 = Q_LORA, Q_LORA + KV_LORA, Q_LORA + KV_LORA + B_ROPE
    kr = w_in[:, o2:o3]
    kr_swapped = jnp.concatenate([kr[:, half:], kr[:, :half]], axis=1)
    kr_block = jnp.concatenate([jnp.zeros((d, B_NOPE), w_in.dtype), kr, kr_swapped], axis=1)
    w_in_p = jnp.concatenate([w_in[:, :o2], kr_block, w_in[:, o3:]], axis=1).astype(BF16)

    wq = w_qb.reshape(Q_LORA, B_HEADS, B_NOPE + B_ROPE)
    q_rope = wq[:, :, B_NOPE:]
    q_rope_swapped = jnp.concatenate([q_rope[:, :, half:], q_rope[:, :, :half]], axis=2)
    wq_p = jnp.concatenate([wq[:, :, :B_NOPE], q_rope, q_rope_swapped], axis=2)
    wq_p = wq_p.reshape(Q_LORA, B_HEADS * B_QK_PAD).astype(BF16)

    wkv = w_kvb.reshape(KV_LORA, B_HEADS, B_NOPE + B_VDIM)
    wk_p = jnp.concatenate([wkv[:, :, :B_NOPE], jnp.zeros((KV_LORA, B_HEADS, B_QK_PAD - B_NOPE), w_kvb.dtype)],
                           axis=2).reshape(KV_LORA, B_HEADS * B_QK_PAD).astype(BF16)
    wv = jnp.concatenate([wkv[:, :, B_NOPE:], jnp.zeros((KV_LORA, B_HEADS, B_V_ROWS - B_VDIM), w_kvb.dtype)],
                         axis=2).reshape(KV_LORA, B_HEADS * B_V_ROWS)
    wvt_p = wv.T.astype(BF16)
    vbias = jnp.tile(jnp.concatenate([jnp.zeros((B_VDIM,), F32), jnp.ones((B_V_ROWS - B_VDIM,), F32)]),
                     B_HEADS).reshape(B_HEADS * B_V_ROWS, 1)
    return w_in_p, wq_p, wk_p, wvt_p, vbias


def _rope_table(s):
    half = B_ROPE // 2
    inv_freq = ROPE_THETA ** (-jnp.arange(half, dtype=F32) / half)
    ang = jnp.arange(s, dtype=F32)[:, None] * inv_freq[None, :]
    cos, sin = jnp.cos(ang), jnp.sin(ang)
    return jnp.concatenate([jnp.ones((s, B_NOPE), F32), cos, cos, -sin, sin], axis=1)


def _mla_kernel(q_ref, k_ref, vt_ref, z_ref, o_ref, s0_ref, s1_ref, p0_ref, p1_ref, *, q_tile, q_sub, kv_tile):
    kv_chunks = kv_tile // vt_ref.shape[3]
    n_kv = vt_ref.shape[1] // kv_chunks
    n_q = q_ref.shape[2] // q_sub
    s_bufs, p_bufs = (s0_ref, s1_ref), (p0_ref, p1_ref)
    chains = [(hh, sub) for sub in range(q_sub) for hh in range(B_HEAD_PAIR)]
    m_init = (jnp.full((1, q_tile), NEG_BIG, F32),) * len(chains)

    def q_rows(qt, sub):
        return pl.ds(pl.multiple_of((qt * q_sub + sub) * q_tile, q_tile), q_tile)

    def load_q(qt):
        return [q_ref[0, hh, qt * q_sub + sub] for hh, sub in chains]

    def score_chain(qc, j, slot, c):
        kc = k_ref[0, chains[c][0], j * kv_tile:(j + 1) * kv_tile, :]
        s_bufs[slot][c] = _dot(kc, qc[c])

    def scores(qc, j, slot):
        for c in range(len(chains)):
            score_chain(qc, j, slot, c)

    def exp_chain(slot, m, c):
        m_new = jnp.maximum(m[c], jnp.max(s_bufs[slot][c], axis=0, keepdims=True))
        p_bufs[slot][c] = jnp.exp2(s_bufs[slot][c] - m_new).astype(BF16)
        return m_new, jnp.exp2(m[c] - m_new)

    def exps(slot, m):
        res = [exp_chain(slot, m, c) for c in range(len(chains))]
        return tuple(r[0] for r in res), tuple(r[1] for r in res)

    def value_chain(j, slot, acc, alpha, c):
        rows = slice(chains[c][0] * B_V_ROWS, (chains[c][0] + 1) * B_V_ROWS)
        vt = jnp.concatenate([vt_ref[0, j * kv_chunks + i, rows, :] for i in range(kv_chunks)], axis=1)
        return acc[c] * alpha[c] + _dot(vt, p_bufs[slot][c])

    def step(qc, j_s, j_v, acc, alpha, m):
        s_slot, e_slot, v_slot = j_s % 2, (j_s - 1) % 2, j_v % 2
        acc_out, m_out, alpha_out = [], [], []
        for c in range(len(chains)):
            score_chain(qc, j_s % n_kv, s_slot, c)
            acc_out.append(value_chain(j_v, v_slot, acc, alpha, c))
            m_new, a_new = exp_chain(e_slot, m, c)
            m_out.append(m_new)
            alpha_out.append(a_new)
        return tuple(acc_out), tuple(m_out), tuple(alpha_out)

    def tile_body(qt, carry):
        m, alpha = carry
        qc = load_q(qt)
        qc_next = load_q(jnp.minimum(qt + 1, n_q - 1))
        acc = (jnp.zeros((B_V_ROWS, q_tile), F32),) * len(chains)
        for t in range(2, n_kv):
            acc, m, alpha = step(qc, t, t - 2, acc, alpha, m)
        acc, m, alpha = step(qc_next, n_kv, n_kv - 2, acc, alpha, m)
        acc, m_next, alpha_next = step(qc_next, n_kv + 1, n_kv - 1, acc, alpha, m_init)
        for sub in range(q_sub):
            o_t = jnp.concatenate([acc[c][:B_VDIM] / acc[c][B_VDIM:B_VDIM + 1]
                                   for c, (_, s_) in enumerate(chains) if s_ == sub], axis=0)
            z = z_ref[0, q_rows(qt, sub), :].astype(F32)
            o_ref[0, q_rows(qt, sub), :] = (o_t.T * (z * jax.nn.sigmoid(z))).astype(BF16)
        return m_next, alpha_next

    qc0 = load_q(0)
    scores(qc0, 0, 0)
    m, alpha = exps(0, m_init)
    scores(qc0, 1, 1)
    lax.fori_loop(0, n_q, tile_body, (m, alpha))


def _mla_attn(q_t, k, vt, z):
    b, _, s, _ = k.shape
    n_chunks, kv_chunk = vt.shape[1], vt.shape[3]
    kv_tile = kv_chunk * (MLA_KV_CHUNKS if n_chunks % (2 * MLA_KV_CHUNKS) == 0 else 1)
    n_kv = s // kv_tile
    assert n_kv >= 2 and n_kv % 2 == 0
    q_block = min(MLA_Q_BLOCK, s)
    q_tile = q_t.shape[-1]
    out_w = B_HEAD_PAIR * B_VDIM
    q_sub = MLA_Q_SUB if q_block % (MLA_Q_SUB * q_tile) == 0 else 1
    s_buf = pltpu.VMEM((B_HEAD_PAIR * q_sub, kv_tile, q_tile), F32)
    p_buf = pltpu.VMEM((B_HEAD_PAIR * q_sub, kv_tile, q_tile), BF16)
    return pl.pallas_call(
        functools.partial(_mla_kernel, q_tile=q_tile, q_sub=q_sub, kv_tile=kv_tile),
        grid=(b, B_HEADS // B_HEAD_PAIR, s // q_block),
        in_specs=[pl.BlockSpec((1, B_HEAD_PAIR, q_block // q_tile, B_QK_PAD, q_tile),
                               lambda bi, hp, qi: (bi, hp, qi, 0, 0)),
                  pl.BlockSpec((1, B_HEAD_PAIR, s, B_QK_PAD), lambda bi, hp, qi: (bi, hp, 0, 0)),
                  pl.BlockSpec((1, n_chunks, B_HEAD_PAIR * B_V_ROWS, kv_chunk), lambda bi, hp, qi: (bi, 0, hp, 0)),
                  pl.BlockSpec((1, q_block, out_w), lambda bi, hp, qi: (bi, qi, hp))],
        out_specs=pl.BlockSpec((1, q_block, out_w), lambda bi, hp, qi: (bi, qi, hp)),
        out_shape=jax.ShapeDtypeStruct((b, s, B_WIDTH), BF16),
        scratch_shapes=[s_buf, s_buf, p_buf, p_buf],
        compiler_params=_params("parallel", "parallel", "arbitrary"),
    )(q_t, k, vt, z)


def kernel(x, p, norm_g, a_w_in, a_sink, a_w_out, b_w_in, b_q_norm, b_w_qb, b_kv_norm, b_w_kvb, b_w_out,
           ple_w, ple_norm_g, ple_w_gate, final_norm_g):
    depth = p.shape[0]
    s = x.shape[1]
    rot = _rope_table(s)
    h = x
    for i in range(depth):
        j = i // 2
        if i % 2 == 0:
            q, k_exp, vt, z = _pre_a(h, norm_g[i], a_w_in[j])
            og = _win_attn(q, k_exp, vt, z, a_sink[j])
            w_out = a_w_out[j]
        else:
            w_in_p, wq_p, wk_p, wvt_p, vbias = _prep_b_weights(b_w_in[j], b_w_qb[j], b_w_kvb[j])
            q, k, vt, z = _pre_b(h, norm_g[i], w_in_p, b_q_norm[j], wq_p, b_kv_norm[j], wk_p, wvt_p, vbias, rot)
            og = _mla_attn(q, k, vt, z)
            w_out = b_w_out[j]
        h = _post(og, h, p, i, w_out, ple_norm_g[i], ple_w_gate[i], ple_w[i], final_norm_g, final=(i == depth - 1))
    return h
```

```python
import functools

import jax
import jax.numpy as jnp
from jax import lax
from jax.experimental import pallas as pl
from jax.experimental.pallas import tpu as pltpu

F32 = jnp.float32
BF16 = jnp.bfloat16

EPS = 1e-6
NEG_BIG = -1e30
LOG2E = 1.4426950408889634

A_HEADS = 16
A_KV_HEADS = 4
A_GROUP = A_HEADS // A_KV_HEADS
A_HEAD_DIM = 64
A_WIDTH = A_HEADS * A_HEAD_DIM
A_KV_WIDTH = A_KV_HEADS * A_HEAD_DIM
WINDOW = 128
BLOCK = 128
A_V_ROWS = 80
WIN_Q_BLOCK = 256
MASK_DIST = 1e33

B_HEADS = 16
B_NOPE = 64
B_ROPE = 32
B_VDIM = 64
B_WIDTH = B_HEADS * B_VDIM
Q_LORA = 384
KV_LORA = 256
ROPE_THETA = 10000.0
B_QK_PAD = 128
B_V_ROWS = 80
B_HEAD_PAIR = 2

ROW_TILE = 512
MLA_Q_BLOCK = 4096
MLA_Q_TILE = 256
MLA_Q_SUB = 1
MLA_KV_TILE = 512
MLA_KV_CHUNKS = 1

VMEM_LIMIT = 56 * 1024 * 1024


def _rms(x, g):
    ms = jnp.mean(x * x, axis=-1, keepdims=True)
    return x * lax.rsqrt(ms + EPS) * g


def _dot(a, b):
    return jnp.dot(a, b, preferred_element_type=F32)


def _dot_nt(a, b):
    return lax.dot_general(a, b, (((1,), (1,)), ((), ())), preferred_element_type=F32)


def _params(*sem):
    return pltpu.CompilerParams(dimension_semantics=sem, vmem_limit_bytes=VMEM_LIMIT)


def _const_spec(shape):
    nd = len(shape)
    return pl.BlockSpec(shape, lambda *_: (0,) * nd)


def _pre_a_kernel(x_ref, g_ref, w_ref, wqt_ref, wvt_ref, vb_ref, qt_ref, k_ref, vt_ref, z_ref):
    u = _rms(x_ref[0], g_ref[...]).astype(BF16)
    qs = (A_HEAD_DIM ** -0.5) * LOG2E
    o1, o2, o3 = A_WIDTH, A_WIDTH + A_KV_WIDTH, A_WIDTH + 2 * A_KV_WIDTH
    q_t = (_dot_nt(wqt_ref[...], u) * qs).astype(BF16)
    for hp in range(A_WIDTH // 128):
        for c in range(q_t.shape[1] // WIN_Q_BLOCK):
            qt_ref[0, hp, c] = q_t[hp * 128:(hp + 1) * 128, c * WIN_Q_BLOCK:(c + 1) * WIN_Q_BLOCK]
    k = _dot(u, w_ref[:, o1:o2])
    lane = lax.broadcasted_iota(jnp.int32, (k.shape[0], 2 * A_HEAD_DIM), 1)
    low = lane < A_HEAD_DIM
    for kv in range(A_KV_HEADS):
        blk = k[:, (kv // 2) * 128:(kv // 2 + 1) * 128]
        swapped = pltpu.roll(blk, A_HEAD_DIM, 1)
        first, second = (blk, swapped) if kv % 2 == 0 else (swapped, blk)
        k_ref[0, :, kv * 256:kv * 256 + 128] = jnp.where(low, first, 0.0).astype(BF16)
        k_ref[0, :, kv * 256 + 128:(kv + 1) * 256] = jnp.where(low, 0.0, second).astype(BF16)
    vt_ref[0] = (_dot_nt(wvt_ref[...], u) + vb_ref[...]).astype(BF16)
    z_ref[0] = _dot(u, w_ref[:, o3:]).astype(BF16)


def _pre_a(h, g, w_in):
    b, s, d = h.shape
    tm = min(ROW_TILE, s)
    n_in = w_in.shape[1]
    o2, o3 = A_WIDTH + A_KV_WIDTH, A_WIDTH + 2 * A_KV_WIDTH
    pad = A_V_ROWS - A_HEAD_DIM
    wv = w_in[:, o2:o3].reshape(d, A_KV_HEADS, A_HEAD_DIM)
    wvt = jnp.concatenate([wv, jnp.zeros((d, A_KV_HEADS, pad), w_in.dtype)], axis=2)
    wvt = wvt.reshape(d, A_KV_HEADS * A_V_ROWS).T.astype(BF16)
    vbias = jnp.tile(jnp.concatenate([jnp.zeros((A_HEAD_DIM,), F32), jnp.ones((pad,), F32)]),
                     A_KV_HEADS).reshape(A_KV_HEADS * A_V_ROWS, 1)
    vrows = A_KV_HEADS * A_V_ROWS
    n_pairs = A_WIDTH // 128
    assert tm % WIN_Q_BLOCK == 0
    row = lambda w: pl.BlockSpec((1, tm, w), lambda bi, i: (bi, i, 0))
    return pl.pallas_call(
        _pre_a_kernel,
        grid=(b, s // tm),
        in_specs=[row(d), _const_spec((1, d)), _const_spec((d, n_in)), _const_spec((A_WIDTH, d)),
                  _const_spec((vrows, d)), _const_spec((vrows, 1))],
        out_specs=[pl.BlockSpec((1, n_pairs, tm // WIN_Q_BLOCK, 128, WIN_Q_BLOCK), lambda bi, i: (bi, 0, i, 0, 0)),
                   row(A_KV_HEADS * 256), pl.BlockSpec((1, vrows, tm), lambda bi, i: (bi, 0, i)),
                   row(A_WIDTH)],
        out_shape=[
            jax.ShapeDtypeStruct((b, n_pairs, s // WIN_Q_BLOCK, 128, WIN_Q_BLOCK), BF16),
            jax.ShapeDtypeStruct((b, s, A_KV_HEADS * 256), BF16),
            jax.ShapeDtypeStruct((b, vrows, s), BF16),
            jax.ShapeDtypeStruct((b, s, A_WIDTH), BF16),
        ],
        compiler_params=_params("parallel", "parallel"),
    )(h, g.reshape(1, d), w_in.astype(BF16), w_in[:, :A_WIDTH].T.astype(BF16), wvt, vbias)


def _win_kernel(sink_ref, q_ref, kp_ref, kc_ref, kn_ref, vp_ref, vc_ref, vn_ref, z_ref, o_ref,
                s0_ref, s1_ref, p0_ref, p1_ref, *, seq):
    i = pl.program_id(1)
    s_bufs, p_bufs = (s0_ref, s1_ref), (p0_ref, p1_ref)
    n_keys = WIN_Q_BLOCK + 2 * BLOCK
    n_live = 3 * BLOCK
    halves = range(WIN_Q_BLOCK // BLOCK)
    live = [(slice(j * BLOCK, j * BLOCK + n_live), slice(j * BLOCK, (j + 1) * BLOCK)) for j in halves]
    r = lax.broadcasted_iota(jnp.int32, (n_live, BLOCK), 0)
    c = lax.broadcasted_iota(jnp.int32, (n_live, BLOCK), 1)
    rel = r - BLOCK - c
    neg_dist = []
    for j in halves:
        k_pos = i * WIN_Q_BLOCK + (j - 1) * BLOCK + r
        valid = (jnp.abs(rel) <= WINDOW) & (k_pos >= 0) & (k_pos < seq)
        neg_dist.append(jnp.where(valid, -jnp.abs(rel).astype(F32), -MASK_DIST))
    for p_buf in p_bufs:
        for rows, lanes in live:
            dead = slice(0, BLOCK) if rows.start > 0 else slice(n_live, n_keys)
            p_buf[dead, lanes] = jnp.zeros((BLOCK, BLOCK), BF16)

    def scores(h, slot):
        kv, hp = h // A_GROUP, h // 2
        cols = slice(kv * 256 + (h % 2) * 128, kv * 256 + (h % 2 + 1) * 128)
        k_sel = jnp.concatenate([kp_ref[0, :, cols], kc_ref[0, :, cols], kn_ref[0, :, cols]], axis=0)
        slope = 2.0 ** (-8.0 * (h + 1) / A_HEADS) * LOG2E
        s = _dot(k_sel, q_ref[0, hp, 0])
        tmax = []
        for j, (rows, lanes) in enumerate(live):
            part = s[rows, lanes] + slope * neg_dist[j]
            s_bufs[slot][rows, lanes] = part
            tmax.append(jnp.max(part, axis=0, keepdims=True))
        return jnp.concatenate(tmax, axis=1)

    def exps(h, slot, tmax):
        m = jnp.maximum(tmax, sink_ref[h] * LOG2E)
        for rows, lanes in live:
            p_bufs[slot][rows, lanes] = jnp.exp2(s_bufs[slot][rows, lanes] - m[:, lanes]).astype(BF16)
        return m

    def values(h, slot, m):
        kv = h // A_GROUP
        rows = slice(kv * A_V_ROWS, (kv + 1) * A_V_ROWS)
        vt = jnp.concatenate([vp_ref[0, rows, :], vc_ref[0, rows, :], vn_ref[0, rows, :]], axis=1)
        acc = _dot(vt, p_bufs[slot][...])
        denom = acc[A_HEAD_DIM:A_HEAD_DIM + 1] + jnp.exp2(sink_ref[h] * LOG2E - m)
        return acc[:A_HEAD_DIM] / denom

    tmax, m, pending = {}, {}, {}
    for t in range(A_HEADS + 2):
        if t < A_HEADS:
            tmax[t] = scores(t, t % 2)
        if t >= 2:
            h = t - 2
            pending[h] = values(h, h % 2, m.pop(h))
            if h % 2 == 1:
                o_t = jnp.concatenate([pending.pop(h - 1), pending.pop(h)], axis=0)
                lanes = slice((h // 2) * 128, (h // 2 + 1) * 128)
                z = z_ref[0, :, lanes].astype(F32)
                o_ref[0, :, lanes] = (o_t.T * (z * jax.nn.sigmoid(z))).astype(BF16)
        if 1 <= t <= A_HEADS:
            m[t - 1] = exps(t - 1, (t - 1) % 2, tmax.pop(t - 1))


def _win_attn(q_t, k_exp, vt, z, sink):
    b, s, _ = k_exp.shape
    qb = WIN_Q_BLOCK
    assert s % qb == 0
    per = qb // BLOCK
    nb = s // BLOCK
    vrows = vt.shape[1]
    kw = k_exp.shape[-1]
    prev_i = lambda i: jnp.maximum(per * i - 1, 0)
    next_i = lambda i: jnp.minimum(per * i + per, nb - 1)
    cur = lambda w: pl.BlockSpec((1, qb, w), lambda bi, i: (bi, i, 0))
    s_buf = pltpu.VMEM((qb + 2 * BLOCK, qb), F32)
    p_buf = pltpu.VMEM((qb + 2 * BLOCK, qb), BF16)
    return pl.pallas_call(
        functools.partial(_win_kernel, seq=s),
        grid=(b, s // qb),
        in_specs=[pl.BlockSpec(memory_space=pltpu.SMEM),
                  pl.BlockSpec((1, q_t.shape[1], 1, 128, qb), lambda bi, i: (bi, 0, i, 0, 0)),
                  pl.BlockSpec((1, BLOCK, kw), lambda bi, i: (bi, prev_i(i), 0)),
                  cur(kw),
                  pl.BlockSpec((1, BLOCK, kw), lambda bi, i: (bi, next_i(i), 0)),
                  pl.BlockSpec((1, vrows, BLOCK), lambda bi, i: (bi, 0, prev_i(i))),
                  pl.BlockSpec((1, vrows, qb), lambda bi, i: (bi, 0, i)),
                  pl.BlockSpec((1, vrows, BLOCK), lambda bi, i: (bi, 0, next_i(i))),
                  cur(A_WIDTH)],
        out_specs=cur(A_WIDTH),
        out_shape=jax.ShapeDtypeStruct((b, s, A_WIDTH), BF16),
        scratch_shapes=[s_buf, s_buf, p_buf, p_buf],
        compiler_params=_params("parallel", "parallel"),
    )(sink.astype(F32), q_t, k_exp, k_exp, k_exp, vt, vt, vt, z)


def _post_kernel(og_ref, h_ref, p_ref, wo_ref, pg_ref, wg_ref, pw_ref, fg_ref, o_ref, *, final):
    h1 = h_ref[0] + _dot(og_ref[0], wo_ref[...])
    gate = jax.nn.sigmoid(_dot(_rms(h1, pg_ref[...]).astype(BF16), wg_ref[...]))
    h2 = h1 + _dot(p_ref[0, 0].astype(BF16), pw_ref[...]) * gate
    if final:
        h2 = _rms(h2, fg_ref[...])
    o_ref[0] = h2


def _post(og, h, p, layer, w_out, ple_g, w_gate, ple_w, final_g, final):
    b, s, d = h.shape
    tm = min(ROW_TILE, s)
    pd = p.shape[-1]
    row = lambda w: pl.BlockSpec((1, tm, w), lambda bi, i: (bi, i, 0))
    return pl.pallas_call(
        functools.partial(_post_kernel, final=final),
        grid=(b, s // tm),
        in_specs=[row(og.shape[-1]), row(d), pl.BlockSpec((1, 1, tm, pd), lambda bi, i: (layer, bi, i, 0)),
                  _const_spec(w_out.shape), _const_spec((1, d)),
                  _const_spec((d, d)), _const_spec((pd, d)), _const_spec((1, d))],
        out_specs=row(d),
        out_shape=jax.ShapeDtypeStruct((b, s, d), F32),
        compiler_params=_params("parallel", "parallel"),
    )(og, h, p, w_out.astype(BF16), ple_g.reshape(1, d), w_gate.astype(BF16), ple_w.astype(BF16),
      final_g.reshape(1, d))


def _pre_b_kernel(x_ref, g_ref, w_ref, qn_ref, wqt_ref, kn_ref, wk_ref, wvt_ref, vb_ref, r_ref, rt_ref,
                  qt_ref, k_ref, vt_ref, z_ref):
    u = _rms(x_ref[0], g_ref[...]).astype(BF16)
    o1, o2, o3 = Q_LORA, Q_LORA + KV_LORA, Q_LORA + KV_LORA + B_QK_PAD
    rot = r_ref[...]
    cq = _rms(_dot(u, w_ref[:, :o1]), qn_ref[...]).astype(BF16)
    qs = ((B_NOPE + B_ROPE) ** -0.5) * LOG2E
    q_t = (_dot_nt(wqt_ref[...], cq) * jnp.tile(rt_ref[...] * qs, (B_HEADS, 1))).astype(BF16)
    for hh in range(B_HEADS):
        for c in range(q_t.shape[1] // MLA_Q_TILE):
            qt_ref[0, hh, c] = q_t[hh * B_QK_PAD:(hh + 1) * B_QK_PAD, c * MLA_Q_TILE:(c + 1) * MLA_Q_TILE]
    ckv = _rms(_dot(u, w_ref[:, o1:o2]), kn_ref[...]).astype(BF16)
    y = _dot(u, w_ref[:, o2:o3]) * rot
    lane = lax.broadcasted_iota(jnp.int32, y.shape, 1)
    both = pltpu.roll(y, 32, 1) + pltpu.roll(y, 96, 1)
    kr = y + jnp.where(lane >= B_NOPE, both, 0.0)
    k = (_dot(ckv, wk_ref[...]) + jnp.tile(kr, (1, B_HEADS))).astype(BF16)
    for hh in range(B_HEADS):
        k_ref[0, hh] = k[:, hh * B_QK_PAD:(hh + 1) * B_QK_PAD]
    vt_ref[0, 0] = (_dot_nt(wvt_ref[...], ckv) + vb_ref[...]).astype(BF16)
    z_ref[0] = _dot(u, w_ref[:, o3:]).astype(BF16)


def _pre_b(h, g, w_in_p, q_norm, wq_p, kv_norm, wk_p, wvt_p, vbias, rot):
    b, s, d = h.shape
    tm = min(MLA_KV_TILE, s)
    assert tm % MLA_Q_TILE == 0
    row = lambda w: pl.BlockSpec((1, tm, w), lambda bi, i: (bi, i, 0))
    vrows = B_HEADS * B_V_ROWS
    return pl.pallas_call(
        _pre_b_kernel,
        grid=(b, s // tm),
        in_specs=[row(d), _const_spec((1, d)), _const_spec(w_in_p.shape), _const_spec((1, Q_LORA)),
                  _const_spec((wq_p.shape[1], wq_p.shape[0])), _const_spec((1, KV_LORA)), _const_spec(wk_p.shape),
                  _const_spec(wvt_p.shape), _const_spec((vrows, 1)),
                  pl.BlockSpec((tm, B_QK_PAD), lambda bi, i: (i, 0)),
                  pl.BlockSpec((B_QK_PAD, tm), lambda bi, i: (0, i))],
        out_specs=[pl.BlockSpec((1, B_HEADS, tm // MLA_Q_TILE, B_QK_PAD, MLA_Q_TILE), lambda bi, i: (bi, 0, i, 0, 0)),
                   pl.BlockSpec((1, B_HEADS, tm, B_QK_PAD), lambda bi, i: (bi, 0, i, 0)),
                   pl.BlockSpec((1, 1, vrows, tm), lambda bi, i: (bi, i, 0, 0)), row(B_WIDTH)],
        out_shape=[
            jax.ShapeDtypeStruct((b, B_HEADS, s // MLA_Q_TILE, B_QK_PAD, MLA_Q_TILE), BF16),
            jax.ShapeDtypeStruct((b, B_HEADS, s, B_QK_PAD), BF16),
            jax.ShapeDtypeStruct((b, s // tm, vrows, tm), BF16),
            jax.ShapeDtypeStruct((b, s, B_WIDTH), BF16),
        ],
        compiler_params=_params("parallel", "parallel"),
    )(h, g.reshape(1, d), w_in_p, q_norm.reshape(1, Q_LORA), wq_p.T, kv_norm.reshape(1, KV_LORA), wk_p,
      wvt_p, vbias, rot, rot.T)


def _prep_b_weights(w_in, w_qb, w_kvb):
    half = B_ROPE // 2
    d = w_in.shape[0]
    o1, o2, o3 = Q_LORA, Q_LORA + KV_LORA, Q_LORA + KV_LORA + B_ROPE
    kr = w_in[:, o2:o3]
    kr_swapped = jnp.concatenate([kr[:, half:], kr[:, :half]], axis=1)
    kr_block = jnp.concatenate([jnp.zeros((d, B_NOPE), w_in.dtype), kr, kr_swapped], axis=1)
    w_in_p = jnp.concatenate([w_in[:, :o2], kr_block, w_in[:, o3:]], axis=1).astype(BF16)

    wq = w_qb.reshape(Q_LORA, B_HEADS, B_NOPE + B_ROPE)
    q_rope = wq[:, :, B_NOPE:]
    q_rope_swapped = jnp.concatenate([q_rope[:, :, half:], q_rope[:, :, :half]], axis=2)
    wq_p = jnp.concatenate([wq[:, :, :B_NOPE], q_rope, q_rope_swapped], axis=2)
    wq_p = wq_p.reshape(Q_LORA, B_HEADS * B_QK_PAD).astype(BF16)

    wkv = w_kvb.reshape(KV_LORA, B_HEADS, B_NOPE + B_VDIM)
    wk_p = jnp.concatenate([wkv[:, :, :B_NOPE], jnp.zeros((KV_LORA, B_HEADS, B_QK_PAD - B_NOPE), w_kvb.dtype)],
                           axis=2).reshape(KV_LORA, B_HEADS * B_QK_PAD).astype(BF16)
    wv = jnp.concatenate([wkv[:, :, B_NOPE:], jnp.zeros((KV_LORA, B_HEADS, B_V_ROWS - B_VDIM), w_kvb.dtype)],
                         axis=2).reshape(KV_LORA, B_HEADS * B_V_ROWS)
    wvt_p = wv.T.astype(BF16)
    vbias = jnp.tile(jnp.concatenate([jnp.zeros((B_VDIM,), F32), jnp.ones((B_V_ROWS - B_VDIM,), F32)]),
                     B_HEADS).reshape(B_HEADS * B_V_ROWS, 1)
    return w_in_p, wq_p, wk_p, wvt_p, vbias


def _rope_table(s):
    half = B_ROPE // 2
    inv_freq = ROPE_THETA ** (-jnp.arange(half, dtype=F32) / half)
    ang = jnp.arange(s, dtype=F32)[:, None] * inv_freq[None, :]
    cos, sin = jnp.cos(ang), jnp.sin(ang)
    return jnp.concatenate([jnp.ones((s, B_NOPE), F32), cos, cos, -sin, sin], axis=1)


def _mla_kernel(q_ref, k_ref, vt_ref, z_ref, o_ref, s0_ref, s1_ref, p0_ref, p1_ref, *, q_tile, q_sub, kv_tile):
    kv_chunks = kv_tile // vt_ref.shape[3]
    n_kv = vt_ref.shape[1] // kv_chunks
    n_q = q_ref.shape[2] // q_sub
    s_bufs, p_bufs = (s0_ref, s1_ref), (p0_ref, p1_ref)
    chains = [(hh, sub) for sub in range(q_sub) for hh in range(B_HEAD_PAIR)]
    m_init = (jnp.full((1, q_tile), NEG_BIG, F32),) * len(chains)

    def q_rows(qt, sub):
        return pl.ds(pl.multiple_of((qt * q_sub + sub) * q_tile, q_tile), q_tile)

    def load_q(qt):
        return [q_ref[0, hh, qt * q_sub + sub] for hh, sub in chains]

    def score_chain(qc, j, slot, c):
        kc = k_ref[0, chains[c][0], j * kv_tile:(j + 1) * kv_tile, :]
        s_bufs[slot][c] = _dot(kc, qc[c])

    def scores(qc, j, slot):
        for c in range(len(chains)):
            score_chain(qc, j, slot, c)

    def exp_chain(slot, m, c):
        m_new = jnp.maximum(m[c], jnp.max(s_bufs[slot][c], axis=0, keepdims=True))
        p_bufs[slot][c] = jnp.exp2(s_bufs[slot][c] - m_new).astype(BF16)
        return m_new, jnp.exp2(m[c] - m_new)

    def exps(slot, m):
        res = [exp_chain(slot, m, c) for c in range(len(chains))]
        return tuple(r[0] for r in res), tuple(r[1] for r in res)

    def value_chain(j, slot, acc, alpha, c):
        rows = slice(chains[c][0] * B_V_ROWS, (chains[c][0] + 1) * B_V_ROWS)
        vt = jnp.concatenate([vt_ref[0, j * kv_chunks + i, rows, :] for i in range(kv_chunks)], axis=1)
        return acc[c] * alpha[c] + _dot(vt, p_bufs[slot][c])

    def step(qc, j_s, j_v, acc, alpha, m):
        s_slot, e_slot, v_slot = j_s % 2, (j_s - 1) % 2, j_v % 2
        acc_out, m_out, alpha_out = [], [], []
        for c in range(len(chains)):
            score_chain(qc, j_s % n_kv, s_slot, c)
            acc_out.append(value_chain(j_v, v_slot, acc, alpha, c))
            m_new, a_new = exp_chain(e_slot, m, c)
            m_out.append(m_new)
            alpha_out.append(a_new)
        return tuple(acc_out), tuple(m_out), tuple(alpha_out)

    def tile_body(qt, carry):
        m, alpha = carry
        qc = load_q(qt)
        qc_next = load_q(jnp.minimum(qt + 1, n_q - 1))
        acc = (jnp.zeros((B_V_ROWS, q_tile), F32),) * len(chains)
        for t in range(2, n_kv):
            acc, m, alpha = step(qc, t, t - 2, acc, alpha, m)
        acc, m, alpha = step(qc_next, n_kv, n_kv - 2, acc, alpha, m)
        acc, m_next, alpha_next = step(qc_next, n_kv + 1, n_kv - 1, acc, alpha, m_init)
        for sub in range(q_sub):
            o_t = jnp.concatenate([acc[c][:B_VDIM] / acc[c][B_VDIM:B_VDIM + 1]
                                   for c, (_, s_) in enumerate(chains) if s_ == sub], axis=0)
            z = z_ref[0, q_rows(qt, sub), :].astype(F32)
            o_ref[0, q_rows(qt, sub), :] = (o_t.T * (z * jax.nn.sigmoid(z))).astype(BF16)
        return m_next, alpha_next

    qc0 = load_q(0)
    scores(qc0, 0, 0)
    m, alpha = exps(0, m_init)
    scores(qc0, 1, 1)
    lax.fori_loop(0, n_q, tile_body, (m, alpha))


def _mla_attn(q_t, k, vt, z):
    b, _, s, _ = k.shape
    n_chunks, kv_chunk = vt.shape[1], vt.shape[3]
    kv_tile = kv_chunk * (MLA_KV_CHUNKS if n_chunks % (2 * MLA_KV_CHUNKS) == 0 else 1)
    n_kv = s // kv_tile
    assert n_kv >= 2 and n_kv % 2 == 0
    q_block = min(MLA_Q_BLOCK, s)
    q_tile = q_t.shape[-1]
    out_w = B_HEAD_PAIR * B_VDIM
    q_sub = MLA_Q_SUB if q_block % (MLA_Q_SUB * q_tile) == 0 else 1
    s_buf = pltpu.VMEM((B_HEAD_PAIR * q_sub, kv_tile, q_tile), F32)
    p_buf = pltpu.VMEM((B_HEAD_PAIR * q_sub, kv_tile, q_tile), BF16)
    return pl.pallas_call(
        functools.partial(_mla_kernel, q_tile=q_tile, q_sub=q_sub, kv_tile=kv_tile),
        grid=(b, B_HEADS // B_HEAD_PAIR, s // q_block),
        in_specs=[pl.BlockSpec((1, B_HEAD_PAIR, q_block // q_tile, B_QK_PAD, q_tile),
                               lambda bi, hp, qi: (bi, hp, qi, 0, 0)),
                  pl.BlockSpec((1, B_HEAD_PAIR, s, B_QK_PAD), lambda bi, hp, qi: (bi, hp, 0, 0)),
                  pl.BlockSpec((1, n_chunks, B_HEAD_PAIR * B_V_ROWS, kv_chunk), lambda bi, hp, qi: (bi, 0, hp, 0)),
                  pl.BlockSpec((1, q_block, out_w), lambda bi, hp, qi: (bi, qi, hp))],
        out_specs=pl.BlockSpec((1, q_block, out_w), lambda bi, hp, qi: (bi, qi, hp)),
        out_shape=jax.ShapeDtypeStruct((b, s, B_WIDTH), BF16),
        scratch_shapes=[s_buf, s_buf, p_buf, p_buf],
        compiler_params=_params("parallel", "parallel", "arbitrary"),
    )(q_t, k, vt, z)


def kernel(x, p, norm_g, a_w_in, a_sink, a_w_out, b_w_in, b_q_norm, b_w_qb, b_kv_norm, b_w_kvb, b_w_out,
           ple_w, ple_norm_g, ple_w_gate, final_norm_g):
    depth = p.shape[0]
    s = x.shape[1]
    rot = _rope_table(s)
    h = x
    for i in range(depth):
        j = i // 2
        if i % 2 == 0:
            q, k_exp, vt, z = _pre_a(h, norm_g[i], a_w_in[j])
            og = _win_attn(q, k_exp, vt, z, a_sink[j])
            w_out = a_w_out[j]
        else:
            w_in_p, wq_p, wk_p, wvt_p, vbias = _prep_b_weights(b_w_in[j], b_w_qb[j], b_w_kvb[j])
            q, k, vt, z = _pre_b(h, norm_g[i], w_in_p, b_q_norm[j], wq_p, b_kv_norm[j], wk_p, wvt_p, vbias, rot)
            og = _mla_attn(q, k, vt, z)
            w_out = b_w_out[j]
        h = _post(og, h, p, i, w_out, ple_norm_g[i], ple_w_gate[i], ple_w[i], final_norm_g, final=(i == depth - 1))
    return h
```

```python
import functools

import jax
import jax.numpy as jnp
from jax import lax
from jax.experimental import pallas as pl
from jax.experimental.pallas import tpu as pltpu

F32 = jnp.float32
BF16 = jnp.bfloat16

EPS = 1e-6
NEG_BIG = -1e30
LOG2E = 1.4426950408889634

A_HEADS = 16
A_KV_HEADS = 4
A_GROUP = A_HEADS // A_KV_HEADS
A_HEAD_DIM = 64
A_WIDTH = A_HEADS * A_HEAD_DIM
A_KV_WIDTH = A_KV_HEADS * A_HEAD_DIM
WINDOW = 128
BLOCK = 128
A_V_ROWS = 80
WIN_Q_BLOCK = 256
MASK_DIST = 1e33

B_HEADS = 16
B_NOPE = 64
B_ROPE = 32
B_VDIM = 64
B_WIDTH = B_HEADS * B_VDIM
Q_LORA = 384
KV_LORA = 256
ROPE_THETA = 10000.0
B_QK_PAD = 128
B_V_ROWS = 80
B_HEAD_PAIR = 2

ROW_TILE = 1024
MLA_Q_BLOCK = 4096
MLA_Q_TILE = 256
MLA_Q_SUB = 1
PRE_B_TILE = 1024
MLA_KV_TILE = 512
MLA_KV_CHUNKS = 1

VMEM_LIMIT = 56 * 1024 * 1024


def _rms(x, g):
    ms = jnp.mean(x * x, axis=-1, keepdims=True)
    return x * lax.rsqrt(ms + EPS) * g


def _dot(a, b):
    return jnp.dot(a, b, preferred_element_type=F32)


def _dot_nt(a, b):
    return lax.dot_general(a, b, (((1,), (1,)), ((), ())), preferred_element_type=F32)


def _params(*sem):
    return pltpu.CompilerParams(dimension_semantics=sem, vmem_limit_bytes=VMEM_LIMIT)


def _const_spec(shape):
    nd = len(shape)
    return pl.BlockSpec(shape, lambda *_: (0,) * nd)


def _pre_a_kernel(x_ref, g_ref, w_ref, wqt_ref, wvt_ref, vb_ref, qt_ref, k_ref, vt_ref, z_ref):
    u = _rms(x_ref[0], g_ref[...]).astype(BF16)
    qs = (A_HEAD_DIM ** -0.5) * LOG2E
    o1, o2, o3 = A_WIDTH, A_WIDTH + A_KV_WIDTH, A_WIDTH + 2 * A_KV_WIDTH
    q_t = (_dot_nt(wqt_ref[...], u) * qs).astype(BF16)
    for hp in range(A_WIDTH // 128):
        for c in range(q_t.shape[1] // WIN_Q_BLOCK):
            qt_ref[0, hp, c] = q_t[hp * 128:(hp + 1) * 128, c * WIN_Q_BLOCK:(c + 1) * WIN_Q_BLOCK]
    k = _dot(u, w_ref[:, o1:o2])
    lane = lax.broadcasted_iota(jnp.int32, (k.shape[0], 2 * A_HEAD_DIM), 1)
    low = lane < A_HEAD_DIM
    for kv in range(A_KV_HEADS):
        blk = k[:, (kv // 2) * 128:(kv // 2 + 1) * 128]
        swapped = pltpu.roll(blk, A_HEAD_DIM, 1)
        first, second = (blk, swapped) if kv % 2 == 0 else (swapped, blk)
        k_ref[0, :, kv * 256:kv * 256 + 128] = jnp.where(low, first, 0.0).astype(BF16)
        k_ref[0, :, kv * 256 + 128:(kv + 1) * 256] = jnp.where(low, 0.0, second).astype(BF16)
    vt_ref[0] = (_dot_nt(wvt_ref[...], u) + vb_ref[...]).astype(BF16)
    z_ref[0] = _dot(u, w_ref[:, o3:]).astype(BF16)


def _pre_a(h, g, w_in):
    b, s, d = h.shape
    tm = min(ROW_TILE, s)
    n_in = w_in.shape[1]
    o2, o3 = A_WIDTH + A_KV_WIDTH, A_WIDTH + 2 * A_KV_WIDTH
    pad = A_V_ROWS - A_HEAD_DIM
    wv = w_in[:, o2:o3].reshape(d, A_KV_HEADS, A_HEAD_DIM)
    wvt = jnp.concatenate([wv, jnp.zeros((d, A_KV_HEADS, pad), w_in.dtype)], axis=2)
    wvt = wvt.reshape(d, A_KV_HEADS * A_V_ROWS).T.astype(BF16)
    vbias = jnp.tile(jnp.concatenate([jnp.zeros((A_HEAD_DIM,), F32), jnp.ones((pad,), F32)]),
                     A_KV_HEADS).reshape(A_KV_HEADS * A_V_ROWS, 1)
    vrows = A_KV_HEADS * A_V_ROWS
    n_pairs = A_WIDTH // 128
    assert tm % WIN_Q_BLOCK == 0
    row = lambda w: pl.BlockSpec((1, tm, w), lambda bi, i: (bi, i, 0))
    return pl.pallas_call(
        _pre_a_kernel,
        grid=(b, s // tm),
        in_specs=[row(d), _const_spec((1, d)), _const_spec((d, n_in)), _const_spec((A_WIDTH, d)),
                  _const_spec((vrows, d)), _const_spec((vrows, 1))],
        out_specs=[pl.BlockSpec((1, n_pairs, tm // WIN_Q_BLOCK, 128, WIN_Q_BLOCK), lambda bi, i: (bi, 0, i, 0, 0)),
                   row(A_KV_HEADS * 256), pl.BlockSpec((1, vrows, tm), lambda bi, i: (bi, 0, i)),
                   row(A_WIDTH)],
        out_shape=[
            jax.ShapeDtypeStruct((b, n_pairs, s // WIN_Q_BLOCK, 128, WIN_Q_BLOCK), BF16),
            jax.ShapeDtypeStruct((b, s, A_KV_HEADS * 256), BF16),
            jax.ShapeDtypeStruct((b, vrows, s), BF16),
            jax.ShapeDtypeStruct((b, s, A_WIDTH), BF16),
        ],
        compiler_params=_params("parallel", "parallel"),
    )(h, g.reshape(1, d), w_in.astype(BF16), w_in[:, :A_WIDTH].T.astype(BF16), wvt, vbias)


def _win_kernel(sink_ref, q_ref, kp_ref, kc_ref, kn_ref, vp_ref, vc_ref, vn_ref, z_ref, o_ref,
                s0_ref, s1_ref, p0_ref, p1_ref, *, seq):
    i = pl.program_id(1)
    s_bufs, p_bufs = (s0_ref, s1_ref), (p0_ref, p1_ref)
    n_keys = WIN_Q_BLOCK + 2 * BLOCK
    n_live = 3 * BLOCK
    halves = range(WIN_Q_BLOCK // BLOCK)
    live = [(slice(j * BLOCK, j * BLOCK + n_live), slice(j * BLOCK, (j + 1) * BLOCK)) for j in halves]
    r = lax.broadcasted_iota(jnp.int32, (n_live, BLOCK), 0)
    c = lax.broadcasted_iota(jnp.int32, (n_live, BLOCK), 1)
    rel = r - BLOCK - c
    neg_dist = []
    for j in halves:
        k_pos = i * WIN_Q_BLOCK + (j - 1) * BLOCK + r
        valid = (jnp.abs(rel) <= WINDOW) & (k_pos >= 0) & (k_pos < seq)
        neg_dist.append(jnp.where(valid, -jnp.abs(rel).astype(F32), -MASK_DIST))
    for p_buf in p_bufs:
        for rows, lanes in live:
            dead = slice(0, BLOCK) if rows.start > 0 else slice(n_live, n_keys)
            p_buf[dead, lanes] = jnp.zeros((BLOCK, BLOCK), BF16)

    def scores(h, slot):
        kv, hp = h // A_GROUP, h // 2
        cols = slice(kv * 256 + (h % 2) * 128, kv * 256 + (h % 2 + 1) * 128)
        k_sel = jnp.concatenate([kp_ref[0, :, cols], kc_ref[0, :, cols], kn_ref[0, :, cols]], axis=0)
        slope = 2.0 ** (-8.0 * (h + 1) / A_HEADS) * LOG2E
        s = _dot(k_sel, q_ref[0, hp, 0])
        tmax = []
        for j, (rows, lanes) in enumerate(live):
            part = s[rows, lanes] + slope * neg_dist[j]
            s_bufs[slot][rows, lanes] = part
            tmax.append(jnp.max(part, axis=0, keepdims=True))
        return jnp.concatenate(tmax, axis=1)

    def exps(h, slot, tmax):
        m = jnp.maximum(tmax, sink_ref[h] * LOG2E)
        for rows, lanes in live:
            p_bufs[slot][rows, lanes] = jnp.exp2(s_bufs[slot][rows, lanes] - m[:, lanes]).astype(BF16)
        return m

    def values(h, slot, m):
        kv = h // A_GROUP
        rows = slice(kv * A_V_ROWS, (kv + 1) * A_V_ROWS)
        vt = jnp.concatenate([vp_ref[0, rows, :], vc_ref[0, rows, :], vn_ref[0, rows, :]], axis=1)
        acc = _dot(vt, p_bufs[slot][...])
        denom = acc[A_HEAD_DIM:A_HEAD_DIM + 1] + jnp.exp2(sink_ref[h] * LOG2E - m)
        return acc[:A_HEAD_DIM] / denom

    tmax, m, pending = {}, {}, {}
    for t in range(A_HEADS + 2):
        if t < A_HEADS:
            tmax[t] = scores(t, t % 2)
        if t >= 2:
            h = t - 2
            pending[h] = values(h, h % 2, m.pop(h))
            if h % 2 == 1:
                o_t = jnp.concatenate([pending.pop(h - 1), pending.pop(h)], axis=0)
                lanes = slice((h // 2) * 128, (h // 2 + 1) * 128)
                z = z_ref[0, :, lanes].astype(F32)
                o_ref[0, :, lanes] = (o_t.T * (z * jax.nn.sigmoid(z))).astype(BF16)
        if 1 <= t <= A_HEADS:
            m[t - 1] = exps(t - 1, (t - 1) % 2, tmax.pop(t - 1))


def _win_attn(q_t, k_exp, vt, z, sink):
    b, s, _ = k_exp.shape
    qb = WIN_Q_BLOCK
    assert s % qb == 0
    per = qb // BLOCK
    nb = s // BLOCK
    vrows = vt.shape[1]
    kw = k_exp.shape[-1]
    prev_i = lambda i: jnp.maximum(per * i - 1, 0)
    next_i = lambda i: jnp.minimum(per * i + per, nb - 1)
    cur = lambda w: pl.BlockSpec((1, qb, w), lambda bi, i: (bi, i, 0))
    s_buf = pltpu.VMEM((qb + 2 * BLOCK, qb), F32)
    p_buf = pltpu.VMEM((qb + 2 * BLOCK, qb), BF16)
    return pl.pallas_call(
        functools.partial(_win_kernel, seq=s),
        grid=(b, s // qb),
        in_specs=[pl.BlockSpec(memory_space=pltpu.SMEM),
                  pl.BlockSpec((1, q_t.shape[1], 1, 128, qb), lambda bi, i: (bi, 0, i, 0, 0)),
                  pl.BlockSpec((1, BLOCK, kw), lambda bi, i: (bi, prev_i(i), 0)),
                  cur(kw),
                  pl.BlockSpec((1, BLOCK, kw), lambda bi, i: (bi, next_i(i), 0)),
                  pl.BlockSpec((1, vrows, BLOCK), lambda bi, i: (bi, 0, prev_i(i))),
                  pl.BlockSpec((1, vrows, qb), lambda bi, i: (bi, 0, i)),
                  pl.BlockSpec((1, vrows, BLOCK), lambda bi, i: (bi, 0, next_i(i))),
                  cur(A_WIDTH)],
        out_specs=cur(A_WIDTH),
        out_shape=jax.ShapeDtypeStruct((b, s, A_WIDTH), BF16),
        scratch_shapes=[s_buf, s_buf, p_buf, p_buf],
        compiler_params=_params("parallel", "parallel"),
    )(sink.astype(F32), q_t, k_exp, k_exp, k_exp, vt, vt, vt, z)


def _post_kernel(og_ref, h_ref, p_ref, wo_ref, pg_ref, wg_ref, pw_ref, fg_ref, o_ref, *, final):
    h1 = h_ref[0] + _dot(og_ref[0], wo_ref[...])
    gate = jax.nn.sigmoid(_dot(_rms(h1, pg_ref[...]).astype(BF16), wg_ref[...]))
    h2 = h1 + _dot(p_ref[0, 0].astype(BF16), pw_ref[...]) * gate
    if final:
        h2 = _rms(h2, fg_ref[...])
    o_ref[0] = h2


def _post(og, h, p, layer, w_out, ple_g, w_gate, ple_w, final_g, final):
    b, s, d = h.shape
    tm = min(ROW_TILE, s)
    pd = p.shape[-1]
    row = lambda w: pl.BlockSpec((1, tm, w), lambda bi, i: (bi, i, 0))
    return pl.pallas_call(
        functools.partial(_post_kernel, final=final),
        grid=(b, s // tm),
        in_specs=[row(og.shape[-1]), row(d), pl.BlockSpec((1, 1, tm, pd), lambda bi, i: (layer, bi, i, 0)),
                  _const_spec(w_out.shape), _const_spec((1, d)),
                  _const_spec((d, d)), _const_spec((pd, d)), _const_spec((1, d))],
        out_specs=row(d),
        out_shape=jax.ShapeDtypeStruct((b, s, d), F32),
        compiler_params=_params("parallel", "parallel"),
    )(og, h, p, w_out.astype(BF16), ple_g.reshape(1, d), w_gate.astype(BF16), ple_w.astype(BF16),
      final_g.reshape(1, d))


def _pre_b_kernel(x_ref, g_ref, w_ref, qn_ref, wqt_ref, kn_ref, wk_ref, wvt_ref, vb_ref, r_ref, rt_ref,
                  qt_ref, k_ref, vt_ref, z_ref):
    u = _rms(x_ref[0], g_ref[...]).astype(BF16)
    o1, o2, o3 = Q_LORA, Q_LORA + KV_LORA, Q_LORA + KV_LORA + B_QK_PAD
    rot = r_ref[...]
    cq = _rms(_dot(u, w_ref[:, :o1]), qn_ref[...]).astype(BF16)
    qs = ((B_NOPE + B_ROPE) ** -0.5) * LOG2E
    q_t = (_dot_nt(wqt_ref[...], cq) * jnp.tile(rt_ref[...] * qs, (B_HEADS, 1))).astype(BF16)
    for hh in range(B_HEADS):
        for c in range(q_t.shape[1] // MLA_Q_TILE):
            qt_ref[0, hh, c] = q_t[hh * B_QK_PAD:(hh + 1) * B_QK_PAD, c * MLA_Q_TILE:(c + 1) * MLA_Q_TILE]
    ckv = _rms(_dot(u, w_ref[:, o1:o2]), kn_ref[...]).astype(BF16)
    y = _dot(u, w_ref[:, o2:o3]) * rot
    lane = lax.broadcasted_iota(jnp.int32, y.shape, 1)
    both = pltpu.roll(y, 32, 1) + pltpu.roll(y, 96, 1)
    kr = y + jnp.where(lane >= B_NOPE, both, 0.0)
    k = (_dot(ckv, wk_ref[...]) + jnp.tile(kr, (1, B_HEADS))).astype(BF16)
    for hh in range(B_HEADS):
        k_ref[0, hh] = k[:, hh * B_QK_PAD:(hh + 1) * B_QK_PAD]
    vt_all = (_dot_nt(wvt_ref[...], ckv) + vb_ref[...]).astype(BF16)
    for c in range(vt_ref.shape[1]):
        vt_ref[0, c] = vt_all[:, c * MLA_KV_TILE:(c + 1) * MLA_KV_TILE]
    z_ref[0] = _dot(u, w_ref[:, o3:]).astype(BF16)


def _pre_b(h, g, w_in_p, q_norm, wq_p, kv_norm, wk_p, wvt_p, vbias, rot):
    b, s, d = h.shape
    tm = min(PRE_B_TILE, s)
    chunk = min(MLA_KV_TILE, tm)
    assert tm % MLA_Q_TILE == 0 and tm % chunk == 0
    row = lambda w: pl.BlockSpec((1, tm, w), lambda bi, i: (bi, i, 0))
    vrows = B_HEADS * B_V_ROWS
    return pl.pallas_call(
        _pre_b_kernel,
        grid=(b, s // tm),
        in_specs=[row(d), _const_spec((1, d)), _const_spec(w_in_p.shape), _const_spec((1, Q_LORA)),
                  _const_spec((wq_p.shape[1], wq_p.shape[0])), _const_spec((1, KV_LORA)), _const_spec(wk_p.shape),
                  _const_spec(wvt_p.shape), _const_spec((vrows, 1)),
                  pl.BlockSpec((tm, B_QK_PAD), lambda bi, i: (i, 0)),
                  pl.BlockSpec((B_QK_PAD, tm), lambda bi, i: (0, i))],
        out_specs=[pl.BlockSpec((1, B_HEADS, tm // MLA_Q_TILE, B_QK_PAD, MLA_Q_TILE), lambda bi, i: (bi, 0, i, 0, 0)),
                   pl.BlockSpec((1, B_HEADS, tm, B_QK_PAD), lambda bi, i: (bi, 0, i, 0)),
                   pl.BlockSpec((1, tm // chunk, vrows, chunk), lambda bi, i: (bi, i, 0, 0)), row(B_WIDTH)],
        out_shape=[
            jax.ShapeDtypeStruct((b, B_HEADS, s // MLA_Q_TILE, B_QK_PAD, MLA_Q_TILE), BF16),
            jax.ShapeDtypeStruct((b, B_HEADS, s, B_QK_PAD), BF16),
            jax.ShapeDtypeStruct((b, s // chunk, vrows, chunk), BF16),
            jax.ShapeDtypeStruct((b, s, B_WIDTH), BF16),
        ],
        compiler_params=_params("parallel", "parallel"),
    )(h, g.reshape(1, d), w_in_p, q_norm.reshape(1, Q_LORA), wq_p.T, kv_norm.reshape(1, KV_LORA), wk_p,
      wvt_p, vbias, rot, rot.T)


def _prep_b_weights(w_in, w_qb, w_kvb):
    half = B_ROPE // 2
    d = w_in.shape[0]
    o1, o2, o3 = Q_LORA, Q_LORA + KV_LORA, Q_LORA + KV_LORA + B_ROPE
    kr = w_in[:, o2:o3]
    kr_swapped = jnp.concatenate([kr[:, half:], kr[:, :half]], axis=1)
    kr_block = jnp.concatenate([jnp.zeros((d, B_NOPE), w_in.dtype), kr, kr_swapped], axis=1)
    w_in_p = jnp.concatenate([w_in[:, :o2], kr_block, w_in[:, o3:]], axis=1).astype(BF16)

    wq = w_qb.reshape(Q_LORA, B_HEADS, B_NOPE + B_ROPE)
    q_rope = wq[:, :, B_NOPE:]
    q_rope_swapped = jnp.concatenate([q_rope[:, :, half:], q_rope[:, :, :half]], axis=2)
    wq_p = jnp.concatenate([wq[:, :, :B_NOPE], q_rope, q_rope_swapped], axis=2)
    wq_p = wq_p.reshape(Q_LORA, B_HEADS * B_QK_PAD).astype(BF16)

    wkv = w_kvb.reshape(KV_LORA, B_HEADS, B_NOPE + B_VDIM)
    wk_p = jnp.concatenate([wkv[:, :, :B_NOPE], jnp.zeros((KV_LORA, B_HEADS, B_QK_PAD - B_NOPE), w_kvb.dtype)],
                           axis=2).reshape(KV_LORA, B_HEADS * B_QK_PAD).astype(BF16)
    wv = jnp.concatenate([wkv[:, :, B_NOPE:], jnp.zeros((KV_LORA, B_HEADS, B_V_ROWS - B_VDIM), w_kvb.dtype)],
                         axis=2).reshape(KV_LORA, B_HEADS * B_V_ROWS)
    wvt_p = wv.T.astype(BF16)
    vbias = jnp.tile(jnp.concatenate([jnp.zeros((B_VDIM,), F32), jnp.ones((B_V_ROWS - B_VDIM,), F32)]),
                     B_HEADS).reshape(B_HEADS * B_V_ROWS, 1)
    return w_in_p, wq_p, wk_p, wvt_p, vbias


def _rope_table(s):
    half = B_ROPE // 2
    inv_freq = ROPE_THETA ** (-jnp.arange(half, dtype=F32) / half)
    ang = jnp.arange(s, dtype=F32)[:, None] * inv_freq[None, :]
    cos, sin = jnp.cos(ang), jnp.sin(ang)
    return jnp.concatenate([jnp.ones((s, B_NOPE), F32), cos, cos, -sin, sin], axis=1)


def _mla_kernel(q_ref, k_ref, vt_ref, z_ref, o_ref, s0_ref, s1_ref, p0_ref, p1_ref, *, q_tile, q_sub, kv_tile):
    kv_chunks = kv_tile // vt_ref.shape[3]
    n_kv = vt_ref.shape[1] // kv_chunks
    n_q = q_ref.shape[2] // q_sub
    s_bufs, p_bufs = (s0_ref, s1_ref), (p0_ref, p1_ref)
    chains = [(hh, sub) for sub in range(q_sub) for hh in range(B_HEAD_PAIR)]
    m_init = (jnp.full((1, q_tile), NEG_BIG, F32),) * len(chains)

    def q_rows(qt, sub):
        return pl.ds(pl.multiple_of((qt * q_sub + sub) * q_tile, q_tile), q_tile)

    def load_q(qt):
        return [q_ref[0, hh, qt * q_sub + sub] for hh, sub in chains]

    def score_chain(qc, j, slot, c):
        kc = k_ref[0, chains[c][0], j * kv_tile:(j + 1) * kv_tile, :]
        s_bufs[slot][c] = _dot(kc, qc[c])

    def scores(qc, j, slot):
        for c in range(len(chains)):
            score_chain(qc, j, slot, c)

    def exp_chain(slot, m, c):
        m_new = jnp.maximum(m[c], jnp.max(s_bufs[slot][c], axis=0, keepdims=True))
        p_bufs[slot][c] = jnp.exp2(s_bufs[slot][c] - m_new).astype(BF16)
        return m_new, jnp.exp2(m[c] - m_new)

    def exps(slot, m):
        res = [exp_chain(slot, m, c) for c in range(len(chains))]
        return tuple(r[0] for r in res), tuple(r[1] for r in res)

    def value_chain(j, slot, acc, alpha, c):
        rows = slice(chains[c][0] * B_V_ROWS, (chains[c][0] + 1) * B_V_ROWS)
        vt = jnp.concatenate([vt_ref[0, j * kv_chunks + i, rows, :] for i in range(kv_chunks)], axis=1)
        return acc[c] * alpha[c] + _dot(vt, p_bufs[slot][c])

    def step(qc, j_s, j_v, acc, alpha, m):
        s_slot, e_slot, v_slot = j_s % 2, (j_s - 1) % 2, j_v % 2
        acc_out, m_out, alpha_out = [], [], []
        for c in range(len(chains)):
            score_chain(qc, j_s % n_kv, s_slot, c)
            acc_out.append(value_chain(j_v, v_slot, acc, alpha, c))
            m_new, a_new = exp_chain(e_slot, m, c)
            m_out.append(m_new)
            alpha_out.append(a_new)
        return tuple(acc_out), tuple(m_out), tuple(alpha_out)

    def tile_body(qt, carry):
        m, alpha = carry
        qc = load_q(qt)
        qc_next = load_q(jnp.minimum(qt + 1, n_q - 1))
        acc = (jnp.zeros((B_V_ROWS, q_tile), F32),) * len(chains)
        for t in range(2, n_kv):
            acc, m, alpha = step(qc, t, t - 2, acc, alpha, m)
        acc, m, alpha = step(qc_next, n_kv, n_kv - 2, acc, alpha, m)
        acc, m_next, alpha_next = step(qc_next, n_kv + 1, n_kv - 1, acc, alpha, m_init)
        for sub in range(q_sub):
            o_t = jnp.concatenate([acc[c][:B_VDIM] / acc[c][B_VDIM:B_VDIM + 1]
                                   for c, (_, s_) in enumerate(chains) if s_ == sub], axis=0)
            z = z_ref[0, q_rows(qt, sub), :].astype(F32)
            o_ref[0, q_rows(qt, sub), :] = (o_t.T * (z * jax.nn.sigmoid(z))).astype(BF16)
        return m_next, alpha_next

    qc0 = load_q(0)
    scores(qc0, 0, 0)
    m, alpha = exps(0, m_init)
    scores(qc0, 1, 1)
    lax.fori_loop(0, n_q, tile_body, (m, alpha))


def _mla_attn(q_t, k, vt, z):
    b, _, s, _ = k.shape
    n_chunks, kv_chunk = vt.shape[1], vt.shape[3]
    kv_tile = kv_chunk * (MLA_KV_CHUNKS if n_chunks % (2 * MLA_KV_CHUNKS) == 0 else 1)
    n_kv = s // kv_tile
    assert n_kv >= 2 and n_kv % 2 == 0
    q_block = min(MLA_Q_BLOCK, s)
    q_tile = q_t.shape[-1]
    out_w = B_HEAD_PAIR * B_VDIM
    q_sub = MLA_Q_SUB if q_block % (MLA_Q_SUB * q_tile) == 0 else 1
    s_buf = pltpu.VMEM((B_HEAD_PAIR * q_sub, kv_tile, q_tile), F32)
    p_buf = pltpu.VMEM((B_HEAD_PAIR * q_sub, kv_tile, q_tile), BF16)
    return pl.pallas_call(
        functools.partial(_mla_kernel, q_tile=q_tile, q_sub=q_sub, kv_tile=kv_tile),
        grid=(b, B_HEADS // B_HEAD_PAIR, s // q_block),
        in_specs=[pl.BlockSpec((1, B_HEAD_PAIR, q_block // q_tile, B_QK_PAD, q_tile),
                               lambda bi, hp, qi: (bi, hp, qi, 0, 0)),
                  pl.BlockSpec((1, B_HEAD_PAIR, s, B_QK_PAD), lambda bi, hp, qi: (bi, hp, 0, 0)),
                  pl.BlockSpec((1, n_chunks, B_HEAD_PAIR * B_V_ROWS, kv_chunk), lambda bi, hp, qi: (bi, 0, hp, 0)),
                  pl.BlockSpec((1, q_block, out_w), lambda bi, hp, qi: (bi, qi, hp))],
        out_specs=pl.BlockSpec((1, q_block, out_w), lambda bi, hp, qi: (bi, qi, hp)),
        out_shape=jax.ShapeDtypeStruct((b, s, B_WIDTH), BF16),
        scratch_shapes=[s_buf, s_buf, p_buf, p_buf],
        compiler_params=_params("parallel", "parallel", "arbitrary"),
    )(q_t, k, vt, z)


def kernel(x, p, norm_g, a_w_in, a_sink, a_w_out, b_w_in, b_q_norm, b_w_qb, b_kv_norm, b_w_kvb, b_w_out,
           ple_w, ple_norm_g, ple_w_gate, final_norm_g):
    depth = p.shape[0]
    s = x.shape[1]
    rot = _rope_table(s)
    h = x
    for i in range(depth):
        j = i // 2
        if i % 2 == 0:
            q, k_exp, vt, z = _pre_a(h, norm_g[i], a_w_in[j])
            og = _win_attn(q, k_exp, vt, z, a_sink[j])
            w_out = a_w_out[j]
        else:
            w_in_p, wq_p, wk_p, wvt_p, vbias = _prep_b_weights(b_w_in[j], b_w_qb[j], b_w_kvb[j])
            q, k, vt, z = _pre_b(h, norm_g[i], w_in_p, b_q_norm[j], wq_p, b_kv_norm[j], wk_p, wvt_p, vbias, rot)
            og = _mla_attn(q, k, vt, z)
            w_out = b_w_out[j]
        h = _post(og, h, p, i, w_out, ple_norm_g[i], ple_w_gate[i], ple_w[i], final_norm_g, final=(i == depth - 1))
    return h
```

```python
import functools

import jax
import jax.numpy as jnp
from jax import lax
from jax.experimental import pallas as pl
from jax.experimental.pallas import tpu as pltpu

F32 = jnp.float32
BF16 = jnp.bfloat16

EPS = 1e-6
NEG_BIG = -1e30
LOG2E = 1.4426950408889634

A_HEADS = 16
A_KV_HEADS = 4
A_GROUP = A_HEADS // A_KV_HEADS
A_HEAD_DIM = 64
A_WIDTH = A_HEADS * A_HEAD_DIM
A_KV_WIDTH = A_KV_HEADS * A_HEAD_DIM
WINDOW = 128
BLOCK = 128
A_V_ROWS = 80
WIN_Q_BLOCK = 256
MASK_DIST = 1e33

B_HEADS = 16
B_NOPE = 64
B_ROPE = 32
B_VDIM = 64
B_WIDTH = B_HEADS * B_VDIM
Q_LORA = 384
KV_LORA = 256
ROPE_THETA = 10000.0
B_QK_PAD = 128
B_V_ROWS = 80
B_HEAD_PAIR = 2

ROW_TILE = 1024
MLA_Q_BLOCK = 4096
MLA_Q_TILE = 256
MLA_Q_SUB = 1
PRE_B_TILE = 1024
MLA_KV_TILE = 512
MLA_KV_CHUNKS = 1

VMEM_LIMIT = 56 * 1024 * 1024


def _rms(x, g):
    ms = jnp.mean(x * x, axis=-1, keepdims=True)
    return x * lax.rsqrt(ms + EPS) * g


def _dot(a, b):
    return jnp.dot(a, b, preferred_element_type=F32)


def _dot_nt(a, b):
    return lax.dot_general(a, b, (((1,), (1,)), ((), ())), preferred_element_type=F32)


def _params(*sem):
    return pltpu.CompilerParams(dimension_semantics=sem, vmem_limit_bytes=VMEM_LIMIT)


def _const_spec(shape):
    nd = len(shape)
    return pl.BlockSpec(shape, lambda *_: (0,) * nd)


def _pre_a_kernel(x_ref, g_ref, w_ref, wqt_ref, wvt_ref, vb_ref, qt_ref, k_ref, vt_ref, z_ref):
    u = _rms(x_ref[0], g_ref[...]).astype(BF16)
    qs = (A_HEAD_DIM ** -0.5) * LOG2E
    o1, o2, o3 = A_WIDTH, A_WIDTH + A_KV_WIDTH, A_WIDTH + 2 * A_KV_WIDTH
    q_t = (_dot_nt(wqt_ref[...], u) * qs).astype(BF16)
    for hp in range(A_WIDTH // 128):
        for c in range(q_t.shape[1] // WIN_Q_BLOCK):
            qt_ref[0, hp, c] = q_t[hp * 128:(hp + 1) * 128, c * WIN_Q_BLOCK:(c + 1) * WIN_Q_BLOCK]
    k = _dot(u, w_ref[:, o1:o2])
    lane = lax.broadcasted_iota(jnp.int32, (k.shape[0], 2 * A_HEAD_DIM), 1)
    low = lane < A_HEAD_DIM
    for kv in range(A_KV_HEADS):
        blk = k[:, (kv // 2) * 128:(kv // 2 + 1) * 128]
        swapped = pltpu.roll(blk, A_HEAD_DIM, 1)
        first, second = (blk, swapped) if kv % 2 == 0 else (swapped, blk)
        k_ref[0, :, kv * 256:kv * 256 + 128] = jnp.where(low, first, 0.0).astype(BF16)
        k_ref[0, :, kv * 256 + 128:(kv + 1) * 256] = jnp.where(low, 0.0, second).astype(BF16)
    vt_ref[0] = (_dot_nt(wvt_ref[...], u) + vb_ref[...]).astype(BF16)
    z_ref[0] = _dot(u, w_ref[:, o3:]).astype(BF16)


def _pre_a(h, g, w_in):
    b, s, d = h.shape
    tm = min(ROW_TILE, s)
    n_in = w_in.shape[1]
    o2, o3 = A_WIDTH + A_KV_WIDTH, A_WIDTH + 2 * A_KV_WIDTH
    pad = A_V_ROWS - A_HEAD_DIM
    wv = w_in[:, o2:o3].reshape(d, A_KV_HEADS, A_HEAD_DIM)
    wvt = jnp.concatenate([wv, jnp.zeros((d, A_KV_HEADS, pad), w_in.dtype)], axis=2)
    wvt = wvt.reshape(d, A_KV_HEADS * A_V_ROWS).T.astype(BF16)
    vbias = jnp.tile(jnp.concatenate([jnp.zeros((A_HEAD_DIM,), F32), jnp.ones((pad,), F32)]),
                     A_KV_HEADS).reshape(A_KV_HEADS * A_V_ROWS, 1)
    vrows = A_KV_HEADS * A_V_ROWS
    n_pairs = A_WIDTH // 128
    assert tm % WIN_Q_BLOCK == 0
    row = lambda w: pl.BlockSpec((1, tm, w), lambda bi, i: (bi, i, 0))
    return pl.pallas_call(
        _pre_a_kernel,
        grid=(b, s // tm),
        in_specs=[row(d), _const_spec((1, d)), _const_spec((d, n_in)), _const_spec((A_WIDTH, d)),
                  _const_spec((vrows, d)), _const_spec((vrows, 1))],
        out_specs=[pl.BlockSpec((1, n_pairs, tm // WIN_Q_BLOCK, 128, WIN_Q_BLOCK), lambda bi, i: (bi, 0, i, 0, 0)),
                   row(A_KV_HEADS * 256), pl.BlockSpec((1, vrows, tm), lambda bi, i: (bi, 0, i)),
                   row(A_WIDTH)],
        out_shape=[
            jax.ShapeDtypeStruct((b, n_pairs, s // WIN_Q_BLOCK, 128, WIN_Q_BLOCK), BF16),
            jax.ShapeDtypeStruct((b, s, A_KV_HEADS * 256), BF16),
            jax.ShapeDtypeStruct((b, vrows, s), BF16),
            jax.ShapeDtypeStruct((b, s, A_WIDTH), BF16),
        ],
        compiler_params=_params("parallel", "parallel"),
    )(h, g.reshape(1, d), w_in.astype(BF16), w_in[:, :A_WIDTH].T.astype(BF16), wvt, vbias)


def _win_kernel(sink_ref, q_ref, kp_ref, kc_ref, kn_ref, vp_ref, vc_ref, vn_ref, z_ref, o_ref,
                s0_ref, s1_ref, p0_ref, p1_ref, *, seq):
    i = pl.program_id(1)
    s_bufs, p_bufs = (s0_ref, s1_ref), (p0_ref, p1_ref)
    n_keys = WIN_Q_BLOCK + 2 * BLOCK
    n_live = 3 * BLOCK
    halves = range(WIN_Q_BLOCK // BLOCK)
    live = [(slice(j * BLOCK, j * BLOCK + n_live), slice(j * BLOCK, (j + 1) * BLOCK)) for j in halves]
    r = lax.broadcasted_iota(jnp.int32, (n_live, BLOCK), 0)
    c = lax.broadcasted_iota(jnp.int32, (n_live, BLOCK), 1)
    rel = r - BLOCK - c
    neg_dist = []
    for j in halves:
        k_pos = i * WIN_Q_BLOCK + (j - 1) * BLOCK + r
        valid = (jnp.abs(rel) <= WINDOW) & (k_pos >= 0) & (k_pos < seq)
        neg_dist.append(jnp.where(valid, -jnp.abs(rel).astype(F32), -MASK_DIST))
    for p_buf in p_bufs:
        for rows, lanes in live:
            dead = slice(0, BLOCK) if rows.start > 0 else slice(n_live, n_keys)
            p_buf[dead, lanes] = jnp.zeros((BLOCK, BLOCK), BF16)

    def scores(h, slot):
        kv, hp = h // A_GROUP, h // 2
        cols = slice(kv * 256 + (h % 2) * 128, kv * 256 + (h % 2 + 1) * 128)
        k_sel = jnp.concatenate([kp_ref[0, :, cols], kc_ref[0, :, cols], kn_ref[0, :, cols]], axis=0)
        slope = 2.0 ** (-8.0 * (h + 1) / A_HEADS) * LOG2E
        s = _dot(k_sel, q_ref[0, hp, 0])
        tmax = []
        for j, (rows, lanes) in enumerate(live):
            part = s[rows, lanes] + slope * neg_dist[j]
            s_bufs[slot][rows, lanes] = part
            tmax.append(jnp.max(part, axis=0, keepdims=True))
        return jnp.concatenate(tmax, axis=1)

    def exps(h, slot, tmax):
        m = jnp.maximum(tmax, sink_ref[h] * LOG2E)
        for rows, lanes in live:
            p_bufs[slot][rows, lanes] = jnp.exp2(s_bufs[slot][rows, lanes] - m[:, lanes]).astype(BF16)
        return m

    def values(h, slot, m):
        kv = h // A_GROUP
        rows = slice(kv * A_V_ROWS, (kv + 1) * A_V_ROWS)
        vt = jnp.concatenate([vp_ref[0, rows, :], vc_ref[0, rows, :], vn_ref[0, rows, :]], axis=1)
        acc = _dot(vt, p_bufs[slot][...])
        denom = acc[A_HEAD_DIM:A_HEAD_DIM + 1] + jnp.exp2(sink_ref[h] * LOG2E - m)
        return acc[:A_HEAD_DIM] / denom

    tmax, m, pending = {}, {}, {}
    for t in range(A_HEADS + 2):
        if t < A_HEADS:
            tmax[t] = scores(t, t % 2)
        if t >= 2:
            h = t - 2
            pending[h] = values(h, h % 2, m.pop(h))
            if h % 2 == 1:
                o_t = jnp.concatenate([pending.pop(h - 1), pending.pop(h)], axis=0)
                lanes = slice((h // 2) * 128, (h // 2 + 1) * 128)
                z = z_ref[0, :, lanes].astype(F32)
                o_ref[0, :, lanes] = (o_t.T * (z * jax.nn.sigmoid(z))).astype(BF16)
        if 1 <= t <= A_HEADS:
            m[t - 1] = exps(t - 1, (t - 1) % 2, tmax.pop(t - 1))


def _win_attn(q_t, k_exp, vt, z, sink):
    b, s, _ = k_exp.shape
    qb = WIN_Q_BLOCK
    assert s % qb == 0
    per = qb // BLOCK
    nb = s // BLOCK
    vrows = vt.shape[1]
    kw = k_exp.shape[-1]
    prev_i = lambda i: jnp.maximum(per * i - 1, 0)
    next_i = lambda i: jnp.minimum(per * i + per, nb - 1)
    cur = lambda w: pl.BlockSpec((1, qb, w), lambda bi, i: (bi, i, 0))
    s_buf = pltpu.VMEM((qb + 2 * BLOCK, qb), F32)
    p_buf = pltpu.VMEM((qb + 2 * BLOCK, qb), BF16)
    return pl.pallas_call(
        functools.partial(_win_kernel, seq=s),
        grid=(b, s // qb),
        in_specs=[pl.BlockSpec(memory_space=pltpu.SMEM),
                  pl.BlockSpec((1, q_t.shape[1], 1, 128, qb), lambda bi, i: (bi, 0, i, 0, 0)),
                  pl.BlockSpec((1, BLOCK, kw), lambda bi, i: (bi, prev_i(i), 0)),
                  cur(kw),
                  pl.BlockSpec((1, BLOCK, kw), lambda bi, i: (bi, next_i(i), 0)),
                  pl.BlockSpec((1, vrows, BLOCK), lambda bi, i: (bi, 0, prev_i(i))),
                  pl.BlockSpec((1, vrows, qb), lambda bi, i: (bi, 0, i)),
                  pl.BlockSpec((1, vrows, BLOCK), lambda bi, i: (bi, 0, next_i(i))),
                  cur(A_WIDTH)],
        out_specs=cur(A_WIDTH),
        out_shape=jax.ShapeDtypeStruct((b, s, A_WIDTH), BF16),
        scratch_shapes=[s_buf, s_buf, p_buf, p_buf],
        compiler_params=_params("parallel", "parallel"),
    )(sink.astype(F32), q_t, k_exp, k_exp, k_exp, vt, vt, vt, z)


def _post_kernel(og_ref, h_ref, p_ref, wo_ref, pg_ref, wg_ref, pw_ref, fg_ref, o_ref, *, final):
    h1 = h_ref[0] + _dot(og_ref[0], wo_ref[...])
    gate = jax.nn.sigmoid(_dot(_rms(h1, pg_ref[...]).astype(BF16), wg_ref[...]))
    h2 = h1 + _dot(p_ref[0, 0].astype(BF16), pw_ref[...]) * gate
    if final:
        h2 = _rms(h2, fg_ref[...])
    o_ref[0] = h2


def _post(og, h, p, layer, w_out, ple_g, w_gate, ple_w, final_g, final):
    b, s, d = h.shape
    tm = min(ROW_TILE, s)
    pd = p.shape[-1]
    row = lambda w: pl.BlockSpec((1, tm, w), lambda bi, i: (bi, i, 0))
    return pl.pallas_call(
        functools.partial(_post_kernel, final=final),
        grid=(b, s // tm),
        in_specs=[row(og.shape[-1]), row(d), pl.BlockSpec((1, 1, tm, pd), lambda bi, i: (layer, bi, i, 0)),
                  _const_spec(w_out.shape), _const_spec((1, d)),
                  _const_spec((d, d)), _const_spec((pd, d)), _const_spec((1, d))],
        out_specs=row(d),
        out_shape=jax.ShapeDtypeStruct((b, s, d), F32),
        compiler_params=_params("parallel", "parallel"),
    )(og, h, p, w_out.astype(BF16), ple_g.reshape(1, d), w_gate.astype(BF16), ple_w.astype(BF16),
      final_g.reshape(1, d))


def _pre_b_kernel(x_ref, g_ref, w_ref, qn_ref, wqt_ref, kn_ref, wk_ref, wvt_ref, vb_ref, r_ref, rt_ref,
                  qt_ref, k_ref, vt_ref, z_ref):
    u = _rms(x_ref[0], g_ref[...]).astype(BF16)
    o1, o2, o3 = Q_LORA, Q_LORA + KV_LORA, Q_LORA + KV_LORA + B_QK_PAD
    rot = r_ref[...]
    cq = _rms(_dot(u, w_ref[:, :o1]), qn_ref[...]).astype(BF16)
    qs = ((B_NOPE + B_ROPE) ** -0.5) * LOG2E
    q_t = (_dot_nt(wqt_ref[...], cq) * jnp.tile(rt_ref[...] * qs, (B_HEADS, 1))).astype(BF16)
    for hh in range(B_HEADS):
        for c in range(q_t.shape[1] // MLA_Q_TILE):
            qt_ref[0, hh, c] = q_t[hh * B_QK_PAD:(hh + 1) * B_QK_PAD, c * MLA_Q_TILE:(c + 1) * MLA_Q_TILE]
    ckv = _rms(_dot(u, w_ref[:, o1:o2]), kn_ref[...]).astype(BF16)
    y = _dot(u, w_ref[:, o2:o3]) * rot
    lane = lax.broadcasted_iota(jnp.int32, y.shape, 1)
    both = pltpu.roll(y, 32, 1) + pltpu.roll(y, 96, 1)
    kr = y + jnp.where(lane >= B_NOPE, both, 0.0)
    k = (_dot(ckv, wk_ref[...]) + jnp.tile(kr, (1, B_HEADS))).astype(BF16)
    for hh in range(B_HEADS):
        k_ref[0, hh] = k[:, hh * B_QK_PAD:(hh + 1) * B_QK_PAD]
    vt_all = (_dot_nt(wvt_ref[...], ckv) + vb_ref[...]).astype(BF16)
    for c in range(vt_ref.shape[1]):
        vt_ref[0, c] = vt_all[:, c * MLA_KV_TILE:(c + 1) * MLA_KV_TILE]
    z_ref[0] = _dot(u, w_ref[:, o3:]).astype(BF16)


def _pre_b(h, g, w_in_p, q_norm, wq_p, kv_norm, wk_p, wvt_p, vbias, rot):
    b, s, d = h.shape
    tm = min(PRE_B_TILE, s)
    chunk = min(MLA_KV_TILE, tm)
    assert tm % MLA_Q_TILE == 0 and tm % chunk == 0
    row = lambda w: pl.BlockSpec((1, tm, w), lambda bi, i: (bi, i, 0))
    vrows = B_HEADS * B_V_ROWS
    return pl.pallas_call(
        _pre_b_kernel,
        grid=(b, s // tm),
        in_specs=[row(d), _const_spec((1, d)), _const_spec(w_in_p.shape), _const_spec((1, Q_LORA)),
                  _const_spec((wq_p.shape[1], wq_p.shape[0])), _const_spec((1, KV_LORA)), _const_spec(wk_p.shape),
                  _const_spec(wvt_p.shape), _const_spec((vrows, 1)),
                  pl.BlockSpec((tm, B_QK_PAD), lambda bi, i: (i, 0)),
                  pl.BlockSpec((B_QK_PAD, tm), lambda bi, i: (0, i))],
        out_specs=[pl.BlockSpec((1, B_HEADS, tm // MLA_Q_TILE, B_QK_PAD, MLA_Q_TILE), lambda bi, i: (bi, 0, i, 0, 0)),
                   pl.BlockSpec((1, B_HEADS, tm, B_QK_PAD), lambda bi, i: (bi, 0, i, 0)),
                   pl.BlockSpec((1, tm // chunk, vrows, chunk), lambda bi, i: (bi, i, 0, 0)), row(B_WIDTH)],
        out_shape=[
            jax.ShapeDtypeStruct((b, B_HEADS, s // MLA_Q_TILE, B_QK_PAD, MLA_Q_TILE), BF16),
            jax.ShapeDtypeStruct((b, B_HEADS, s, B_QK_PAD), BF16),
            jax.ShapeDtypeStruct((b, s // chunk, vrows, chunk), BF16),
            jax.ShapeDtypeStruct((b, s, B_WIDTH), BF16),
        ],
        compiler_params=_params("parallel", "parallel"),
    )(h, g.reshape(1, d), w_in_p, q_norm.reshape(1, Q_LORA), wq_p.T, kv_norm.reshape(1, KV_LORA), wk_p,
      wvt_p, vbias, rot, rot.T)


def _prep_b_weights(w_in, w_qb, w_kvb):
    half = B_ROPE // 2
    d = w_in.shape[0]
    o1, o2, o3 = Q_LORA, Q_LORA + KV_LORA, Q_LORA + KV_LORA + B_ROPE
    kr = w_in[:, o2:o3]
    kr_swapped = jnp.concatenate([kr[:, half:], kr[:, :half]], axis=1)
    kr_block = jnp.concatenate([jnp.zeros((d, B_NOPE), w_in.dtype), kr, kr_swapped], axis=1)
    w_in_p = jnp.concatenate([w_in[:, :o2], kr_block, w_in[:, o3:]], axis=1).astype(BF16)

    wq = w_qb.reshape(Q_LORA, B_HEADS, B_NOPE + B_ROPE)
    q_rope = wq[:, :, B_NOPE:]
    q_rope_swapped = jnp.concatenate([q_rope[:, :, half:], q_rope[:, :, :half]], axis=2)
    wq_p = jnp.concatenate([wq[:, :, :B_NOPE], q_rope, q_rope_swapped], axis=2)
    wq_p = wq_p.reshape(Q_LORA, B_HEADS * B_QK_PAD).astype(BF16)

    wkv = w_kvb.reshape(KV_LORA, B_HEADS, B_NOPE + B_VDIM)
    wk_p = jnp.concatenate([wkv[:, :, :B_NOPE], jnp.zeros((KV_LORA, B_HEADS, B_QK_PAD - B_NOPE), w_kvb.dtype)],
                           axis=2).reshape(KV_LORA, B_HEADS * B_QK_PAD).astype(BF16)
    wv = jnp.concatenate([wkv[:, :, B_NOPE:], jnp.zeros((KV_LORA, B_HEADS, B_V_ROWS - B_VDIM), w_kvb.dtype)],
                         axis=2).reshape(KV_LORA, B_HEADS * B_V_ROWS)
    wvt_p = wv.T.astype(BF16)
    vbias = jnp.tile(jnp.concatenate([jnp.zeros((B_VDIM,), F32), jnp.ones((B_V_ROWS - B_VDIM,), F32)]),
                     B_HEADS).reshape(B_HEADS * B_V_ROWS, 1)
    return w_in_p, wq_p, wk_p, wvt_p, vbias


def _rope_table(s):
    half = B_ROPE // 2
    inv_freq = ROPE_THETA ** (-jnp.arange(half, dtype=F32) / half)
    ang = jnp.arange(s, dtype=F32)[:, None] * inv_freq[None, :]
    cos, sin = jnp.cos(ang), jnp.sin(ang)
    return jnp.concatenate([jnp.ones((s, B_NOPE), F32), cos, cos, -sin, sin], axis=1)


def _mla_kernel(q_ref, k_ref, vt_ref, z_ref, o_ref, s0_ref, s1_ref, p0_ref, p1_ref, *, q_tile, q_sub, kv_tile):
    kv_chunks = kv_tile // vt_ref.shape[3]
    n_kv = vt_ref.shape[1] // kv_chunks
    n_q = q_ref.shape[2] // q_sub
    s_bufs, p_bufs = (s0_ref, s1_ref), (p0_ref, p1_ref)
    chains = [(hh, sub) for sub in range(q_sub) for hh in range(B_HEAD_PAIR)]
    m_init = (jnp.full((1, q_tile), NEG_BIG, F32),) * len(chains)

    def q_rows(qt, sub):
        return pl.ds(pl.multiple_of((qt * q_sub + sub) * q_tile, q_tile), q_tile)

    def load_q(qt):
        return [q_ref[0, hh, qt * q_sub + sub] for hh, sub in chains]

    def score_chain(qc, j, slot, c):
        kc = k_ref[0, chains[c][0], j * kv_tile:(j + 1) * kv_tile, :]
        s_bufs[slot][c] = _dot(kc, qc[c])

    def scores(qc, j, slot):
        for c in range(len(chains)):
            score_chain(qc, j, slot, c)

    def exp_chain(slot, m, c):
        m_new = jnp.maximum(m[c], jnp.max(s_bufs[slot][c], axis=0, keepdims=True))
        p_bufs[slot][c] = jnp.exp2(s_bufs[slot][c] - m_new).astype(BF16)
        return m_new, jnp.exp2(m[c] - m_new)

    def exps(slot, m):
        res = [exp_chain(slot, m, c) for c in range(len(chains))]
        return tuple(r[0] for r in res), tuple(r[1] for r in res)

    def value_chain(j, slot, acc, alpha, c):
        rows = slice(chains[c][0] * B_V_ROWS, (chains[c][0] + 1) * B_V_ROWS)
        vt = jnp.concatenate([vt_ref[0, j * kv_chunks + i, rows, :] for i in range(kv_chunks)], axis=1)
        return acc[c] * alpha[c] + _dot(vt, p_bufs[slot][c])

    def step(qc, j_s, j_v, acc, alpha, m):
        s_slot, e_slot, v_slot = j_s % 2, (j_s - 1) % 2, j_v % 2
        acc_out, m_out, alpha_out = [], [], []
        for c in range(len(chains)):
            score_chain(qc, j_s % n_kv, s_slot, c)
            acc_out.append(value_chain(j_v, v_slot, acc, alpha, c))
            m_new, a_new = exp_chain(e_slot, m, c)
            m_out.append(m_new)
            alpha_out.append(a_new)
        return tuple(acc_out), tuple(m_out), tuple(alpha_out)

    def tile_body(qt, carry):
        m, alpha = carry
        qc = load_q(qt)
        qc_next = load_q(jnp.minimum(qt + 1, n_q - 1))
        acc = (jnp.zeros((B_V_ROWS, q_tile), F32),) * len(chains)
        for t in range(2, n_kv):
            acc, m, alpha = step(qc, t, t - 2, acc, alpha, m)
        acc, m, alpha = step(qc_next, n_kv, n_kv - 2, acc, alpha, m)
        acc, m_next, alpha_next = step(qc_next, n_kv + 1, n_kv - 1, acc, alpha, m_init)
        for sub in range(q_sub):
            o_t = jnp.concatenate([acc[c][:B_VDIM] / acc[c][B_VDIM:B_VDIM + 1]
                                   for c, (_, s_) in enumerate(chains) if s_ == sub], axis=0)
            z = z_ref[0, q_rows(qt, sub), :].astype(F32)
            o_ref[0, q_rows(qt, sub), :] = (o_t.T * (z * jax.nn.sigmoid(z))).astype(BF16)
        return m_next, alpha_next

    qc0 = load_q(0)
    scores(qc0, 0, 0)
    m, alpha = exps(0, m_init)
    scores(qc0, 1, 1)
    unroll = max(u for u in (4, 2, 1) if n_q % u == 0)

    def tiles(i, carry):
        for u in range(unroll):
            carry = tile_body(unroll * i + u, carry)
        return carry

    lax.fori_loop(0, n_q // unroll, tiles, (m, alpha))


def _mla_attn(q_t, k, vt, z):
    b, _, s, _ = k.shape
    n_chunks, kv_chunk = vt.shape[1], vt.shape[3]
    kv_tile = kv_chunk * (MLA_KV_CHUNKS if n_chunks % (2 * MLA_KV_CHUNKS) == 0 else 1)
    n_kv = s // kv_tile
    assert n_kv >= 2 and n_kv % 2 == 0
    q_block = min(MLA_Q_BLOCK, s)
    q_tile = q_t.shape[-1]
    out_w = B_HEAD_PAIR * B_VDIM
    q_sub = MLA_Q_SUB if q_block % (MLA_Q_SUB * q_tile) == 0 else 1
    s_buf = pltpu.VMEM((B_HEAD_PAIR * q_sub, kv_tile, q_tile), F32)
    p_buf = pltpu.VMEM((B_HEAD_PAIR * q_sub, kv_tile, q_tile), BF16)
    return pl.pallas_call(
        functools.partial(_mla_kernel, q_tile=q_tile, q_sub=q_sub, kv_tile=kv_tile),
        grid=(b, B_HEADS // B_HEAD_PAIR, s // q_block),
        in_specs=[pl.BlockSpec((1, B_HEAD_PAIR, q_block // q_tile, B_QK_PAD, q_tile),
                               lambda bi, hp, qi: (bi, hp, qi, 0, 0)),
                  pl.BlockSpec((1, B_HEAD_PAIR, s, B_QK_PAD), lambda bi, hp, qi: (bi, hp, 0, 0)),
                  pl.BlockSpec((1, n_chunks, B_HEAD_PAIR * B_V_ROWS, kv_chunk), lambda bi, hp, qi: (bi, 0, hp, 0)),
                  pl.BlockSpec((1, q_block, out_w), lambda bi, hp, qi: (bi, qi, hp))],
        out_specs=pl.BlockSpec((1, q_block, out_w), lambda bi, hp, qi: (bi, qi, hp)),
        out_shape=jax.ShapeDtypeStruct((b, s, B_WIDTH), BF16),
        scratch_shapes=[s_buf, s_buf, p_buf, p_buf],
        compiler_params=_params("parallel", "parallel", "arbitrary"),
    )(q_t, k, vt, z)


def kernel(x, p, norm_g, a_w_in, a_sink, a_w_out, b_w_in, b_q_norm, b_w_qb, b_kv_norm, b_w_kvb, b_w_out,
           ple_w, ple_norm_g, ple_w_gate, final_norm_g):
    depth = p.shape[0]
    s = x.shape[1]
    rot = _rope_table(s)
    h = x
    for i in range(depth):
        j = i // 2
        if i % 2 == 0:
            q, k_exp, vt, z = _pre_a(h, norm_g[i], a_w_in[j])
            og = _win_attn(q, k_exp, vt, z, a_sink[j])
            w_out = a_w_out[j]
        else:
            w_in_p, wq_p, wk_p, wvt_p, vbias = _prep_b_weights(b_w_in[j], b_w_qb[j], b_w_kvb[j])
            q, k, vt, z = _pre_b(h, norm_g[i], w_in_p, b_q_norm[j], wq_p, b_kv_norm[j], wk_p, wvt_p, vbias, rot)
            og = _mla_attn(q, k, vt, z)
            w_out = b_w_out[j]
        h = _post(og, h, p, i, w_out, ple_norm_g[i], ple_w_gate[i], ple_w[i], final_norm_g, final=(i == depth - 1))
    return h
```

```python
import functools

import jax
import jax.numpy as jnp
from jax import lax
from jax.experimental import pallas as pl
from jax.experimental.pallas import tpu as pltpu

F32 = jnp.float32
BF16 = jnp.bfloat16

EPS = 1e-6
NEG_BIG = -1e30
LOG2E = 1.4426950408889634

A_HEADS = 16
A_KV_HEADS = 4
A_GROUP = A_HEADS // A_KV_HEADS
A_HEAD_DIM = 64
A_WIDTH = A_HEADS * A_HEAD_DIM
A_KV_WIDTH = A_KV_HEADS * A_HEAD_DIM
WINDOW = 128
BLOCK = 128
A_V_ROWS = 80
WIN_Q_BLOCK = 256
MASK_DIST = 1e33

B_HEADS = 16
B_NOPE = 64
B_ROPE = 32
B_VDIM = 64
B_WIDTH = B_HEADS * B_VDIM
Q_LORA = 384
KV_LORA = 256
ROPE_THETA = 10000.0
B_QK_PAD = 128
B_V_ROWS = 80
B_HEAD_PAIR = 2

ROW_TILE = 1024
MLA_Q_BLOCK = 8192
MLA_Q_TILE = 256
MLA_Q_SUB = 1
PRE_B_TILE = 1024
MLA_KV_TILE = 512
MLA_KV_CHUNKS = 1

VMEM_LIMIT = 56 * 1024 * 1024


def _rms(x, g):
    ms = jnp.mean(x * x, axis=-1, keepdims=True)
    return x * lax.rsqrt(ms + EPS) * g


def _dot(a, b):
    return jnp.dot(a, b, preferred_element_type=F32)


def _dot_nt(a, b):
    return lax.dot_general(a, b, (((1,), (1,)), ((), ())), preferred_element_type=F32)


def _params(*sem):
    return pltpu.CompilerParams(dimension_semantics=sem, vmem_limit_bytes=VMEM_LIMIT)


def _const_spec(shape):
    nd = len(shape)
    return pl.BlockSpec(shape, lambda *_: (0,) * nd)


def _pre_a_kernel(x_ref, g_ref, w_ref, wqt_ref, wvt_ref, vb_ref, qt_ref, k_ref, vt_ref, z_ref):
    u = _rms(x_ref[0], g_ref[...]).astype(BF16)
    qs = (A_HEAD_DIM ** -0.5) * LOG2E
    o1, o2, o3 = A_WIDTH, A_WIDTH + A_KV_WIDTH, A_WIDTH + 2 * A_KV_WIDTH
    q_t = (_dot_nt(wqt_ref[...], u) * qs).astype(BF16)
    for hp in range(A_WIDTH // 128):
        for c in range(q_t.shape[1] // WIN_Q_BLOCK):
            qt_ref[0, hp, c] = q_t[hp * 128:(hp + 1) * 128, c * WIN_Q_BLOCK:(c + 1) * WIN_Q_BLOCK]
    k = _dot(u, w_ref[:, o1:o2])
    lane = lax.broadcasted_iota(jnp.int32, (k.shape[0], 2 * A_HEAD_DIM), 1)
    low = lane < A_HEAD_DIM
    for kv in range(A_KV_HEADS):
        blk = k[:, (kv // 2) * 128:(kv // 2 + 1) * 128]
        swapped = pltpu.roll(blk, A_HEAD_DIM, 1)
        first, second = (blk, swapped) if kv % 2 == 0 else (swapped, blk)
        k_ref[0, :, kv * 256:kv * 256 + 128] = jnp.where(low, first, 0.0).astype(BF16)
        k_ref[0, :, kv * 256 + 128:(kv + 1) * 256] = jnp.where(low, 0.0, second).astype(BF16)
    vt_ref[0] = (_dot_nt(wvt_ref[...], u) + vb_ref[...]).astype(BF16)
    z_ref[0] = _dot(u, w_ref[:, o3:]).astype(BF16)


def _pre_a(h, g, w_in):
    b, s, d = h.shape
    tm = min(ROW_TILE, s)
    n_in = w_in.shape[1]
    o2, o3 = A_WIDTH + A_KV_WIDTH, A_WIDTH + 2 * A_KV_WIDTH
    pad = A_V_ROWS - A_HEAD_DIM
    wv = w_in[:, o2:o3].reshape(d, A_KV_HEADS, A_HEAD_DIM)
    wvt = jnp.concatenate([wv, jnp.zeros((d, A_KV_HEADS, pad), w_in.dtype)], axis=2)
    wvt = wvt.reshape(d, A_KV_HEADS * A_V_ROWS).T.astype(BF16)
    vbias = jnp.tile(jnp.concatenate([jnp.zeros((A_HEAD_DIM,), F32), jnp.ones((pad,), F32)]),
                     A_KV_HEADS).reshape(A_KV_HEADS * A_V_ROWS, 1)
    vrows = A_KV_HEADS * A_V_ROWS
    n_pairs = A_WIDTH // 128
    assert tm % WIN_Q_BLOCK == 0
    row = lambda w: pl.BlockSpec((1, tm, w), lambda bi, i: (bi, i, 0))
    return pl.pallas_call(
        _pre_a_kernel,
        grid=(b, s // tm),
        in_specs=[row(d), _const_spec((1, d)), _const_spec((d, n_in)), _const_spec((A_WIDTH, d)),
                  _const_spec((vrows, d)), _const_spec((vrows, 1))],
        out_specs=[pl.BlockSpec((1, n_pairs, tm // WIN_Q_BLOCK, 128, WIN_Q_BLOCK), lambda bi, i: (bi, 0, i, 0, 0)),
                   row(A_KV_HEADS * 256), pl.BlockSpec((1, vrows, tm), lambda bi, i: (bi, 0, i)),
                   row(A_WIDTH)],
        out_shape=[
            jax.ShapeDtypeStruct((b, n_pairs, s // WIN_Q_BLOCK, 128, WIN_Q_BLOCK), BF16),
            jax.ShapeDtypeStruct((b, s, A_KV_HEADS * 256), BF16),
            jax.ShapeDtypeStruct((b, vrows, s), BF16),
            jax.ShapeDtypeStruct((b, s, A_WIDTH), BF16),
        ],
        compiler_params=_params("parallel", "parallel"),
    )(h, g.reshape(1, d), w_in.astype(BF16), w_in[:, :A_WIDTH].T.astype(BF16), wvt, vbias)


def _win_kernel(sink_ref, q_ref, kp_ref, kc_ref, kn_ref, vp_ref, vc_ref, vn_ref, z_ref, o_ref,
                s0_ref, s1_ref, p0_ref, p1_ref, *, seq):
    i = pl.program_id(1)
    s_bufs, p_bufs = (s0_ref, s1_ref), (p0_ref, p1_ref)
    n_keys = WIN_Q_BLOCK + 2 * BLOCK
    n_live = 3 * BLOCK
    halves = range(WIN_Q_BLOCK // BLOCK)
    live = [(slice(j * BLOCK, j * BLOCK + n_live), slice(j * BLOCK, (j + 1) * BLOCK)) for j in halves]
    r = lax.broadcasted_iota(jnp.int32, (n_live, BLOCK), 0)
    c = lax.broadcasted_iota(jnp.int32, (n_live, BLOCK), 1)
    rel = r - BLOCK - c
    neg_dist = []
    for j in halves:
        k_pos = i * WIN_Q_BLOCK + (j - 1) * BLOCK + r
        valid = (jnp.abs(rel) <= WINDOW) & (k_pos >= 0) & (k_pos < seq)
        neg_dist.append(jnp.where(valid, -jnp.abs(rel).astype(F32), -MASK_DIST))
    for p_buf in p_bufs:
        for rows, lanes in live:
            dead = slice(0, BLOCK) if rows.start > 0 else slice(n_live, n_keys)
            p_buf[dead, lanes] = jnp.zeros((BLOCK, BLOCK), BF16)

    def scores(h, slot):
        kv, hp = h // A_GROUP, h // 2
        cols = slice(kv * 256 + (h % 2) * 128, kv * 256 + (h % 2 + 1) * 128)
        k_sel = jnp.concatenate([kp_ref[0, :, cols], kc_ref[0, :, cols], kn_ref[0, :, cols]], axis=0)
        slope = 2.0 ** (-8.0 * (h + 1) / A_HEADS) * LOG2E
        s = _dot(k_sel, q_ref[0, hp, 0])
        tmax = []
        for j, (rows, lanes) in enumerate(live):
            part = s[rows, lanes] + slope * neg_dist[j]
            s_bufs[slot][rows, lanes] = part
            tmax.append(jnp.max(part, axis=0, keepdims=True))
        return jnp.concatenate(tmax, axis=1)

    def exps(h, slot, tmax):
        m = jnp.maximum(tmax, sink_ref[h] * LOG2E)
        for rows, lanes in live:
            p_bufs[slot][rows, lanes] = jnp.exp2(s_bufs[slot][rows, lanes] - m[:, lanes]).astype(BF16)
        return m

    def values(h, slot, m):
        kv = h // A_GROUP
        rows = slice(kv * A_V_ROWS, (kv + 1) * A_V_ROWS)
        vt = jnp.concatenate([vp_ref[0, rows, :], vc_ref[0, rows, :], vn_ref[0, rows, :]], axis=1)
        acc = _dot(vt, p_bufs[slot][...])
        denom = acc[A_HEAD_DIM:A_HEAD_DIM + 1] + jnp.exp2(sink_ref[h] * LOG2E - m)
        return acc[:A_HEAD_DIM] / denom

    tmax, m, pending = {}, {}, {}
    for t in range(A_HEADS + 2):
        if t < A_HEADS:
            tmax[t] = scores(t, t % 2)
        if t >= 2:
            h = t - 2
            pending[h] = values(h, h % 2, m.pop(h))
            if h % 2 == 1:
                o_t = jnp.concatenate([pending.pop(h - 1), pending.pop(h)], axis=0)
                lanes = slice((h // 2) * 128, (h // 2 + 1) * 128)
                z = z_ref[0, :, lanes].astype(F32)
                o_ref[0, :, lanes] = (o_t.T * (z * jax.nn.sigmoid(z))).astype(BF16)
        if 1 <= t <= A_HEADS:
            m[t - 1] = exps(t - 1, (t - 1) % 2, tmax.pop(t - 1))


def _win_attn(q_t, k_exp, vt, z, sink):
    b, s, _ = k_exp.shape
    qb = WIN_Q_BLOCK
    assert s % qb == 0
    per = qb // BLOCK
    nb = s // BLOCK
    vrows = vt.shape[1]
    kw = k_exp.shape[-1]
    prev_i = lambda i: jnp.maximum(per * i - 1, 0)
    next_i = lambda i: jnp.minimum(per * i + per, nb - 1)
    cur = lambda w: pl.BlockSpec((1, qb, w), lambda bi, i: (bi, i, 0))
    s_buf = pltpu.VMEM((qb + 2 * BLOCK, qb), F32)
    p_buf = pltpu.VMEM((qb + 2 * BLOCK, qb), BF16)
    return pl.pallas_call(
        functools.partial(_win_kernel, seq=s),
        grid=(b, s // qb),
        in_specs=[pl.BlockSpec(memory_space=pltpu.SMEM),
                  pl.BlockSpec((1, q_t.shape[1], 1, 128, qb), lambda bi, i: (bi, 0, i, 0, 0)),
                  pl.BlockSpec((1, BLOCK, kw), lambda bi, i: (bi, prev_i(i), 0)),
                  cur(kw),
                  pl.BlockSpec((1, BLOCK, kw), lambda bi, i: (bi, next_i(i), 0)),
                  pl.BlockSpec((1, vrows, BLOCK), lambda bi, i: (bi, 0, prev_i(i))),
                  pl.BlockSpec((1, vrows, qb), lambda bi, i: (bi, 0, i)),
                  pl.BlockSpec((1, vrows, BLOCK), lambda bi, i: (bi, 0, next_i(i))),
                  cur(A_WIDTH)],
        out_specs=cur(A_WIDTH),
        out_shape=jax.ShapeDtypeStruct((b, s, A_WIDTH), BF16),
        scratch_shapes=[s_buf, s_buf, p_buf, p_buf],
        compiler_params=_params("parallel", "parallel"),
    )(sink.astype(F32), q_t, k_exp, k_exp, k_exp, vt, vt, vt, z)


def _post_kernel(og_ref, h_ref, p_ref, wo_ref, pg_ref, wg_ref, pw_ref, fg_ref, o_ref, *, final):
    h1 = h_ref[0] + _dot(og_ref[0], wo_ref[...])
    gate = jax.nn.sigmoid(_dot(_rms(h1, pg_ref[...]).astype(BF16), wg_ref[...]))
    h2 = h1 + _dot(p_ref[0, 0].astype(BF16), pw_ref[...]) * gate
    if final:
        h2 = _rms(h2, fg_ref[...])
    o_ref[0] = h2


def _post(og, h, p, layer, w_out, ple_g, w_gate, ple_w, final_g, final):
    b, s, d = h.shape
    tm = min(ROW_TILE, s)
    pd = p.shape[-1]
    row = lambda w: pl.BlockSpec((1, tm, w), lambda bi, i: (bi, i, 0))
    return pl.pallas_call(
        functools.partial(_post_kernel, final=final),
        grid=(b, s // tm),
        in_specs=[row(og.shape[-1]), row(d), pl.BlockSpec((1, 1, tm, pd), lambda bi, i: (layer, bi, i, 0)),
                  _const_spec(w_out.shape), _const_spec((1, d)),
                  _const_spec((d, d)), _const_spec((pd, d)), _const_spec((1, d))],
        out_specs=row(d),
        out_shape=jax.ShapeDtypeStruct((b, s, d), F32),
        compiler_params=_params("parallel", "parallel"),
    )(og, h, p, w_out.astype(BF16), ple_g.reshape(1, d), w_gate.astype(BF16), ple_w.astype(BF16),
      final_g.reshape(1, d))


def _pre_b_kernel(x_ref, g_ref, w_ref, qn_ref, wqt_ref, kn_ref, wk_ref, wvt_ref, vb_ref, r_ref, rt_ref,
                  qt_ref, k_ref, vt_ref, z_ref):
    u = _rms(x_ref[0], g_ref[...]).astype(BF16)
    o1, o2, o3 = Q_LORA, Q_LORA + KV_LORA, Q_LORA + KV_LORA + B_QK_PAD
    rot = r_ref[...]
    cq = _rms(_dot(u, w_ref[:, :o1]), qn_ref[...]).astype(BF16)
    qs = ((B_NOPE + B_ROPE) ** -0.5) * LOG2E
    q_t = (_dot_nt(wqt_ref[...], cq) * jnp.tile(rt_ref[...] * qs, (B_HEADS, 1))).astype(BF16)
    for hh in range(B_HEADS):
        for c in range(q_t.shape[1] // MLA_Q_TILE):
            qt_ref[0, hh, c] = q_t[hh * B_QK_PAD:(hh + 1) * B_QK_PAD, c * MLA_Q_TILE:(c + 1) * MLA_Q_TILE]
    ckv = _rms(_dot(u, w_ref[:, o1:o2]), kn_ref[...]).astype(BF16)
    y = _dot(u, w_ref[:, o2:o3]) * rot
    lane = lax.broadcasted_iota(jnp.int32, y.shape, 1)
    both = pltpu.roll(y, 32, 1) + pltpu.roll(y, 96, 1)
    kr = y + jnp.where(lane >= B_NOPE, both, 0.0)
    k = (_dot(ckv, wk_ref[...]) + jnp.tile(kr, (1, B_HEADS))).astype(BF16)
    for hh in range(B_HEADS):
        k_ref[0, hh] = k[:, hh * B_QK_PAD:(hh + 1) * B_QK_PAD]
    vt_all = (_dot_nt(wvt_ref[...], ckv) + vb_ref[...]).astype(BF16)
    for c in range(vt_ref.shape[1]):
        vt_ref[0, c] = vt_all[:, c * MLA_KV_TILE:(c + 1) * MLA_KV_TILE]
    z_ref[0] = _dot(u, w_ref[:, o3:]).astype(BF16)


def _pre_b(h, g, w_in_p, q_norm, wq_p, kv_norm, wk_p, wvt_p, vbias, rot):
    b, s, d = h.shape
    tm = min(PRE_B_TILE, s)
    chunk = min(MLA_KV_TILE, tm)
    assert tm % MLA_Q_TILE == 0 and tm % chunk == 0
    row = lambda w: pl.BlockSpec((1, tm, w), lambda bi, i: (bi, i, 0))
    vrows = B_HEADS * B_V_ROWS
    return pl.pallas_call(
        _pre_b_kernel,
        grid=(b, s // tm),
        in_specs=[row(d), _const_spec((1, d)), _const_spec(w_in_p.shape), _const_spec((1, Q_LORA)),
                  _const_spec((wq_p.shape[1], wq_p.shape[0])), _const_spec((1, KV_LORA)), _const_spec(wk_p.shape),
                  _const_spec(wvt_p.shape), _const_spec((vrows, 1)),
                  pl.BlockSpec((tm, B_QK_PAD), lambda bi, i: (i, 0)),
                  pl.BlockSpec((B_QK_PAD, tm), lambda bi, i: (0, i))],
        out_specs=[pl.BlockSpec((1, B_HEADS, tm // MLA_Q_TILE, B_QK_PAD, MLA_Q_TILE), lambda bi, i: (bi, 0, i, 0, 0)),
                   pl.BlockSpec((1, B_HEADS, tm, B_QK_PAD), lambda bi, i: (bi, 0, i, 0)),
                   pl.BlockSpec((1, tm // chunk, vrows, chunk), lambda bi, i: (bi, i, 0, 0)), row(B_WIDTH)],
        out_shape=[
            jax.ShapeDtypeStruct((b, B_HEADS, s // MLA_Q_TILE, B_QK_PAD, MLA_Q_TILE), BF16),
            jax.ShapeDtypeStruct((b, B_HEADS, s, B_QK_PAD), BF16),
            jax.ShapeDtypeStruct((b, s // chunk, vrows, chunk), BF16),
            jax.ShapeDtypeStruct((b, s, B_WIDTH), BF16),
        ],
        compiler_params=_params("parallel", "parallel"),
    )(h, g.reshape(1, d), w_in_p, q_norm.reshape(1, Q_LORA), wq_p.T, kv_norm.reshape(1, KV_LORA), wk_p,
      wvt_p, vbias, rot, rot.T)


def _prep_b_weights(w_in, w_qb, w_kvb):
    half = B_ROPE // 2
    d = w_in.shape[0]
    o1, o2, o3 = Q_LORA, Q_LORA + KV_LORA, Q_LORA + KV_LORA + B_ROPE
    kr = w_in[:, o2:o3]
    kr_swapped = jnp.concatenate([kr[:, half:], kr[:, :half]], axis=1)
    kr_block = jnp.concatenate([jnp.zeros((d, B_NOPE), w_in.dtype), kr, kr_swapped], axis=1)
    w_in_p = jnp.concatenate([w_in[:, :o2], kr_block, w_in[:, o3:]], axis=1).astype(BF16)

    wq = w_qb.reshape(Q_LORA, B_HEADS, B_NOPE + B_ROPE)
    q_rope = wq[:, :, B_NOPE:]
    q_rope_swapped = jnp.concatenate([q_rope[:, :, half:], q_rope[:, :, :half]], axis=2)
    wq_p = jnp.concatenate([wq[:, :, :B_NOPE], q_rope, q_rope_swapped], axis=2)
    wq_p = wq_p.reshape(Q_LORA, B_HEADS * B_QK_PAD).astype(BF16)

    wkv = w_kvb.reshape(KV_LORA, B_HEADS, B_NOPE + B_VDIM)
    wk_p = jnp.concatenate([wkv[:, :, :B_NOPE], jnp.zeros((KV_LORA, B_HEADS, B_QK_PAD - B_NOPE), w_kvb.dtype)],
                           axis=2).reshape(KV_LORA, B_HEADS * B_QK_PAD).astype(BF16)
    wv = jnp.concatenate([wkv[:, :, B_NOPE:], jnp.zeros((KV_LORA, B_HEADS, B_V_ROWS - B_VDIM), w_kvb.dtype)],
                         axis=2).reshape(KV_LORA, B_HEADS * B_V_ROWS)
    wvt_p = wv.T.astype(BF16)
    vbias = jnp.tile(jnp.concatenate([jnp.zeros((B_VDIM,), F32), jnp.ones((B_V_ROWS - B_VDIM,), F32)]),
                     B_HEADS).reshape(B_HEADS * B_V_ROWS, 1)
    return w_in_p, wq_p, wk_p, wvt_p, vbias


def _rope_table(s):
    half = B_ROPE // 2
    inv_freq = ROPE_THETA ** (-jnp.arange(half, dtype=F32) / half)
    ang = jnp.arange(s, dtype=F32)[:, None] * inv_freq[None, :]
    cos, sin = jnp.cos(ang), jnp.sin(ang)
    return jnp.concatenate([jnp.ones((s, B_NOPE), F32), cos, cos, -sin, sin], axis=1)


def _mla_kernel(q_ref, k_ref, vt_ref, z_ref, o_ref, s0_ref, s1_ref, p0_ref, p1_ref, *, q_tile, q_sub, kv_tile):
    kv_chunks = kv_tile // vt_ref.shape[3]
    n_kv = vt_ref.shape[1] // kv_chunks
    n_q = q_ref.shape[2] // q_sub
    s_bufs, p_bufs = (s0_ref, s1_ref), (p0_ref, p1_ref)
    chains = [(hh, sub) for sub in range(q_sub) for hh in range(B_HEAD_PAIR)]
    m_init = (jnp.full((1, q_tile), NEG_BIG, F32),) * len(chains)

    def q_rows(qt, sub):
        return pl.ds(pl.multiple_of((qt * q_sub + sub) * q_tile, q_tile), q_tile)

    def load_q(qt):
        return [q_ref[0, hh, qt * q_sub + sub] for hh, sub in chains]

    def score_chain(qc, j, slot, c):
        kc = k_ref[0, chains[c][0], j * kv_tile:(j + 1) * kv_tile, :]
        s_bufs[slot][c] = _dot(kc, qc[c])

    def scores(qc, j, slot):
        for c in range(len(chains)):
            score_chain(qc, j, slot, c)

    def exp_chain(slot, m, c):
        m_new = jnp.maximum(m[c], jnp.max(s_bufs[slot][c], axis=0, keepdims=True))
        p_bufs[slot][c] = jnp.exp2(s_bufs[slot][c] - m_new).astype(BF16)
        return m_new, jnp.exp2(m[c] - m_new)

    def exps(slot, m):
        res = [exp_chain(slot, m, c) for c in range(len(chains))]
        return tuple(r[0] for r in res), tuple(r[1] for r in res)

    def value_chain(j, slot, acc, alpha, c):
        rows = slice(chains[c][0] * B_V_ROWS, (chains[c][0] + 1) * B_V_ROWS)
        vt = jnp.concatenate([vt_ref[0, j * kv_chunks + i, rows, :] for i in range(kv_chunks)], axis=1)
        return acc[c] * alpha[c] + _dot(vt, p_bufs[slot][c])

    def step(qc, j_s, j_v, acc, alpha, m):
        s_slot, e_slot, v_slot = j_s % 2, (j_s - 1) % 2, j_v % 2
        acc_out, m_out, alpha_out = [], [], []
        for c in range(len(chains)):
            score_chain(qc, j_s % n_kv, s_slot, c)
            acc_out.append(value_chain(j_v, v_slot, acc, alpha, c))
            m_new, a_new = exp_chain(e_slot, m, c)
            m_out.append(m_new)
            alpha_out.append(a_new)
        return tuple(acc_out), tuple(m_out), tuple(alpha_out)

    def tile_body(qt, carry):
        m, alpha = carry
        qc = load_q(qt)
        qc_next = load_q(jnp.minimum(qt + 1, n_q - 1))
        acc = (jnp.zeros((B_V_ROWS, q_tile), F32),) * len(chains)
        for t in range(2, n_kv):
            acc, m, alpha = step(qc, t, t - 2, acc, alpha, m)
        acc, m, alpha = step(qc_next, n_kv, n_kv - 2, acc, alpha, m)
        acc, m_next, alpha_next = step(qc_next, n_kv + 1, n_kv - 1, acc, alpha, m_init)
        for sub in range(q_sub):
            o_t = jnp.concatenate([acc[c][:B_VDIM] / acc[c][B_VDIM:B_VDIM + 1]
                                   for c, (_, s_) in enumerate(chains) if s_ == sub], axis=0)
            z = z_ref[0, q_rows(qt, sub), :].astype(F32)
            o_ref[0, q_rows(qt, sub), :] = (o_t.T * (z * jax.nn.sigmoid(z))).astype(BF16)
        return m_next, alpha_next

    qc0 = load_q(0)
    scores(qc0, 0, 0)
    m, alpha = exps(0, m_init)
    scores(qc0, 1, 1)
    unroll = max(u for u in (4, 2, 1) if n_q % u == 0)

    def tiles(i, carry):
        for u in range(unroll):
            carry = tile_body(unroll * i + u, carry)
        return carry

    lax.fori_loop(0, n_q // unroll, tiles, (m, alpha))


def _mla_attn(q_t, k, vt, z):
    b, _, s, _ = k.shape
    n_chunks, kv_chunk = vt.shape[1], vt.shape[3]
    kv_tile = kv_chunk * (MLA_KV_CHUNKS if n_chunks % (2 * MLA_KV_CHUNKS) == 0 else 1)
    n_kv = s // kv_tile
    assert n_kv >= 2 and n_kv % 2 == 0
    q_block = min(MLA_Q_BLOCK, s)
    q_tile = q_t.shape[-1]
    out_w = B_HEAD_PAIR * B_VDIM
    q_sub = MLA_Q_SUB if q_block % (MLA_Q_SUB * q_tile) == 0 else 1
    s_buf = pltpu.VMEM((B_HEAD_PAIR * q_sub, kv_tile, q_tile), F32)
    p_buf = pltpu.VMEM((B_HEAD_PAIR * q_sub, kv_tile, q_tile), BF16)
    return pl.pallas_call(
        functools.partial(_mla_kernel, q_tile=q_tile, q_sub=q_sub, kv_tile=kv_tile),
        grid=(b, B_HEADS // B_HEAD_PAIR, s // q_block),
        in_specs=[pl.BlockSpec((1, B_HEAD_PAIR, q_block // q_tile, B_QK_PAD, q_tile),
                               lambda bi, hp, qi: (bi, hp, qi, 0, 0)),
                  pl.BlockSpec((1, B_HEAD_PAIR, s, B_QK_PAD), lambda bi, hp, qi: (bi, hp, 0, 0)),
                  pl.BlockSpec((1, n_chunks, B_HEAD_PAIR * B_V_ROWS, kv_chunk), lambda bi, hp, qi: (bi, 0, hp, 0)),
                  pl.BlockSpec((1, q_block, out_w), lambda bi, hp, qi: (bi, qi, hp))],
        out_specs=pl.BlockSpec((1, q_block, out_w), lambda bi, hp, qi: (bi, qi, hp)),
        out_shape=jax.ShapeDtypeStruct((b, s, B_WIDTH), BF16),
        scratch_shapes=[s_buf, s_buf, p_buf, p_buf],
        compiler_params=_params("parallel", "parallel", "arbitrary"),
    )(q_t, k, vt, z)


def kernel(x, p, norm_g, a_w_in, a_sink, a_w_out, b_w_in, b_q_norm, b_w_qb, b_kv_norm, b_w_kvb, b_w_out,
           ple_w, ple_norm_g, ple_w_gate, final_norm_g):
    depth = p.shape[0]
    s = x.shape[1]
    rot = _rope_table(s)
    h = x
    for i in range(depth):
        j = i // 2
        if i % 2 == 0:
            q, k_exp, vt, z = _pre_a(h, norm_g[i], a_w_in[j])
            og = _win_attn(q, k_exp, vt, z, a_sink[j])
            w_out = a_w_out[j]
        else:
            w_in_p, wq_p, wk_p, wvt_p, vbias = _prep_b_weights(b_w_in[j], b_w_qb[j], b_w_kvb[j])
            q, k, vt, z = _pre_b(h, norm_g[i], w_in_p, b_q_norm[j], wq_p, b_kv_norm[j], wk_p, wvt_p, vbias, rot)
            og = _mla_attn(q, k, vt, z)
            w_out = b_w_out[j]
        h = _post(og, h, p, i, w_out, ple_norm_g[i], ple_w_gate[i], ple_w[i], final_norm_g, final=(i == depth - 1))
    return h
```

```python
import functools

import jax
import jax.numpy as jnp
from jax import lax
from jax.experimental import pallas as pl
from jax.experimental.pallas import tpu as pltpu

F32 = jnp.float32
BF16 = jnp.bfloat16

EPS = 1e-6
NEG_BIG = -1e30
LOG2E = 1.4426950408889634

A_HEADS = 16
A_KV_HEADS = 4
A_GROUP = A_HEADS // A_KV_HEADS
A_HEAD_DIM = 64
A_WIDTH = A_HEADS * A_HEAD_DIM
A_KV_WIDTH = A_KV_HEADS * A_HEAD_DIM
WINDOW = 128
BLOCK = 128
A_V_ROWS = 80
WIN_Q_BLOCK = 256
MASK_DIST = 1e33

B_HEADS = 16
B_NOPE = 64
B_ROPE = 32
B_VDIM = 64
B_WIDTH = B_HEADS * B_VDIM
Q_LORA = 384
KV_LORA = 256
ROPE_THETA = 10000.0
B_QK_PAD = 128
B_V_ROWS = 80
B_HEAD_PAIR = 2

ROW_TILE = 1024
MLA_Q_BLOCK = 8192
MLA_Q_TILE = 256
PRE_B_TILE = 1024
MLA_KV_TILE = 512

VMEM_LIMIT = 56 * 1024 * 1024


def _rms(x, g):
    ms = jnp.mean(x * x, axis=-1, keepdims=True)
    return x * lax.rsqrt(ms + EPS) * g


def _dot(a, b):
    return jnp.dot(a, b, preferred_element_type=F32)


def _dot_nt(a, b):
    return lax.dot_general(a, b, (((1,), (1,)), ((), ())), preferred_element_type=F32)


def _params(*sem):
    return pltpu.CompilerParams(dimension_semantics=sem, vmem_limit_bytes=VMEM_LIMIT)


def _const_spec(shape):
    nd = len(shape)
    return pl.BlockSpec(shape, lambda *_: (0,) * nd)


def _pre_a_kernel(x_ref, g_ref, w_ref, wqt_ref, wvt_ref, vb_ref, qt_ref, k_ref, vt_ref, z_ref):
    u = _rms(x_ref[0], g_ref[...]).astype(BF16)
    qs = (A_HEAD_DIM ** -0.5) * LOG2E
    o1, o2, o3 = A_WIDTH, A_WIDTH + A_KV_WIDTH, A_WIDTH + 2 * A_KV_WIDTH
    q_t = (_dot_nt(wqt_ref[...], u) * qs).astype(BF16)
    for hp in range(A_WIDTH // 128):
        for c in range(q_t.shape[1] // WIN_Q_BLOCK):
            qt_ref[0, hp, c] = q_t[hp * 128:(hp + 1) * 128, c * WIN_Q_BLOCK:(c + 1) * WIN_Q_BLOCK]
    k = _dot(u, w_ref[:, o1:o2])
    lane = lax.broadcasted_iota(jnp.int32, (k.shape[0], 2 * A_HEAD_DIM), 1)
    low = lane < A_HEAD_DIM
    for kv in range(A_KV_HEADS):
        blk = k[:, (kv // 2) * 128:(kv // 2 + 1) * 128]
        swapped = pltpu.roll(blk, A_HEAD_DIM, 1)
        first, second = (blk, swapped) if kv % 2 == 0 else (swapped, blk)
        k_ref[0, :, kv * 256:kv * 256 + 128] = jnp.where(low, first, 0.0).astype(BF16)
        k_ref[0, :, kv * 256 + 128:(kv + 1) * 256] = jnp.where(low, 0.0, second).astype(BF16)
    vt_ref[0] = (_dot_nt(wvt_ref[...], u) + vb_ref[...]).astype(BF16)
    z_ref[0] = _dot(u, w_ref[:, o3:]).astype(BF16)


def _pre_a(h, g, w_in):
    b, s, d = h.shape
    tm = min(ROW_TILE, s)
    n_in = w_in.shape[1]
    o2, o3 = A_WIDTH + A_KV_WIDTH, A_WIDTH + 2 * A_KV_WIDTH
    pad = A_V_ROWS - A_HEAD_DIM
    wv = w_in[:, o2:o3].reshape(d, A_KV_HEADS, A_HEAD_DIM)
    wvt = jnp.concatenate([wv, jnp.zeros((d, A_KV_HEADS, pad), w_in.dtype)], axis=2)
    wvt = wvt.reshape(d, A_KV_HEADS * A_V_ROWS).T.astype(BF16)
    vbias = jnp.tile(jnp.concatenate([jnp.zeros((A_HEAD_DIM,), F32), jnp.ones((pad,), F32)]),
                     A_KV_HEADS).reshape(A_KV_HEADS * A_V_ROWS, 1)
    vrows = A_KV_HEADS * A_V_ROWS
    n_pairs = A_WIDTH // 128
    assert tm % WIN_Q_BLOCK == 0
    row = lambda w: pl.BlockSpec((1, tm, w), lambda bi, i: (bi, i, 0))
    return pl.pallas_call(
        _pre_a_kernel,
        grid=(b, s // tm),
        in_specs=[row(d), _const_spec((1, d)), _const_spec((d, n_in)), _const_spec((A_WIDTH, d)),
                  _const_spec((vrows, d)), _const_spec((vrows, 1))],
        out_specs=[pl.BlockSpec((1, n_pairs, tm // WIN_Q_BLOCK, 128, WIN_Q_BLOCK), lambda bi, i: (bi, 0, i, 0, 0)),
                   row(A_KV_HEADS * 256), pl.BlockSpec((1, vrows, tm), lambda bi, i: (bi, 0, i)),
                   row(A_WIDTH)],
        out_shape=[
            jax.ShapeDtypeStruct((b, n_pairs, s // WIN_Q_BLOCK, 128, WIN_Q_BLOCK), BF16),
            jax.ShapeDtypeStruct((b, s, A_KV_HEADS * 256), BF16),
            jax.ShapeDtypeStruct((b, vrows, s), BF16),
            jax.ShapeDtypeStruct((b, s, A_WIDTH), BF16),
        ],
        compiler_params=_params("parallel", "parallel"),
    )(h, g.reshape(1, d), w_in.astype(BF16), w_in[:, :A_WIDTH].T.astype(BF16), wvt, vbias)


def _win_kernel(sink_ref, q_ref, kp_ref, kc_ref, kn_ref, vp_ref, vc_ref, vn_ref, z_ref, o_ref,
                s0_ref, s1_ref, p0_ref, p1_ref, *, seq):
    i = pl.program_id(1)
    s_bufs, p_bufs = (s0_ref, s1_ref), (p0_ref, p1_ref)
    n_keys = WIN_Q_BLOCK + 2 * BLOCK
    n_live = 3 * BLOCK
    halves = range(WIN_Q_BLOCK // BLOCK)
    live = [(slice(j * BLOCK, j * BLOCK + n_live), slice(j * BLOCK, (j + 1) * BLOCK)) for j in halves]
    r = lax.broadcasted_iota(jnp.int32, (n_live, BLOCK), 0)
    c = lax.broadcasted_iota(jnp.int32, (n_live, BLOCK), 1)
    rel = r - BLOCK - c
    neg_dist = []
    for j in halves:
        k_pos = i * WIN_Q_BLOCK + (j - 1) * BLOCK + r
        valid = (jnp.abs(rel) <= WINDOW) & (k_pos >= 0) & (k_pos < seq)
        neg_dist.append(jnp.where(valid, -jnp.abs(rel).astype(F32), -MASK_DIST))
    for p_buf in p_bufs:
        for rows, lanes in live:
            dead = slice(0, BLOCK) if rows.start > 0 else slice(n_live, n_keys)
            p_buf[dead, lanes] = jnp.zeros((BLOCK, BLOCK), BF16)

    def scores(h, slot):
        kv, hp = h // A_GROUP, h // 2
        cols = slice(kv * 256 + (h % 2) * 128, kv * 256 + (h % 2 + 1) * 128)
        k_sel = jnp.concatenate([kp_ref[0, :, cols], kc_ref[0, :, cols], kn_ref[0, :, cols]], axis=0)
        slope = 2.0 ** (-8.0 * (h + 1) / A_HEADS) * LOG2E
        s = _dot(k_sel, q_ref[0, hp, 0])
        tmax = []
        for j, (rows, lanes) in enumerate(live):
            part = s[rows, lanes] + slope * neg_dist[j]
            s_bufs[slot][rows, lanes] = part
            tmax.append(jnp.max(part, axis=0, keepdims=True))
        return jnp.concatenate(tmax, axis=1)

    def exps(h, slot, tmax):
        m = jnp.maximum(tmax, sink_ref[h] * LOG2E)
        for rows, lanes in live:
            p_bufs[slot][rows, lanes] = jnp.exp2(s_bufs[slot][rows, lanes] - m[:, lanes]).astype(BF16)
        return m

    def values(h, slot, m):
        kv = h // A_GROUP
        rows = slice(kv * A_V_ROWS, (kv + 1) * A_V_ROWS)
        vt = jnp.concatenate([vp_ref[0, rows, :], vc_ref[0, rows, :], vn_ref[0, rows, :]], axis=1)
        acc = _dot(vt, p_bufs[slot][...])
        denom = acc[A_HEAD_DIM:A_HEAD_DIM + 1] + jnp.exp2(sink_ref[h] * LOG2E - m)
        return acc[:A_HEAD_DIM] / denom

    tmax, m, pending = {}, {}, {}
    for t in range(A_HEADS + 2):
        if t < A_HEADS:
            tmax[t] = scores(t, t % 2)
        if t >= 2:
            h = t - 2
            pending[h] = values(h, h % 2, m.pop(h))
            if h % 2 == 1:
                o_t = jnp.concatenate([pending.pop(h - 1), pending.pop(h)], axis=0)
                lanes = slice((h // 2) * 128, (h // 2 + 1) * 128)
                z = z_ref[0, :, lanes].astype(F32)
                o_ref[0, :, lanes] = (o_t.T * (z * jax.nn.sigmoid(z))).astype(BF16)
        if 1 <= t <= A_HEADS:
            m[t - 1] = exps(t - 1, (t - 1) % 2, tmax.pop(t - 1))


def _win_attn(q_t, k_exp, vt, z, sink):
    b, s, _ = k_exp.shape
    qb = WIN_Q_BLOCK
    assert s % qb == 0
    per = qb // BLOCK
    nb = s // BLOCK
    vrows = vt.shape[1]
    kw = k_exp.shape[-1]
    prev_i = lambda i: jnp.maximum(per * i - 1, 0)
    next_i = lambda i: jnp.minimum(per * i + per, nb - 1)
    cur = lambda w: pl.BlockSpec((1, qb, w), lambda bi, i: (bi, i, 0))
    s_buf = pltpu.VMEM((qb + 2 * BLOCK, qb), F32)
    p_buf = pltpu.VMEM((qb + 2 * BLOCK, qb), BF16)
    return pl.pallas_call(
        functools.partial(_win_kernel, seq=s),
        grid=(b, s // qb),
        in_specs=[pl.BlockSpec(memory_space=pltpu.SMEM),
                  pl.BlockSpec((1, q_t.shape[1], 1, 128, qb), lambda bi, i: (bi, 0, i, 0, 0)),
                  pl.BlockSpec((1, BLOCK, kw), lambda bi, i: (bi, prev_i(i), 0)),
                  cur(kw),
                  pl.BlockSpec((1, BLOCK, kw), lambda bi, i: (bi, next_i(i), 0)),
                  pl.BlockSpec((1, vrows, BLOCK), lambda bi, i: (bi, 0, prev_i(i))),
                  pl.BlockSpec((1, vrows, qb), lambda bi, i: (bi, 0, i)),
                  pl.BlockSpec((1, vrows, BLOCK), lambda bi, i: (bi, 0, next_i(i))),
                  cur(A_WIDTH)],
        out_specs=cur(A_WIDTH),
        out_shape=jax.ShapeDtypeStruct((b, s, A_WIDTH), BF16),
        scratch_shapes=[s_buf, s_buf, p_buf, p_buf],
        compiler_params=_params("parallel", "parallel"),
    )(sink.astype(F32), q_t, k_exp, k_exp, k_exp, vt, vt, vt, z)


def _post_kernel(og_ref, h_ref, p_ref, wo_ref, pg_ref, wg_ref, pw_ref, fg_ref, o_ref, *, final):
    h1 = h_ref[0] + _dot(og_ref[0], wo_ref[...])
    gate = jax.nn.sigmoid(_dot(_rms(h1, pg_ref[...]).astype(BF16), wg_ref[...]))
    h2 = h1 + _dot(p_ref[0, 0].astype(BF16), pw_ref[...]) * gate
    if final:
        h2 = _rms(h2, fg_ref[...])
    o_ref[0] = h2


def _post(og, h, p, layer, w_out, ple_g, w_gate, ple_w, final_g, final):
    b, s, d = h.shape
    tm = min(ROW_TILE, s)
    pd = p.shape[-1]
    row = lambda w: pl.BlockSpec((1, tm, w), lambda bi, i: (bi, i, 0))
    return pl.pallas_call(
        functools.partial(_post_kernel, final=final),
        grid=(b, s // tm),
        in_specs=[row(og.shape[-1]), row(d), pl.BlockSpec((1, 1, tm, pd), lambda bi, i: (layer, bi, i, 0)),
                  _const_spec(w_out.shape), _const_spec((1, d)),
                  _const_spec((d, d)), _const_spec((pd, d)), _const_spec((1, d))],
        out_specs=row(d),
        out_shape=jax.ShapeDtypeStruct((b, s, d), F32),
        compiler_params=_params("parallel", "parallel"),
    )(og, h, p, w_out.astype(BF16), ple_g.reshape(1, d), w_gate.astype(BF16), ple_w.astype(BF16),
      final_g.reshape(1, d))


def _pre_b_kernel(x_ref, g_ref, w_ref, qn_ref, wqt_ref, kn_ref, wk_ref, wvt_ref, vb_ref, r_ref, rt_ref,
                  qt_ref, k_ref, vt_ref, z_ref):
    u = _rms(x_ref[0], g_ref[...]).astype(BF16)
    o1, o2, o3 = Q_LORA, Q_LORA + KV_LORA, Q_LORA + KV_LORA + B_QK_PAD
    rot = r_ref[...]
    cq = _rms(_dot(u, w_ref[:, :o1]), qn_ref[...]).astype(BF16)
    qs = ((B_NOPE + B_ROPE) ** -0.5) * LOG2E
    q_t = (_dot_nt(wqt_ref[...], cq) * jnp.tile(rt_ref[...] * qs, (B_HEADS, 1))).astype(BF16)
    for hh in range(B_HEADS):
        for c in range(q_t.shape[1] // MLA_Q_TILE):
            qt_ref[0, hh, c] = q_t[hh * B_QK_PAD:(hh + 1) * B_QK_PAD, c * MLA_Q_TILE:(c + 1) * MLA_Q_TILE]
    ckv = _rms(_dot(u, w_ref[:, o1:o2]), kn_ref[...]).astype(BF16)
    y = _dot(u, w_ref[:, o2:o3]) * rot
    lane = lax.broadcasted_iota(jnp.int32, y.shape, 1)
    both = pltpu.roll(y, 32, 1) + pltpu.roll(y, 96, 1)
    kr = y + jnp.where(lane >= B_NOPE, both, 0.0)
    k = (_dot(ckv, wk_ref[...]) + jnp.tile(kr, (1, B_HEADS))).astype(BF16)
    for hh in range(B_HEADS):
        k_ref[0, hh] = k[:, hh * B_QK_PAD:(hh + 1) * B_QK_PAD]
    vt_all = (_dot_nt(wvt_ref[...], ckv) + vb_ref[...]).astype(BF16)
    for c in range(vt_ref.shape[1]):
        vt_ref[0, c] = vt_all[:, c * MLA_KV_TILE:(c + 1) * MLA_KV_TILE]
    z_ref[0] = _dot(u, w_ref[:, o3:]).astype(BF16)


def _pre_b(h, g, w_in_p, q_norm, wq_p, kv_norm, wk_p, wvt_p, vbias, rot):
    b, s, d = h.shape
    tm = min(PRE_B_TILE, s)
    chunk = min(MLA_KV_TILE, tm)
    assert tm % MLA_Q_TILE == 0 and tm % chunk == 0
    row = lambda w: pl.BlockSpec((1, tm, w), lambda bi, i: (bi, i, 0))
    vrows = B_HEADS * B_V_ROWS
    return pl.pallas_call(
        _pre_b_kernel,
        grid=(b, s // tm),
        in_specs=[row(d), _const_spec((1, d)), _const_spec(w_in_p.shape), _const_spec((1, Q_LORA)),
                  _const_spec((wq_p.shape[1], wq_p.shape[0])), _const_spec((1, KV_LORA)), _const_spec(wk_p.shape),
                  _const_spec(wvt_p.shape), _const_spec((vrows, 1)),
                  pl.BlockSpec((tm, B_QK_PAD), lambda bi, i: (i, 0)),
                  pl.BlockSpec((B_QK_PAD, tm), lambda bi, i: (0, i))],
        out_specs=[pl.BlockSpec((1, B_HEADS, tm // MLA_Q_TILE, B_QK_PAD, MLA_Q_TILE), lambda bi, i: (bi, 0, i, 0, 0)),
                   pl.BlockSpec((1, B_HEADS, tm, B_QK_PAD), lambda bi, i: (bi, 0, i, 0)),
                   pl.BlockSpec((1, tm // chunk, vrows, chunk), lambda bi, i: (bi, i, 0, 0)), row(B_WIDTH)],
        out_shape=[
            jax.ShapeDtypeStruct((b, B_HEADS, s // MLA_Q_TILE, B_QK_PAD, MLA_Q_TILE), BF16),
            jax.ShapeDtypeStruct((b, B_HEADS, s, B_QK_PAD), BF16),
            jax.ShapeDtypeStruct((b, s // chunk, vrows, chunk), BF16),
            jax.ShapeDtypeStruct((b, s, B_WIDTH), BF16),
        ],
        compiler_params=_params("parallel", "parallel"),
    )(h, g.reshape(1, d), w_in_p, q_norm.reshape(1, Q_LORA), wq_p.T, kv_norm.reshape(1, KV_LORA), wk_p,
      wvt_p, vbias, rot, rot.T)


def _prep_b_weights(w_in, w_qb, w_kvb):
    half = B_ROPE // 2
    d = w_in.shape[0]
    o1, o2, o3 = Q_LORA, Q_LORA + KV_LORA, Q_LORA + KV_LORA + B_ROPE
    kr = w_in[:, o2:o3]
    kr_swapped = jnp.concatenate([kr[:, half:], kr[:, :half]], axis=1)
    kr_block = jnp.concatenate([jnp.zeros((d, B_NOPE), w_in.dtype), kr, kr_swapped], axis=1)
    w_in_p = jnp.concatenate([w_in[:, :o2], kr_block, w_in[:, o3:]], axis=1).astype(BF16)

    wq = w_qb.reshape(Q_LORA, B_HEADS, B_NOPE + B_ROPE)
    q_rope = wq[:, :, B_NOPE:]
    q_rope_swapped = jnp.concatenate([q_rope[:, :, half:], q_rope[:, :, :half]], axis=2)
    wq_p = jnp.concatenate([wq[:, :, :B_NOPE], q_rope, q_rope_swapped], axis=2)
    wq_p = wq_p.reshape(Q_LORA, B_HEADS * B_QK_PAD).astype(BF16)

    wkv = w_kvb.reshape(KV_LORA, B_HEADS, B_NOPE + B_VDIM)
    wk_p = jnp.concatenate([wkv[:, :, :B_NOPE], jnp.zeros((KV_LORA, B_HEADS, B_QK_PAD - B_NOPE), w_kvb.dtype)],
                           axis=2).reshape(KV_LORA, B_HEADS * B_QK_PAD).astype(BF16)
    wv = jnp.concatenate([wkv[:, :, B_NOPE:], jnp.zeros((KV_LORA, B_HEADS, B_V_ROWS - B_VDIM), w_kvb.dtype)],
                         axis=2).reshape(KV_LORA, B_HEADS * B_V_ROWS)
    wvt_p = wv.T.astype(BF16)
    vbias = jnp.tile(jnp.concatenate([jnp.zeros((B_VDIM,), F32), jnp.ones((B_V_ROWS - B_VDIM,), F32)]),
                     B_HEADS).reshape(B_HEADS * B_V_ROWS, 1)
    return w_in_p, wq_p, wk_p, wvt_p, vbias


def _rope_table(s):
    half = B_ROPE // 2
    inv_freq = ROPE_THETA ** (-jnp.arange(half, dtype=F32) / half)
    ang = jnp.arange(s, dtype=F32)[:, None] * inv_freq[None, :]
    cos, sin = jnp.cos(ang), jnp.sin(ang)
    return jnp.concatenate([jnp.ones((s, B_NOPE), F32), cos, cos, -sin, sin], axis=1)


def _mla_kernel(q_ref, k_ref, vt_ref, z_ref, o_ref, s0_ref, s1_ref, p0_ref, p1_ref):
    n_kv, kv_tile = vt_ref.shape[1], vt_ref.shape[3]
    n_q, q_tile = q_ref.shape[2], q_ref.shape[4]
    s_bufs, p_bufs = (s0_ref, s1_ref), (p0_ref, p1_ref)
    heads = range(B_HEAD_PAIR)
    m_init = (jnp.full((1, q_tile), NEG_BIG, F32),) * B_HEAD_PAIR

    def q_rows(qt):
        return pl.ds(pl.multiple_of(qt * q_tile, q_tile), q_tile)

    def load_q(qt):
        return [q_ref[0, hh, qt] for hh in heads]

    def scores(q, j, slot, hh):
        kc = k_ref[0, hh, j * kv_tile:(j + 1) * kv_tile, :]
        s_bufs[slot][hh] = _dot(kc, q[hh])

    def exps(slot, m, hh):
        m_new = jnp.maximum(m[hh], jnp.max(s_bufs[slot][hh], axis=0, keepdims=True))
        p_bufs[slot][hh] = jnp.exp2(s_bufs[slot][hh] - m_new).astype(BF16)
        return m_new, jnp.exp2(m[hh] - m_new)

    def values(j, slot, acc, alpha, hh):
        vt = vt_ref[0, j, hh * B_V_ROWS:(hh + 1) * B_V_ROWS, :]
        return acc[hh] * alpha[hh] + _dot(vt, p_bufs[slot][hh])

    def step(q, j_s, j_v, acc, alpha, m):
        s_slot, e_slot, v_slot = j_s % 2, (j_s - 1) % 2, j_v % 2
        acc_out, m_out, alpha_out = [], [], []
        for hh in heads:
            scores(q, j_s % n_kv, s_slot, hh)
            acc_out.append(values(j_v, v_slot, acc, alpha, hh))
            m_new, a_new = exps(e_slot, m, hh)
            m_out.append(m_new)
            alpha_out.append(a_new)
        return tuple(acc_out), tuple(m_out), tuple(alpha_out)

    def tile_body(qt, carry):
        m, alpha = carry
        q = load_q(qt)
        q_next = load_q(jnp.minimum(qt + 1, n_q - 1))
        acc = (jnp.zeros((B_V_ROWS, q_tile), F32),) * B_HEAD_PAIR
        for t in range(2, n_kv):
            acc, m, alpha = step(q, t, t - 2, acc, alpha, m)
        acc, m, alpha = step(q_next, n_kv, n_kv - 2, acc, alpha, m)
        acc, m_next, alpha_next = step(q_next, n_kv + 1, n_kv - 1, acc, alpha, m_init)
        o_t = jnp.concatenate([acc[hh][:B_VDIM] / acc[hh][B_VDIM:B_VDIM + 1] for hh in heads], axis=0)
        z = z_ref[0, q_rows(qt), :].astype(F32)
        o_ref[0, q_rows(qt), :] = (o_t.T * (z * jax.nn.sigmoid(z))).astype(BF16)
        return m_next, alpha_next

    q0 = load_q(0)
    for hh in heads:
        scores(q0, 0, 0, hh)
    first = [exps(0, m_init, hh) for hh in heads]
    for hh in heads:
        scores(q0, 1, 1, hh)
    unroll = max(u for u in (4, 2, 1) if n_q % u == 0)

    def tiles(i, carry):
        for u in range(unroll):
            carry = tile_body(unroll * i + u, carry)
        return carry

    lax.fori_loop(0, n_q // unroll, tiles, (tuple(f[0] for f in first), tuple(f[1] for f in first)))


def _mla_attn(q_t, k, vt, z):
    b, _, s, _ = k.shape
    n_kv, kv_tile = vt.shape[1], vt.shape[3]
    assert n_kv >= 2 and n_kv % 2 == 0
    q_block = min(MLA_Q_BLOCK, s)
    q_tile = q_t.shape[-1]
    out_w = B_HEAD_PAIR * B_VDIM
    s_buf = pltpu.VMEM((B_HEAD_PAIR, kv_tile, q_tile), F32)
    p_buf = pltpu.VMEM((B_HEAD_PAIR, kv_tile, q_tile), BF16)
    return pl.pallas_call(
        _mla_kernel,
        grid=(b, B_HEADS // B_HEAD_PAIR, s // q_block),
        in_specs=[pl.BlockSpec((1, B_HEAD_PAIR, q_block // q_tile, B_QK_PAD, q_tile),
                               lambda bi, hp, qi: (bi, hp, qi, 0, 0)),
                  pl.BlockSpec((1, B_HEAD_PAIR, s, B_QK_PAD), lambda bi, hp, qi: (bi, hp, 0, 0)),
                  pl.BlockSpec((1, n_kv, B_HEAD_PAIR * B_V_ROWS, kv_tile), lambda bi, hp, qi: (bi, 0, hp, 0)),
                  pl.BlockSpec((1, q_block, out_w), lambda bi, hp, qi: (bi, qi, hp))],
        out_specs=pl.BlockSpec((1, q_block, out_w), lambda bi, hp, qi: (bi, qi, hp)),
        out_shape=jax.ShapeDtypeStruct((b, s, B_WIDTH), BF16),
        scratch_shapes=[s_buf, s_buf, p_buf, p_buf],
        compiler_params=_params("parallel", "parallel", "arbitrary"),
    )(q_t, k, vt, z)


def kernel(x, p, norm_g, a_w_in, a_sink, a_w_out, b_w_in, b_q_norm, b_w_qb, b_kv_norm, b_w_kvb, b_w_out,
           ple_w, ple_norm_g, ple_w_gate, final_norm_g):
    depth = p.shape[0]
    s = x.shape[1]
    rot = _rope_table(s)
    h = x
    for i in range(depth):
        j = i // 2
        if i % 2 == 0:
            q, k_exp, vt, z = _pre_a(h, norm_g[i], a_w_in[j])
            og = _win_attn(q, k_exp, vt, z, a_sink[j])
            w_out = a_w_out[j]
        else:
            w_in_p, wq_p, wk_p, wvt_p, vbias = _prep_b_weights(b_w_in[j], b_w_qb[j], b_w_kvb[j])
            q, k, vt, z = _pre_b(h, norm_g[i], w_in_p, b_q_norm[j], wq_p, b_kv_norm[j], wk_p, wvt_p, vbias, rot)
            og = _mla_attn(q, k, vt, z)
            w_out = b_w_out[j]
        h = _post(og, h, p, i, w_out, ple_norm_g[i], ple_w_gate[i], ple_w[i], final_norm_g, final=(i == depth - 1))
    return h
```

```python
import functools

import jax
import jax.numpy as jnp
from jax import lax
from jax.experimental import pallas as pl
from jax.experimental.pallas import tpu as pltpu

F32 = jnp.float32
BF16 = jnp.bfloat16

EPS = 1e-6
NEG_BIG = -1e30
LOG2E = 1.4426950408889634

A_HEADS = 16
A_KV_HEADS = 4
A_GROUP = A_HEADS // A_KV_HEADS
A_HEAD_DIM = 64
A_WIDTH = A_HEADS * A_HEAD_DIM
A_KV_WIDTH = A_KV_HEADS * A_HEAD_DIM
WINDOW = 128
BLOCK = 128
A_V_ROWS = 80
WIN_Q_BLOCK = 256
MASK_DIST = 1e33

B_HEADS = 16
B_NOPE = 64
B_ROPE = 32
B_VDIM = 64
B_WIDTH = B_HEADS * B_VDIM
Q_LORA = 384
KV_LORA = 256
ROPE_THETA = 10000.0
B_QK_PAD = 128
B_V_ROWS = 80
B_HEAD_PAIR = 2

ROW_TILE = 1024
MLA_Q_BLOCK = 8192
MLA_Q_TILE = 256
PRE_B_TILE = 1024
MLA_KV_TILE = 512

VMEM_LIMIT = 56 * 1024 * 1024


def _rms(x, g):
    ms = jnp.mean(x * x, axis=-1, keepdims=True)
    return x * lax.rsqrt(ms + EPS) * g


def _dot(a, b):
    return jnp.dot(a, b, preferred_element_type=F32)


def _dot_nt(a, b):
    return lax.dot_general(a, b, (((1,), (1,)), ((), ())), preferred_element_type=F32)


def _params(*sem):
    return pltpu.CompilerParams(dimension_semantics=sem, vmem_limit_bytes=VMEM_LIMIT)


def _const_spec(shape):
    nd = len(shape)
    return pl.BlockSpec(shape, lambda *_: (0,) * nd)


def _pre_a_kernel(x_ref, g_ref, w_ref, wqt_ref, wvt_ref, vb_ref, qt_ref, k_ref, vt_ref, z_ref):
    u = _rms(x_ref[0], g_ref[...]).astype(BF16)
    qs = (A_HEAD_DIM ** -0.5) * LOG2E
    o1, o2, o3 = A_WIDTH, A_WIDTH + A_KV_WIDTH, A_WIDTH + 2 * A_KV_WIDTH
    q_t = (_dot_nt(wqt_ref[...], u) * qs).astype(BF16)
    for hp in range(A_WIDTH // 128):
        for c in range(q_t.shape[1] // WIN_Q_BLOCK):
            qt_ref[0, hp, c] = q_t[hp * 128:(hp + 1) * 128, c * WIN_Q_BLOCK:(c + 1) * WIN_Q_BLOCK]
    k = _dot(u, w_ref[:, o1:o2])
    lane = lax.broadcasted_iota(jnp.int32, (k.shape[0], 2 * A_HEAD_DIM), 1)
    low = lane < A_HEAD_DIM
    for kv in range(A_KV_HEADS):
        blk = k[:, (kv // 2) * 128:(kv // 2 + 1) * 128]
        swapped = pltpu.roll(blk, A_HEAD_DIM, 1)
        first, second = (blk, swapped) if kv % 2 == 0 else (swapped, blk)
        k_ref[0, :, kv * 256:kv * 256 + 128] = jnp.where(low, first, 0.0).astype(BF16)
        k_ref[0, :, kv * 256 + 128:(kv + 1) * 256] = jnp.where(low, 0.0, second).astype(BF16)
    vt_ref[0] = (_dot_nt(wvt_ref[...], u) + vb_ref[...]).astype(BF16)
    z_ref[0] = _dot(u, w_ref[:, o3:]).astype(BF16)


def _pre_a(h, g, w_in):
    b, s, d = h.shape
    tm = min(ROW_TILE, s)
    n_in = w_in.shape[1]
    o2, o3 = A_WIDTH + A_KV_WIDTH, A_WIDTH + 2 * A_KV_WIDTH
    pad = A_V_ROWS - A_HEAD_DIM
    wv = w_in[:, o2:o3].reshape(d, A_KV_HEADS, A_HEAD_DIM)
    wvt = jnp.concatenate([wv, jnp.zeros((d, A_KV_HEADS, pad), w_in.dtype)], axis=2)
    wvt = wvt.reshape(d, A_KV_HEADS * A_V_ROWS).T.astype(BF16)
    vbias = jnp.tile(jnp.concatenate([jnp.zeros((A_HEAD_DIM,), F32), jnp.ones((pad,), F32)]),
                     A_KV_HEADS).reshape(A_KV_HEADS * A_V_ROWS, 1)
    vrows = A_KV_HEADS * A_V_ROWS
    n_pairs = A_WIDTH // 128
    assert tm % WIN_Q_BLOCK == 0
    row = lambda w: pl.BlockSpec((1, tm, w), lambda bi, i: (bi, i, 0))
    return pl.pallas_call(
        _pre_a_kernel,
        grid=(b, s // tm),
        in_specs=[row(d), _const_spec((1, d)), _const_spec((d, n_in)), _const_spec((A_WIDTH, d)),
                  _const_spec((vrows, d)), _const_spec((vrows, 1))],
        out_specs=[pl.BlockSpec((1, n_pairs, tm // WIN_Q_BLOCK, 128, WIN_Q_BLOCK), lambda bi, i: (bi, 0, i, 0, 0)),
                   row(A_KV_HEADS * 256), pl.BlockSpec((1, vrows, tm), lambda bi, i: (bi, 0, i)),
                   row(A_WIDTH)],
        out_shape=[
            jax.ShapeDtypeStruct((b, n_pairs, s // WIN_Q_BLOCK, 128, WIN_Q_BLOCK), BF16),
            jax.ShapeDtypeStruct((b, s, A_KV_HEADS * 256), BF16),
            jax.ShapeDtypeStruct((b, vrows, s), BF16),
            jax.ShapeDtypeStruct((b, s, A_WIDTH), BF16),
        ],
        compiler_params=_params("parallel", "parallel"),
    )(h, g.reshape(1, d), w_in.astype(BF16), w_in[:, :A_WIDTH].T.astype(BF16), wvt, vbias)


def _win_kernel(sink_ref, q_ref, kp_ref, kc_ref, kn_ref, vp_ref, vc_ref, vn_ref, z_ref, o_ref,
                s0_ref, s1_ref, s2_ref, s3_ref, p0_ref, p1_ref, p2_ref, p3_ref, *, seq):
    i = pl.program_id(1)
    s_bufs, p_bufs = (s0_ref, s1_ref, s2_ref, s3_ref), (p0_ref, p1_ref, p2_ref, p3_ref)
    n_keys = WIN_Q_BLOCK + 2 * BLOCK
    n_live = 3 * BLOCK
    halves = range(WIN_Q_BLOCK // BLOCK)
    live = [(slice(j * BLOCK, j * BLOCK + n_live), slice(j * BLOCK, (j + 1) * BLOCK)) for j in halves]
    r = lax.broadcasted_iota(jnp.int32, (n_live, BLOCK), 0)
    c = lax.broadcasted_iota(jnp.int32, (n_live, BLOCK), 1)
    rel = r - BLOCK - c
    neg_dist = []
    for j in halves:
        k_pos = i * WIN_Q_BLOCK + (j - 1) * BLOCK + r
        valid = (jnp.abs(rel) <= WINDOW) & (k_pos >= 0) & (k_pos < seq)
        neg_dist.append(jnp.where(valid, -jnp.abs(rel).astype(F32), -MASK_DIST))
    for p_buf in p_bufs:
        for rows, lanes in live:
            dead = slice(0, BLOCK) if rows.start > 0 else slice(n_live, n_keys)
            p_buf[dead, lanes] = jnp.zeros((BLOCK, BLOCK), BF16)

    def scores(h, slot):
        kv, hp = h // A_GROUP, h // 2
        cols = slice(kv * 256 + (h % 2) * 128, kv * 256 + (h % 2 + 1) * 128)
        k_sel = jnp.concatenate([kp_ref[0, :, cols], kc_ref[0, :, cols], kn_ref[0, :, cols]], axis=0)
        slope = 2.0 ** (-8.0 * (h + 1) / A_HEADS) * LOG2E
        s = _dot(k_sel, q_ref[0, hp, 0])
        tmax = []
        for j, (rows, lanes) in enumerate(live):
            part = s[rows, lanes] + slope * neg_dist[j]
            s_bufs[slot][rows, lanes] = part
            tmax.append(jnp.max(part, axis=0, keepdims=True))
        return jnp.concatenate(tmax, axis=1)

    def exps(h, slot, tmax):
        m = jnp.maximum(tmax, sink_ref[h] * LOG2E)
        for rows, lanes in live:
            p_bufs[slot][rows, lanes] = jnp.exp2(s_bufs[slot][rows, lanes] - m[:, lanes]).astype(BF16)
        return m

    def values(h, slot, m):
        kv = h // A_GROUP
        rows = slice(kv * A_V_ROWS, (kv + 1) * A_V_ROWS)
        vt = jnp.concatenate([vp_ref[0, rows, :], vc_ref[0, rows, :], vn_ref[0, rows, :]], axis=1)
        acc = _dot(vt, p_bufs[slot][...])
        denom = acc[A_HEAD_DIM:A_HEAD_DIM + 1] + jnp.exp2(sink_ref[h] * LOG2E - m)
        return acc[:A_HEAD_DIM] / denom

    n_pairs = A_HEADS // 2
    tmax, m, pending = {}, {}, {}
    for t in range(n_pairs + 2):
        for c in range(2):
            if t < n_pairs:
                tmax[2 * t + c] = scores(2 * t + c, 2 * (t % 2) + c)
            if t >= 2:
                h = 2 * (t - 2) + c
                pending[h] = values(h, 2 * (t % 2) + c, m.pop(h))
        if t >= 2:
            h = 2 * (t - 2)
            o_t = jnp.concatenate([pending.pop(h), pending.pop(h + 1)], axis=0)
            lanes = slice((t - 2) * 128, (t - 1) * 128)
            z = z_ref[0, :, lanes].astype(F32)
            o_ref[0, :, lanes] = (o_t.T * (z * jax.nn.sigmoid(z))).astype(BF16)
        if 1 <= t <= n_pairs:
            for c in range(2):
                h = 2 * (t - 1) + c
                m[h] = exps(h, 2 * ((t - 1) % 2) + c, tmax.pop(h))


def _win_attn(q_t, k_exp, vt, z, sink):
    b, s, _ = k_exp.shape
    qb = WIN_Q_BLOCK
    assert s % qb == 0
    per = qb // BLOCK
    nb = s // BLOCK
    vrows = vt.shape[1]
    kw = k_exp.shape[-1]
    prev_i = lambda i: jnp.maximum(per * i - 1, 0)
    next_i = lambda i: jnp.minimum(per * i + per, nb - 1)
    cur = lambda w: pl.BlockSpec((1, qb, w), lambda bi, i: (bi, i, 0))
    s_buf = pltpu.VMEM((qb + 2 * BLOCK, qb), F32)
    p_buf = pltpu.VMEM((qb + 2 * BLOCK, qb), BF16)
    return pl.pallas_call(
        functools.partial(_win_kernel, seq=s),
        grid=(b, s // qb),
        in_specs=[pl.BlockSpec(memory_space=pltpu.SMEM),
                  pl.BlockSpec((1, q_t.shape[1], 1, 128, qb), lambda bi, i: (bi, 0, i, 0, 0)),
                  pl.BlockSpec((1, BLOCK, kw), lambda bi, i: (bi, prev_i(i), 0)),
                  cur(kw),
                  pl.BlockSpec((1, BLOCK, kw), lambda bi, i: (bi, next_i(i), 0)),
                  pl.BlockSpec((1, vrows, BLOCK), lambda bi, i: (bi, 0, prev_i(i))),
                  pl.BlockSpec((1, vrows, qb), lambda bi, i: (bi, 0, i)),
                  pl.BlockSpec((1, vrows, BLOCK), lambda bi, i: (bi, 0, next_i(i))),
                  cur(A_WIDTH)],
        out_specs=cur(A_WIDTH),
        out_shape=jax.ShapeDtypeStruct((b, s, A_WIDTH), BF16),
        scratch_shapes=[s_buf] * 4 + [p_buf] * 4,
        compiler_params=_params("parallel", "parallel"),
    )(sink.astype(F32), q_t, k_exp, k_exp, k_exp, vt, vt, vt, z)


def _post_kernel(og_ref, h_ref, p_ref, wo_ref, pg_ref, wg_ref, pw_ref, fg_ref, o_ref, *, final):
    h1 = h_ref[0] + _dot(og_ref[0], wo_ref[...])
    gate = jax.nn.sigmoid(_dot(_rms(h1, pg_ref[...]).astype(BF16), wg_ref[...]))
    h2 = h1 + _dot(p_ref[0, 0].astype(BF16), pw_ref[...]) * gate
    if final:
        h2 = _rms(h2, fg_ref[...])
    o_ref[0] = h2


def _post(og, h, p, layer, w_out, ple_g, w_gate, ple_w, final_g, final):
    b, s, d = h.shape
    tm = min(ROW_TILE, s)
    pd = p.shape[-1]
    row = lambda w: pl.BlockSpec((1, tm, w), lambda bi, i: (bi, i, 0))
    return pl.pallas_call(
        functools.partial(_post_kernel, final=final),
        grid=(b, s // tm),
        in_specs=[row(og.shape[-1]), row(d), pl.BlockSpec((1, 1, tm, pd), lambda bi, i: (layer, bi, i, 0)),
                  _const_spec(w_out.shape), _const_spec((1, d)),
                  _const_spec((d, d)), _const_spec((pd, d)), _const_spec((1, d))],
        out_specs=row(d),
        out_shape=jax.ShapeDtypeStruct((b, s, d), F32),
        compiler_params=_params("parallel", "parallel"),
    )(og, h, p, w_out.astype(BF16), ple_g.reshape(1, d), w_gate.astype(BF16), ple_w.astype(BF16),
      final_g.reshape(1, d))


def _pre_b_kernel(x_ref, g_ref, w_ref, qn_ref, wqt_ref, kn_ref, wk_ref, wvt_ref, vb_ref, r_ref, rt_ref,
                  qt_ref, k_ref, vt_ref, z_ref):
    u = _rms(x_ref[0], g_ref[...]).astype(BF16)
    o1, o2, o3 = Q_LORA, Q_LORA + KV_LORA, Q_LORA + KV_LORA + B_QK_PAD
    rot = r_ref[...]
    cq = _rms(_dot(u, w_ref[:, :o1]), qn_ref[...]).astype(BF16)
    qs = ((B_NOPE + B_ROPE) ** -0.5) * LOG2E
    q_t = (_dot_nt(wqt_ref[...], cq) * jnp.tile(rt_ref[...] * qs, (B_HEADS, 1))).astype(BF16)
    for hh in range(B_HEADS):
        for c in range(q_t.shape[1] // MLA_Q_TILE):
            qt_ref[0, hh, c] = q_t[hh * B_QK_PAD:(hh + 1) * B_QK_PAD, c * MLA_Q_TILE:(c + 1) * MLA_Q_TILE]
    ckv = _rms(_dot(u, w_ref[:, o1:o2]), kn_ref[...]).astype(BF16)
    y = _dot(u, w_ref[:, o2:o3]) * rot
    lane = lax.broadcasted_iota(jnp.int32, y.shape, 1)
    both = pltpu.roll(y, 32, 1) + pltpu.roll(y, 96, 1)
    kr = y + jnp.where(lane >= B_NOPE, both, 0.0)
    k = (_dot(ckv, wk_ref[...]) + jnp.tile(kr, (1, B_HEADS))).astype(BF16)
    for hh in range(B_HEADS):
        k_ref[0, hh] = k[:, hh * B_QK_PAD:(hh + 1) * B_QK_PAD]
    vt_all = (_dot_nt(wvt_ref[...], ckv) + vb_ref[...]).astype(BF16)
    for c in range(vt_ref.shape[1]):
        vt_ref[0, c] = vt_all[:, c * MLA_KV_TILE:(c + 1) * MLA_KV_TILE]
    z_ref[0] = _dot(u, w_ref[:, o3:]).astype(BF16)


def _pre_b(h, g, w_in_p, q_norm, wq_p, kv_norm, wk_p, wvt_p, vbias, rot):
    b, s, d = h.shape
    tm = min(PRE_B_TILE, s)
    chunk = min(MLA_KV_TILE, tm)
    assert tm % MLA_Q_TILE == 0 and tm % chunk == 0
    row = lambda w: pl.BlockSpec((1, tm, w), lambda bi, i: (bi, i, 0))
    vrows = B_HEADS * B_V_ROWS
    return pl.pallas_call(
        _pre_b_kernel,
        grid=(b, s // tm),
        in_specs=[row(d), _const_spec((1, d)), _const_spec(w_in_p.shape), _const_spec((1, Q_LORA)),
                  _const_spec((wq_p.shape[1], wq_p.shape[0])), _const_spec((1, KV_LORA)), _const_spec(wk_p.shape),
                  _const_spec(wvt_p.shape), _const_spec((vrows, 1)),
                  pl.BlockSpec((tm, B_QK_PAD), lambda bi, i: (i, 0)),
                  pl.BlockSpec((B_QK_PAD, tm), lambda bi, i: (0, i))],
        out_specs=[pl.BlockSpec((1, B_HEADS, tm // MLA_Q_TILE, B_QK_PAD, MLA_Q_TILE), lambda bi, i: (bi, 0, i, 0, 0)),
                   pl.BlockSpec((1, B_HEADS, tm, B_QK_PAD), lambda bi, i: (bi, 0, i, 0)),
                   pl.BlockSpec((1, tm // chunk, vrows, chunk), lambda bi, i: (bi, i, 0, 0)), row(B_WIDTH)],
        out_shape=[
            jax.ShapeDtypeStruct((b, B_HEADS, s // MLA_Q_TILE, B_QK_PAD, MLA_Q_TILE), BF16),
            jax.ShapeDtypeStruct((b, B_HEADS, s, B_QK_PAD), BF16),
            jax.ShapeDtypeStruct((b, s // chunk, vrows, chunk), BF16),
            jax.ShapeDtypeStruct((b, s, B_WIDTH), BF16),
        ],
        compiler_params=_params("parallel", "parallel"),
    )(h, g.reshape(1, d), w_in_p, q_norm.reshape(1, Q_LORA), wq_p.T, kv_norm.reshape(1, KV_LORA), wk_p,
      wvt_p, vbias, rot, rot.T)


def _prep_b_weights(w_in, w_qb, w_kvb):
    half = B_ROPE // 2
    d = w_in.shape[0]
    o1, o2, o3 = Q_LORA, Q_LORA + KV_LORA, Q_LORA + KV_LORA + B_ROPE
    kr = w_in[:, o2:o3]
    kr_swapped = jnp.concatenate([kr[:, half:], kr[:, :half]], axis=1)
    kr_block = jnp.concatenate([jnp.zeros((d, B_NOPE), w_in.dtype), kr, kr_swapped], axis=1)
    w_in_p = jnp.concatenate([w_in[:, :o2], kr_block, w_in[:, o3:]], axis=1).astype(BF16)

    wq = w_qb.reshape(Q_LORA, B_HEADS, B_NOPE + B_ROPE)
    q_rope = wq[:, :, B_NOPE:]
    q_rope_swapped = jnp.concatenate([q_rope[:, :, half:], q_rope[:, :, :half]], axis=2)
    wq_p = jnp.concatenate([wq[:, :, :B_NOPE], q_rope, q_rope_swapped], axis=2)
    wq_p = wq_p.reshape(Q_LORA, B_HEADS * B_QK_PAD).astype(BF16)

    wkv = w_kvb.reshape(KV_LORA, B_HEADS, B_NOPE + B_VDIM)
    wk_p = jnp.concatenate([wkv[:, :, :B_NOPE], jnp.zeros((KV_LORA, B_HEADS, B_QK_PAD - B_NOPE), w_kvb.dtype)],
                           axis=2).reshape(KV_LORA, B_HEADS * B_QK_PAD).astype(BF16)
    wv = jnp.concatenate([wkv[:, :, B_NOPE:], jnp.zeros((KV_LORA, B_HEADS, B_V_ROWS - B_VDIM), w_kvb.dtype)],
                         axis=2).reshape(KV_LORA, B_HEADS * B_V_ROWS)
    wvt_p = wv.T.astype(BF16)
    vbias = jnp.tile(jnp.concatenate([jnp.zeros((B_VDIM,), F32), jnp.ones((B_V_ROWS - B_VDIM,), F32)]),
                     B_HEADS).reshape(B_HEADS * B_V_ROWS, 1)
    return w_in_p, wq_p, wk_p, wvt_p, vbias


def _rope_table(s):
    half = B_ROPE // 2
    inv_freq = ROPE_THETA ** (-jnp.arange(half, dtype=F32) / half)
    ang = jnp.arange(s, dtype=F32)[:, None] * inv_freq[None, :]
    cos, sin = jnp.cos(ang), jnp.sin(ang)
    return jnp.concatenate([jnp.ones((s, B_NOPE), F32), cos, cos, -sin, sin], axis=1)


def _mla_kernel(q_ref, k_ref, vt_ref, z_ref, o_ref, s0_ref, s1_ref, p0_ref, p1_ref):
    n_kv, kv_tile = vt_ref.shape[1], vt_ref.shape[3]
    n_q, q_tile = q_ref.shape[2], q_ref.shape[4]
    s_bufs, p_bufs = (s0_ref, s1_ref), (p0_ref, p1_ref)
    heads = range(B_HEAD_PAIR)
    m_init = (jnp.full((1, q_tile), NEG_BIG, F32),) * B_HEAD_PAIR

    def q_rows(qt):
        return pl.ds(pl.multiple_of(qt * q_tile, q_tile), q_tile)

    def load_q(qt):
        return [q_ref[0, hh, qt] for hh in heads]

    def scores(q, j, slot, hh):
        kc = k_ref[0, hh, j * kv_tile:(j + 1) * kv_tile, :]
        s_bufs[slot][hh] = _dot(kc, q[hh])

    def exps(slot, m, hh):
        m_new = jnp.maximum(m[hh], jnp.max(s_bufs[slot][hh], axis=0, keepdims=True))
        p_bufs[slot][hh] = jnp.exp2(s_bufs[slot][hh] - m_new).astype(BF16)
        return m_new, jnp.exp2(m[hh] - m_new)

    def values(j, slot, acc, alpha, hh):
        vt = vt_ref[0, j, hh * B_V_ROWS:(hh + 1) * B_V_ROWS, :]
        return acc[hh] * alpha[hh] + _dot(vt, p_bufs[slot][hh])

    def step(q, j_s, j_v, acc, alpha, m):
        s_slot, e_slot, v_slot = j_s % 2, (j_s - 1) % 2, j_v % 2
        acc_out, m_out, alpha_out = [], [], []
        for hh in heads:
            scores(q, j_s % n_kv, s_slot, hh)
            acc_out.append(values(j_v, v_slot, acc, alpha, hh))
            m_new, a_new = exps(e_slot, m, hh)
            m_out.append(m_new)
            alpha_out.append(a_new)
        return tuple(acc_out), tuple(m_out), tuple(alpha_out)

    def tile_body(qt, carry):
        m, alpha = carry
        q = load_q(qt)
        q_next = load_q(jnp.minimum(qt + 1, n_q - 1))
        acc = (jnp.zeros((B_V_ROWS, q_tile), F32),) * B_HEAD_PAIR
        for t in range(2, n_kv):
            acc, m, alpha = step(q, t, t - 2, acc, alpha, m)
        acc, m, alpha = step(q_next, n_kv, n_kv - 2, acc, alpha, m)
        acc, m_next, alpha_next = step(q_next, n_kv + 1, n_kv - 1, acc, alpha, m_init)
        o_t = jnp.concatenate([acc[hh][:B_VDIM] / acc[hh][B_VDIM:B_VDIM + 1] for hh in heads], axis=0)
        z = z_ref[0, q_rows(qt), :].astype(F32)
        o_ref[0, q_rows(qt), :] = (o_t.T * (z * jax.nn.sigmoid(z))).astype(BF16)
        return m_next, alpha_next

    q0 = load_q(0)
    for hh in heads:
        scores(q0, 0, 0, hh)
    first = [exps(0, m_init, hh) for hh in heads]
    for hh in heads:
        scores(q0, 1, 1, hh)
    unroll = max(u for u in (4, 2, 1) if n_q % u == 0)

    def tiles(i, carry):
        for u in range(unroll):
            carry = tile_body(unroll * i + u, carry)
        return carry

    lax.fori_loop(0, n_q // unroll, tiles, (tuple(f[0] for f in first), tuple(f[1] for f in first)))


def _mla_attn(q_t, k, vt, z):
    b, _, s, _ = k.shape
    n_kv, kv_tile = vt.shape[1], vt.shape[3]
    assert n_kv >= 2 and n_kv % 2 == 0
    q_block = min(MLA_Q_BLOCK, s)
    q_tile = q_t.shape[-1]
    out_w = B_HEAD_PAIR * B_VDIM
    s_buf = pltpu.VMEM((B_HEAD_PAIR, kv_tile, q_tile), F32)
    p_buf = pltpu.VMEM((B_HEAD_PAIR, kv_tile, q_tile), BF16)
    return pl.pallas_call(
        _mla_kernel,
        grid=(b, B_HEADS // B_HEAD_PAIR, s // q_block),
        in_specs=[pl.BlockSpec((1, B_HEAD_PAIR, q_block // q_tile, B_QK_PAD, q_tile),
                               lambda bi, hp, qi: (bi, hp, qi, 0, 0)),
                  pl.BlockSpec((1, B_HEAD_PAIR, s, B_QK_PAD), lambda bi, hp, qi: (bi, hp, 0, 0)),
                  pl.BlockSpec((1, n_kv, B_HEAD_PAIR * B_V_ROWS, kv_tile), lambda bi, hp, qi: (bi, 0, hp, 0)),
                  pl.BlockSpec((1, q_block, out_w), lambda bi, hp, qi: (bi, qi, hp))],
        out_specs=pl.BlockSpec((1, q_block, out_w), lambda bi, hp, qi: (bi, qi, hp)),
        out_shape=jax.ShapeDtypeStruct((b, s, B_WIDTH), BF16),
        scratch_shapes=[s_buf, s_buf, p_buf, p_buf],
        compiler_params=_params("parallel", "parallel", "arbitrary"),
    )(q_t, k, vt, z)


def kernel(x, p, norm_g, a_w_in, a_sink, a_w_out, b_w_in, b_q_norm, b_w_qb, b_kv_norm, b_w_kvb, b_w_out,
           ple_w, ple_norm_g, ple_w_gate, final_norm_g):
    depth = p.shape[0]
    s = x.shape[1]
    rot = _rope_table(s)
    h = x
    for i in range(depth):
        j = i // 2
        if i % 2 == 0:
            q, k_exp, vt, z = _pre_a(h, norm_g[i], a_w_in[j])
            og = _win_attn(q, k_exp, vt, z, a_sink[j])
            w_out = a_w_out[j]
        else:
            w_in_p, wq_p, wk_p, wvt_p, vbias = _prep_b_weights(b_w_in[j], b_w_qb[j], b_w_kvb[j])
            q, k, vt, z = _pre_b(h, norm_g[i], w_in_p, b_q_norm[j], wq_p, b_kv_norm[j], wk_p, wvt_p, vbias, rot)
            og = _mla_attn(q, k, vt, z)
            w_out = b_w_out[j]
        h = _post(og, h, p, i, w_out, ple_norm_g[i], ple_w_gate[i], ple_w[i], final_norm_g, final=(i == depth - 1))
    return h
```
